```python
import math
import jax
import jax.numpy as jnp
from jax import lax
import numpy as np

D_MODEL = 1024
BATCH = 4
SEQ = 8192
DEPTH = 2

GRID_W = 64
CTX_LEN = 256
EPS = 1e-6
F32 = jnp.float32
HEAD_DIM = 64
GROUP_W = D_MODEL // 2
D_FF = 256 * ((8 * D_MODEL // 3 + 255) // 256)
N_MOD = 9

NA_HEADS = GROUP_W // HEAD_DIM
NA_WIN_R = 8
NA_WIN_C = 16
NA_QBLK = 16
NA_STRIP = NA_QBLK + NA_WIN_C

SSD_HEADS = GROUP_W // HEAD_DIM
SSD_P = HEAD_DIM
SSD_GROUPS = 2
SSD_STATE = 128
SSD_CONV = 4
SSD_CHUNK = 128
SSD_CONV_CH = GROUP_W + 2 * SSD_GROUPS * SSD_STATE

LRU_W = GROUP_W
LRU_BLOCKS = LRU_W // HEAD_DIM
LRU_BS = LRU_W // LRU_BLOCKS
LRU_CONV = 4
LRU_C = 8.0

DIFF_HEADS = GROUP_W // (2 * HEAD_DIM)
DIFF_QBLK = 128
ROPE_BASE = 10000.0

AB_SPLITS = (GROUP_W, GROUP_W, GROUP_W, GROUP_W, SSD_CONV_CH, 2 * SSD_HEADS)
AB_IN = sum(AB_SPLITS)
CD_SPLITS = (LRU_W, LRU_W, GROUP_W, GROUP_W, GROUP_W)
CD_IN = sum(CD_SPLITS)

kernel_name = 'hybrid_na_ssd_rglru_diffattn_prefix_dit'


def _split(u, sizes):
    return jnp.split(u, [int(s) for s in np.cumsum(sizes)[:-1]], axis=-1)


def rms_norm(x, g):
    xf = x.astype(F32)
    y = xf * lax.rsqrt(jnp.mean(xf * xf, axis=-1, keepdims=True) + EPS)
    return (y * g.astype(F32)).astype(x.dtype)


def _modulate(h, g, m, k):
    return (rms_norm(h, g) * (1 + m[:, :, 3 * k + 1]) + m[:, :, 3 * k]).astype(h.dtype)


def swiglu(h, w1, w3, w2):
    return (jax.nn.silu(h @ w1) * (h @ w3)) @ w2


def _half_ffn(h, g, m, k, w1, w3, w2):
    y = swiglu(_modulate(h, g, m, k), w1, w3, w2)
    return h + (0.5 * m[:, :, 3 * k + 2] * y).astype(h.dtype)


def dwconv_centred(x, w, b):
    K, C = w.shape
    left = K // 2
    y = lax.conv_general_dilated(x, w[:, None, :].astype(x.dtype), window_strides=(1,),
                                 padding=[(left, K - 1 - left)],
                                 dimension_numbers=('NWC', 'WIO', 'NWC'), feature_group_count=C)
    return y + b.astype(x.dtype)


def axial_rope(L, d):
    t = jnp.arange(L)
    row = (t // GRID_W).astype(F32)
    col = (t % GRID_W).astype(F32)
    n = d // 4
    inv = ROPE_BASE ** (-jnp.arange(n, dtype=F32) / n)
    ang = jnp.concatenate([row[:, None] * inv, col[:, None] * inv], axis=-1)
    return jnp.cos(ang), jnp.sin(ang)


def apply_rope(x, cos, sin):
    L = x.shape[1]
    shp = (1, L) + (1,) * (x.ndim - 3) + (cos.shape[-1],)
    cs, sn = cos.reshape(shp), sin.reshape(shp)
    xr = x.astype(F32).reshape(x.shape[:-1] + (-1, 2))
    x1, x2 = xr[..., 0], xr[..., 1]
    out = jnp.stack([x1 * cs - x2 * sn, x1 * sn + x2 * cs], axis=-1)
    return out.reshape(x.shape).astype(x.dtype)


def softmax_attn(q, k, v):
    s = jnp.einsum('bqhd,bkhd->bhqk', q, k, preferred_element_type=F32) * (q.shape[-1] ** -0.5)
    p = jax.nn.softmax(s, axis=-1)
    return jnp.einsum('bhqk,bkhd->bqhd', p, v.astype(F32)).astype(q.dtype)


def neighborhood_attention(q, k, v, kc, vc, rpb):
    Bsz, S, H, d = q.shape
    rows = S // GRID_W
    kr = min(NA_WIN_R, rows)
    ncb = GRID_W // NA_QBLK
    scale = d ** -0.5
    qcols = jnp.arange(GRID_W).reshape(ncb, NA_QBLK)
    qc0 = jnp.clip(qcols - NA_WIN_C // 2, 0, GRID_W - NA_WIN_C)
    strip0 = jnp.clip(jnp.arange(ncb) * NA_QBLK - NA_WIN_C // 2, 0, GRID_W - NA_STRIP)
    strip_cols = strip0[:, None] + jnp.arange(NA_STRIP)[None, :]
    kcol = strip_cols[:, None, :]
    col_in = (kcol >= qc0[..., None]) & (kcol < qc0[..., None] + NA_WIN_C)
    dc = jnp.clip(kcol - qcols[..., None] + NA_WIN_C - 1, 0, 2 * NA_WIN_C - 2)
    mask = col_in[None, None, :, :, None, :]

    def row_block(args):
        q_row, r = args
        r0 = jnp.clip(r - kr // 2, 0, rows - kr)
        k_rows = lax.dynamic_slice_in_dim(k, r0 * GRID_W, kr * GRID_W, axis=1).reshape(Bsz, kr, GRID_W, H, d)
        v_rows = lax.dynamic_slice_in_dim(v, r0 * GRID_W, kr * GRID_W, axis=1).reshape(Bsz, kr, GRID_W, H, d)
        k_s = jnp.take(k_rows, strip_cols, axis=2)
        v_s = jnp.take(v_rows, strip_cols, axis=2)
        qb = q_row.reshape(Bsz, ncb, NA_QBLK, H, d)
        dr = r0 + jnp.arange(kr) - r + NA_WIN_R - 1
        bias = rpb[:, dr[:, None, None, None], dc[None]].transpose(0, 2, 3, 1, 4)
        s_lat = jnp.einsum('bnqhd,bknshd->bhnqks', qb, k_s, preferred_element_type=F32) * scale + bias[None].astype(F32)
        s_lat = jnp.where(mask, s_lat, -jnp.inf).reshape(Bsz, H, ncb, NA_QBLK, kr * NA_STRIP)
        s_ctx = jnp.einsum('bnqhd,bchd->bhnqc', qb, kc, preferred_element_type=F32) * scale
        p = jax.nn.softmax(jnp.concatenate([s_lat, s_ctx], axis=-1), axis=-1)
        p_lat = p[..., :kr * NA_STRIP].reshape(Bsz, H, ncb, NA_QBLK, kr, NA_STRIP)
        p_ctx = p[..., kr * NA_STRIP:]
        o = (jnp.einsum('bhnqks,bknshd->bnqhd', p_lat, v_s.astype(F32))
             + jnp.einsum('bhnqc,bchd->bnqhd', p_ctx, vc.astype(F32)))
        return o.reshape(Bsz, GRID_W, H, d).astype(q.dtype)

    q_rows = jnp.moveaxis(q.reshape(Bsz, rows, GRID_W, H, d), 1, 0)
    o = lax.map(row_block, (q_rows, jnp.arange(rows)))
    return jnp.moveaxis(o, 0, 1).reshape(Bsz, S, H, d)


def ssd_scan(x, dt, A, Bm, Cm, h0):
    Bsz, L, H, P = x.shape
    G, N = Bm.shape[2], Bm.shape[3]
    R = H // G
    Q = SSD_CHUNK
    nc = L // Q
    x = x.astype(F32).reshape(Bsz, nc, Q, G, R, P)
    dt = dt.reshape(Bsz, nc, Q, G, R)
    Bm = Bm.astype(F32).reshape(Bsz, nc, Q, G, N)
    Cm = Cm.astype(F32).reshape(Bsz, nc, Q, G, N)
    a_cs = jnp.cumsum(dt * A.reshape(G, R), axis=2).transpose(0, 1, 3, 4, 2)
    tri = jnp.tril(jnp.ones((Q, Q), dtype=bool))
    decay_in = jnp.exp(jnp.where(tri, a_cs[..., :, None] - a_cs[..., None, :], -jnp.inf))
    xdt = x * dt[..., None]
    cb = jnp.einsum('bcign,bcjgn->bcgij', Cm, Bm)
    y_diag = jnp.einsum('bcgij,bcgrij,bcjgrp->bcigrp', cb, decay_in, xdt)
    decay_end = jnp.exp(a_cs[..., -1:] - a_cs)
    states = jnp.einsum('bcjgn,bcgrj,bcjgrp->bcgrpn', Bm, decay_end, xdt)
    chunk_decay = jnp.exp(a_cs[..., -1])

    def step(h, inp):
        s, dcy = inp
        return h * dcy[..., None, None] + s, h

    h_last, h_start = lax.scan(step, h0.astype(F32).reshape(Bsz, G, R, P, N),
                               (jnp.moveaxis(states, 1, 0), jnp.moveaxis(chunk_decay, 1, 0)))
    h_start = jnp.moveaxis(h_start, 0, 1)
    y_off = jnp.einsum('bcign,bcgrpn,bcgri->bcigrp', Cm, h_start, jnp.exp(a_cs))
    y = (y_diag + y_off).reshape(Bsz, L, H, P)
    return y, h_last.reshape(Bsz, H, P, N)


def ssd_branch(xbc, dt_raw, conv_w, conv_b, dt_bias, a_log, d_skip, h0_f, h0_b):
    Bsz, L, _ = xbc.shape
    xbc = jax.nn.silu(dwconv_centred(xbc, conv_w, conv_b))
    xs, bm, cm = _split(xbc, (GROUP_W, SSD_GROUPS * SSD_STATE, SSD_GROUPS * SSD_STATE))
    xs = xs.reshape(Bsz, L, SSD_HEADS, SSD_P)
    bm = bm.reshape(Bsz, L, SSD_GROUPS, SSD_STATE)
    cm = cm.reshape(Bsz, L, SSD_GROUPS, SSD_STATE)
    dt = jax.nn.softplus(dt_raw.astype(F32).reshape(Bsz, L, 2, SSD_HEADS) + dt_bias.astype(F32))
    A = -jnp.exp(a_log.astype(F32))
    flip = lambda t: jnp.flip(t, axis=1)
    y_f, hf = ssd_scan(xs, dt[:, :, 0], A[0], bm, cm, h0_f)
    y_b, hb = ssd_scan(flip(xs), flip(dt[:, :, 1]), A[1], flip(bm), flip(cm), h0_b)
    y = y_f + flip(y_b) + d_skip.astype(F32)[:, None] * xs.astype(F32)
    return y, hf, hb


def rglru_scan(x, wa, ba, wx, bx, lam, h0):
    Bsz, L, W = x.shape
    xf = x.astype(F32)
    xb = xf.reshape(Bsz, L, LRU_BLOCKS, LRU_BS)
    r = jax.nn.sigmoid(jnp.einsum('blnd,nde->blne', xb, wa.astype(F32)).reshape(Bsz, L, W) + ba.astype(F32))
    i = jax.nn.sigmoid(jnp.einsum('blnd,nde->blne', xb, wx.astype(F32)).reshape(Bsz, L, W) + bx.astype(F32))
    log_a = -LRU_C * r * jax.nn.softplus(-lam.astype(F32))
    a = jnp.exp(log_a)
    b = jnp.sqrt(-jnp.expm1(2 * log_a)) * (i * xf)
    b = b.at[:, 0].add(a[:, 0] * h0)

    def combine(u, v):
        return u[0] * v[0], v[0] * u[1] + v[1]

    _, h = lax.associative_scan(combine, (a, b), axis=1)
    return h, h[:, -1]


def rglru_bidir(x, wa, ba, wx, bx, lam, h0_f, h0_b):
    h_f, last_f = rglru_scan(x, wa[0], ba[0], wx[0], bx[0], lam[0], h0_f)
    h_b, last_b = rglru_scan(jnp.flip(x, axis=1), wa[1], ba[1], wx[1], bx[1], lam[1], h0_b)
    return h_f + jnp.flip(h_b, axis=1), last_f, last_b


def diff_attn_core(qb, k, v, lam):
    s = jnp.einsum('bqhtd,bkhtd->bhtqk', qb, k, preferred_element_type=F32) * (qb.shape[-1] ** -0.5)
    p = jax.nn.softmax(s, axis=-1)
    w = p[:, :, 0] - lam * p[:, :, 1]
    return jnp.einsum('bhqk,bkhe->bqhe', w, v.astype(F32))


def mixer_ab(hx, hc, w_in, w_out, q_g, k_g, rpb, conv_w, conv_b, dt_bias, a_log, d_skip, norm_g, with_ctx):
    Bsz, S, _ = hx.shape
    Lc = hc.shape[1]
    q, k, v, z, xbc, dt = _split(hx @ w_in, AB_SPLITS)
    q_c, k_c, v_c, z_c, xbc_c, dt_c = _split(hc @ w_in, AB_SPLITS)

    def heads(q, k, v):
        shp = (Bsz, q.shape[1], NA_HEADS, HEAD_DIM)
        return rms_norm(q.reshape(shp), q_g), rms_norm(k.reshape(shp), k_g), v.reshape(shp)

    q, k, v = heads(q, k, v)
    q_c, k_c, v_c = heads(q_c, k_c, v_c)
    o_na = neighborhood_attention(q, k, v, k_c, v_c, rpb)

    zeros = jnp.zeros((Bsz, SSD_HEADS, SSD_P, SSD_STATE), F32)
    y_c, hf_c, hb_c = ssd_branch(xbc_c, dt_c, conv_w, conv_b, dt_bias, a_log, d_skip, zeros, zeros)
    y_s, _, _ = ssd_branch(xbc, dt, conv_w, conv_b, dt_bias, a_log, d_skip, hf_c, hb_c)

    def gated(y, zz):
        return rms_norm(y.reshape(Bsz, y.shape[1], GROUP_W) * jax.nn.silu(zz.astype(F32)), norm_g).astype(hx.dtype)

    y = jnp.concatenate([o_na.reshape(Bsz, S, GROUP_W), gated(y_s, z)], axis=-1) @ w_out
    yc = None
    if with_ctx:
        o_c = softmax_attn(q_c, k_c, v_c)
        yc = jnp.concatenate([o_c.reshape(Bsz, Lc, GROUP_W), gated(y_c, z_c)], axis=-1) @ w_out
    return y, yc


def mixer_cd(hx, hc, w_in, w_out, conv_w, conv_b, wa, ba, wx, bx, lam_p, q_g, k_g, diff_lam, subln_g, lam_init, with_ctx):
    Bsz, S, _ = hx.shape
    Lc = hc.shape[1]
    gate, xr, q, k, v = _split(hx @ w_in, CD_SPLITS)
    gate_c, xr_c, q_c, k_c, v_c = _split(hc @ w_in, CD_SPLITS)

    xr_c = dwconv_centred(xr_c, conv_w, conv_b)
    xr = dwconv_centred(xr, conv_w, conv_b)
    zeros = jnp.zeros((Bsz, LRU_W), F32)
    h_c, hf_c, hb_c = rglru_bidir(xr_c, wa, ba, wx, bx, lam_p, zeros, zeros)
    h, _, _ = rglru_bidir(xr, wa, ba, wx, bx, lam_p, hf_c, hb_c)
    out_c = (jax.nn.gelu(gate.astype(F32)) * h).astype(hx.dtype)

    lam = (jnp.exp(jnp.sum(diff_lam[0].astype(F32) * diff_lam[1].astype(F32)))
           - jnp.exp(jnp.sum(diff_lam[2].astype(F32) * diff_lam[3].astype(F32))) + lam_init)

    def heads(q, k, v):
        L = q.shape[1]
        shp = (Bsz, L, DIFF_HEADS, 2, HEAD_DIM)
        return (rms_norm(q.reshape(shp), q_g), rms_norm(k.reshape(shp), k_g),
                v.reshape(Bsz, L, DIFF_HEADS, 2 * HEAD_DIM))

    q, k, v = heads(q, k, v)
    q_c, k_c, v_c = heads(q_c, k_c, v_c)
    cos, sin = axial_rope(S, HEAD_DIM)
    q = apply_rope(q, cos, sin)
    k = apply_rope(k, cos, sin)
    k_all = jnp.concatenate([k_c, k], axis=1)
    v_all = jnp.concatenate([v_c, v], axis=1)
    nb = S // DIFF_QBLK
    q_blocks = jnp.moveaxis(q.reshape(Bsz, nb, DIFF_QBLK, DIFF_HEADS, 2, HEAD_DIM), 1, 0)
    o = lax.map(lambda qb: diff_attn_core(qb, k_all, v_all, lam), q_blocks)
    o = jnp.moveaxis(o, 0, 1).reshape(Bsz, S, DIFF_HEADS, 2 * HEAD_DIM)
    o = (rms_norm(o, subln_g) * (1 - lam_init)).reshape(Bsz, S, GROUP_W).astype(hx.dtype)
    y = jnp.concatenate([out_c, o], axis=-1) @ w_out
    yc = None
    if with_ctx:
        oc = diff_attn_core(q_c, k_c, v_c, lam)
        oc = (rms_norm(oc, subln_g) * (1 - lam_init)).reshape(Bsz, Lc, GROUP_W).astype(hc.dtype)
        oc_lru = (jax.nn.gelu(gate_c.astype(F32)) * h_c).astype(hc.dtype)
        yc = jnp.concatenate([oc_lru, oc], axis=-1) @ w_out
    return y, yc


def setup_inputs(seed: int = 0) -> dict:
    key = jax.random.key(seed)
    ks = list(jax.random.split(key, 40))
    cnt = [0]

    def nk():
        cnt[0] += 1
        return ks[cnt[0] - 1]

    def nrm(shape, s):
        return jax.random.normal(nk(), shape, F32) * s

    D = D_MODEL
    NE = (DEPTH + 1) // 2
    NO = DEPTH // 2
    x = nrm((BATCH, SEQ, D), 1.0)
    c = nrm((BATCH, D), 1.0)
    ctx = nrm((BATCH, CTX_LEN, D), 1.0)
    c_ctx = nrm((D,), 1.0)
    w_mod = nrm((DEPTH, D, N_MOD * D), 0.5 * D ** -0.5)
    b_mod = nrm((DEPTH, N_MOD * D), 0.01)
    norm_g = 1.0 + nrm((DEPTH, 3, D), 0.02)
    ffn_w1 = nrm((DEPTH, 2, D, D_FF), D ** -0.5)
    ffn_w3 = nrm((DEPTH, 2, D, D_FF), D ** -0.5)
    ffn_w2 = nrm((DEPTH, 2, D_FF, D), D_FF ** -0.5)
    ab_w_in = nrm((NE, D, AB_IN), D ** -0.5)
    ab_w_out = nrm((NE, 2 * GROUP_W, D), (2 * GROUP_W) ** -0.5)
    na_q_g = 1.0 + nrm((NE, HEAD_DIM), 0.02)
    na_k_g = 1.0 + nrm((NE, HEAD_DIM), 0.02)
    na_rpb = nrm((NE, NA_HEADS, 2 * NA_WIN_R - 1, 2 * NA_WIN_C - 1), 0.2)
    ssd_conv_w = nrm((NE, SSD_CONV, SSD_CONV_CH), SSD_CONV ** -0.5)
    ssd_conv_b = nrm((NE, SSD_CONV_CH), 0.01)
    dt0 = jnp.exp(jax.random.uniform(nk(), (NE, 2, SSD_HEADS), F32, math.log(1e-3), math.log(1e-1)))
    ssd_dt_bias = dt0 + jnp.log(-jnp.expm1(-dt0))
    ssd_a_log = jnp.log(jax.random.uniform(nk(), (NE, 2, SSD_HEADS), F32, 1.0, 16.0))
    ssd_d = 1.0 + nrm((NE, SSD_HEADS), 0.1)
    ssd_norm_g = 1.0 + nrm((NE, GROUP_W), 0.02)
    cd_w_in = nrm((NO, D, CD_IN), D ** -0.5)
    cd_w_out = nrm((NO, 2 * GROUP_W, D), (2 * GROUP_W) ** -0.5)
    lru_conv_w = nrm((NO, LRU_CONV, LRU_W), LRU_CONV ** -0.5)
    lru_conv_b = nrm((NO, LRU_W), 0.01)
    lru_wa = nrm((NO, 2, LRU_BLOCKS, LRU_BS, LRU_BS), LRU_BS ** -0.5)
    lru_ba = nrm((NO, 2, LRU_W), 0.1)
    lru_wx = nrm((NO, 2, LRU_BLOCKS, LRU_BS, LRU_BS), LRU_BS ** -0.5)
    lru_bx = nrm((NO, 2, LRU_W), 0.1)
    u = jax.random.uniform(nk(), (NO, 2, LRU_W), F32, 0.9, 0.999) ** (1.0 / LRU_C)
    lru_lambda = jnp.log(u) - jnp.log1p(-u)
    diff_q_g = 1.0 + nrm((NO, HEAD_DIM), 0.02)
    diff_k_g = 1.0 + nrm((NO, HEAD_DIM), 0.02)
    diff_lambda = nrm((NO, 4, HEAD_DIM), 0.1)
    diff_subln_g = 1.0 + nrm((NO, 2 * HEAD_DIM), 0.02)
    return {'x': x, 'c': c, 'ctx': ctx, 'c_ctx': c_ctx, 'w_mod': w_mod, 'b_mod': b_mod, 'norm_g': norm_g,
            'ffn_w1': ffn_w1, 'ffn_w3': ffn_w3, 'ffn_w2': ffn_w2, 'ab_w_in': ab_w_in, 'ab_w_out': ab_w_out,
            'na_q_g': na_q_g, 'na_k_g': na_k_g, 'na_rpb': na_rpb, 'ssd_conv_w': ssd_conv_w,
            'ssd_conv_b': ssd_conv_b, 'ssd_dt_bias': ssd_dt_bias, 'ssd_a_log': ssd_a_log, 'ssd_d': ssd_d,
            'ssd_norm_g': ssd_norm_g, 'cd_w_in': cd_w_in, 'cd_w_out': cd_w_out, 'lru_conv_w': lru_conv_w,
            'lru_conv_b': lru_conv_b, 'lru_wa': lru_wa, 'lru_ba': lru_ba, 'lru_wx': lru_wx, 'lru_bx': lru_bx,
            'lru_lambda': lru_lambda, 'diff_q_g': diff_q_g, 'diff_k_g': diff_k_g, 'diff_lambda': diff_lambda,
            'diff_subln_g': diff_subln_g}


def reference(x, c, ctx, c_ctx, w_mod, b_mod, norm_g, ffn_w1, ffn_w3, ffn_w2, ab_w_in, ab_w_out,
              na_q_g, na_k_g, na_rpb, ssd_conv_w, ssd_conv_b, ssd_dt_bias, ssd_a_log, ssd_d, ssd_norm_g,
              cd_w_in, cd_w_out, lru_conv_w, lru_conv_b, lru_wa, lru_ba, lru_wx, lru_bx, lru_lambda,
              diff_q_g, diff_k_g, diff_lambda, diff_subln_g):
    Bsz = x.shape[0]
    sc = jax.nn.silu(c.astype(F32))
    scc = jax.nn.silu(c_ctx.astype(F32))
    xc = ctx
    for i in range(DEPTH):
        last = i == DEPTH - 1
        j = i // 2
        m = (sc @ w_mod[i].astype(F32) + b_mod[i].astype(F32)).reshape(Bsz, 1, N_MOD, D_MODEL)
        mc = (scc @ w_mod[i].astype(F32) + b_mod[i].astype(F32)).reshape(1, 1, N_MOD, D_MODEL)
        x = _half_ffn(x, norm_g[i, 0], m, 0, ffn_w1[i, 0], ffn_w3[i, 0], ffn_w2[i, 0])
        xc = _half_ffn(xc, norm_g[i, 0], mc, 0, ffn_w1[i, 0], ffn_w3[i, 0], ffn_w2[i, 0])
        hx = _modulate(x, norm_g[i, 1], m, 1)
        hc = _modulate(xc, norm_g[i, 1], mc, 1)
        if i % 2 == 0:
            y, yc = mixer_ab(hx, hc, ab_w_in[j], ab_w_out[j], na_q_g[j], na_k_g[j], na_rpb[j],
                             ssd_conv_w[j], ssd_conv_b[j], ssd_dt_bias[j], ssd_a_log[j], ssd_d[j],
                             ssd_norm_g[j], not last)
        else:
            lam_init = 0.8 - 0.6 * math.exp(-0.3 * i)
            y, yc = mixer_cd(hx, hc, cd_w_in[j], cd_w_out[j], lru_conv_w[j], lru_conv_b[j], lru_wa[j],
                             lru_ba[j], lru_wx[j], lru_bx[j], lru_lambda[j], diff_q_g[j], diff_k_g[j],
                             diff_lambda[j], diff_subln_g[j], lam_init, not last)
        x = x + (m[:, :, 5] * y).astype(x.dtype)
        x = _half_ffn(x, norm_g[i, 2], m, 2, ffn_w1[i, 1], ffn_w3[i, 1], ffn_w2[i, 1])
        if not last:
            xc = xc + (mc[:, :, 5] * yc).astype(xc.dtype)
            xc = _half_ffn(xc, norm_g[i, 2], mc, 2, ffn_w1[i, 1], ffn_w3[i, 1], ffn_w2[i, 1])
    return x
```

```python
import functools
import math

import jax
import jax.numpy as jnp
from jax import lax
from jax.experimental import pallas as pl
from jax.experimental.pallas import tpu as pltpu

F32 = jnp.float32
BF16 = jnp.bfloat16
HIGHEST = lax.Precision.HIGHEST

GRID_W = 64
EPS = 1e-6
HEAD_DIM = 64
N_MOD = 9
NA_WIN_R = 8
NA_WIN_C = 16
SSD_STATE = 128
SSD_CHUNK = 128
SSD_HEADS = 8
LRU_C = 8.0
ROPE_BASE = 10000.0
NEG_BIG = -1e30

LANES = 128
VMEM_LIMIT = 48 * 1024 * 1024


def _cp(*sem):
    return pltpu.CompilerParams(dimension_semantics=sem, vmem_limit_bytes=VMEM_LIMIT)


def _const_spec(shape):
    nd = len(shape)
    return pl.BlockSpec(shape, lambda *_: (0,) * nd, pipeline_mode=pl.Buffered(1))


def _dot(a, b):
    return jnp.dot(a, b, preferred_element_type=F32)


def _dot_nt(a, b):
    return lax.dot_general(a, b, (((1,), (1,)), ((), ())), preferred_element_type=F32)


def _sigmoid(x):
    return 1.0 / (1.0 + jnp.exp(-x))


def _silu(x):
    return x * _sigmoid(x)


def _softplus(x):
    return jnp.maximum(x, 0.0) + jnp.log(1.0 + jnp.exp(-jnp.abs(x)))


def _rms_mod(x, g, shift, scale):
    ms = jnp.mean(x * x, axis=-1, keepdims=True)
    return (x * lax.rsqrt(ms + EPS) * g) * (1.0 + scale) + shift


def _mod_kernel(c_ref, w_ref, b_ref, o_ref):
    s = _silu(c_ref[...])
    o_ref[...] = jnp.dot(s, w_ref[...], preferred_element_type=F32, precision=HIGHEST) + b_ref[...]


def _modulation(cc, w_mod, b_mod):
    depth, d, n = w_mod.shape
    tn = 1024
    return pl.pallas_call(
        _mod_kernel,
        grid=(depth, n // tn),
        in_specs=[pl.BlockSpec((8, d), lambda l, j: (0, 0)),
                  pl.BlockSpec((None, d, tn), lambda l, j: (l, 0, j)),
                  pl.BlockSpec((None, 1, tn), lambda l, j: (l, 0, j))],
        out_specs=pl.BlockSpec((None, 8, tn), lambda l, j: (l, 0, j)),
        out_shape=jax.ShapeDtypeStruct((depth, 8, n), F32),
        compiler_params=_cp("parallel", "parallel"),
        name="modulation",
    )(cc, w_mod, b_mod.reshape(depth, 1, n))


def _mod_spec(mod):
    if mod.shape[0] == 1:
        return pl.BlockSpec((None, N_MOD, mod.shape[2]), lambda b, i: (0, 0, 0))
    return pl.BlockSpec((None, N_MOD, mod.shape[2]), lambda b, i: (b, 0, 0))


def _ffn_kernel(x_ref, mod_ref, g_ref, w1_ref, w3_ref, w2_ref, o_ref, *, k, chunk):
    x = x_ref[...]
    h = _rms_mod(x, g_ref[...], mod_ref[3 * k:3 * k + 1, :], mod_ref[3 * k + 1:3 * k + 2, :]).astype(BF16)
    ff = w1_ref.shape[1]
    acc = jnp.zeros(x.shape, F32)
    for c0 in range(0, ff, chunk):
        a = _dot(h, w1_ref[:, c0:c0 + chunk])
        b = _dot(h, w3_ref[:, c0:c0 + chunk])
        acc = acc + _dot((_silu(a) * b).astype(BF16), w2_ref[c0:c0 + chunk, :])
    o_ref[...] = x + (0.5 * mod_ref[3 * k + 2:3 * k + 3, :]) * acc


def _ffn(x, mod, g, w1, w3, w2, k):
    bsz, length, d = x.shape
    ff = w1.shape[1]
    tm = min(512, length)
    return pl.pallas_call(
        functools.partial(_ffn_kernel, k=k, chunk=ff // 2),
        grid=(bsz, length // tm),
        in_specs=[pl.BlockSpec((None, tm, d), lambda b, i: (b, i, 0)),
                  _mod_spec(mod),
                  _const_spec((1, d)),
                  _const_spec((d, ff)), _const_spec((d, ff)), _const_spec((ff, d))],
        out_specs=pl.BlockSpec((None, tm, d), lambda b, i: (b, i, 0)),
        out_shape=jax.ShapeDtypeStruct(x.shape, F32),
        compiler_params=_cp("parallel", "parallel"),
        name="half_ffn",
    )(x, mod, g.reshape(1, d), w1, w3, w2)


def _norm_proj_kernel(x_ref, mod_ref, g_ref, w_ref, *o_refs, k, offs):
    h = _rms_mod(x_ref[...], g_ref[...], mod_ref[3 * k:3 * k + 1, :], mod_ref[3 * k + 1:3 * k + 2, :]).astype(BF16)
    for o_ref, off in zip(o_refs, offs):
        wdt = o_ref.shape[-1]
        o_ref[...] = _dot(h, w_ref[:, off:off + wdt]).astype(o_ref.dtype)


def _norm_proj(x, mod, g, w, k, widths, dtypes):
    bsz, length, d = x.shape
    tm = min(512, length)
    offs = tuple(int(sum(widths[:j])) for j in range(len(widths)))
    return pl.pallas_call(
        functools.partial(_norm_proj_kernel, k=k, offs=offs),
        grid=(bsz, length // tm),
        in_specs=[pl.BlockSpec((None, tm, d), lambda b, i: (b, i, 0)),
                  _mod_spec(mod),
                  _const_spec((1, d)),
                  _const_spec(w.shape)],
        out_specs=[pl.BlockSpec((None, tm, wd), lambda b, i: (b, i, 0)) for wd in widths],
        out_shape=[jax.ShapeDtypeStruct((bsz, length, wd), dt) for wd, dt in zip(widths, dtypes)],
        compiler_params=_cp("parallel", "parallel"),
        name="norm_proj",
    )(x, mod, g.reshape(1, d), w)


def _swap_pairs(x):
    n = x.shape[-1]
    lane = lax.broadcasted_iota(jnp.int32, x.shape, 1)
    return jnp.where(lane % 2 == 0, pltpu.roll(x, n - 1, axis=1), pltpu.roll(x, 1, axis=1))


def _headnorm_kernel(x_ref, g_ref, p_ref, *rest, scale, rope):
    o_ref = rest[-1]
    x = x_ref[...]
    sq = x * x
    hi = sq.astype(BF16)
    lo = (sq - hi.astype(F32)).astype(BF16)
    ms = _dot(hi, p_ref[...]) + _dot(lo, p_ref[...])
    y = x * lax.rsqrt(ms + EPS) * g_ref[...]
    if rope:
        cos_ref, sin_ref = rest[0], rest[1]
        reps = x.shape[-1] // cos_ref.shape[-1]
        cs = jnp.concatenate([cos_ref[...]] * reps, axis=1)
        sn = jnp.concatenate([sin_ref[...]] * reps, axis=1)
        y = y * cs + _swap_pairs(y) * sn
    o_ref[...] = (y * scale).astype(o_ref.dtype)


def _headnorm(x, g, pmat, scale, rope_tabs=None):
    bsz, length, w = x.shape
    tm = min(512, length)
    in_specs = [pl.BlockSpec((None, tm, w), lambda b, i: (b, i, 0)),
                _const_spec((1, w)), _const_spec(pmat.shape)]
    args = [x, jnp.tile(g.astype(F32), w // g.shape[0]).reshape(1, w), pmat]
    if rope_tabs is not None:
        tw = rope_tabs[0].shape[-1]
        in_specs += [pl.BlockSpec((tm, tw), lambda b, i: (i, 0))] * 2
        args += list(rope_tabs)
    return pl.pallas_call(
        functools.partial(_headnorm_kernel, scale=scale, rope=rope_tabs is not None),
        grid=(bsz, length // tm),
        in_specs=in_specs,
        out_specs=pl.BlockSpec((None, tm, w), lambda b, i: (b, i, 0)),
        out_shape=jax.ShapeDtypeStruct(x.shape, BF16),
        compiler_params=_cp("parallel", "parallel"),
        name="headnorm",
    )(*args)


def _group_mean_matrix(w, group):
    idx = jnp.arange(w) // group
    return jnp.where(idx[:, None] == idx[None, :], 1.0 / group, 0.0).astype(BF16)


def _rope_tables(length):
    t = jnp.arange(length)
    row = (t // GRID_W).astype(F32)
    col = (t % GRID_W).astype(F32)
    n = HEAD_DIM // 4
    inv = ROPE_BASE ** (-jnp.arange(n, dtype=F32) / n)
    ang = jnp.concatenate([row[:, None] * inv, col[:, None] * inv], axis=-1)
    cos = jnp.repeat(jnp.cos(ang), 2, axis=-1)
    sin = jnp.repeat(jnp.sin(ang), 2, axis=-1)
    sign = jnp.tile(jnp.array([-1.0, 1.0], F32), HEAD_DIM // 2)
    reps = LANES // HEAD_DIM
    return jnp.tile(cos, (1, reps)), jnp.tile(sin * sign, (1, reps))


def _half_masks(shape):
    lane = lax.broadcasted_iota(jnp.int32, shape, len(shape) - 1)
    return lane < HEAD_DIM, lane >= HEAD_DIM


def _softmax_pv(s_list, v_list):
    m = s_list[0].max(axis=-1, keepdims=True)
    for s in s_list[1:]:
        m = jnp.maximum(m, s.max(axis=-1, keepdims=True))
    acc, l = None, None
    for s, v in zip(s_list, v_list):
        p = jnp.exp(s - m)
        ls = p.sum(axis=-1, keepdims=True)
        o = _dot(p.astype(BF16), v)
        acc = o if acc is None else acc + o
        l = ls if l is None else l + ls
    return acc / l


def _na_kernel(q_ref, kp_ref, kc_ref, kn_ref, vp_ref, vc_ref, vn_ref, kctx_ref, vctx_ref, bias_ref,
               o_ref, kbuf, vbuf, *, rows_per_blk, n_rows):
    i = pl.program_id(1)
    blk = rows_per_blk * GRID_W
    kbuf[0:blk, :] = kp_ref[...]
    kbuf[blk:2 * blk, :] = kc_ref[...]
    kbuf[2 * blk:3 * blk, :] = kn_ref[...]
    vbuf[0:blk, :] = vp_ref[...]
    vbuf[blk:2 * blk, :] = vc_ref[...]
    vbuf[2 * blk:3 * blk, :] = vn_ref[...]
    win = NA_WIN_R * GRID_W
    n_pairs = q_ref.shape[-1] // LANES

    def row_body(j, carry):
        r = i * rows_per_blk + j
        r0 = jnp.clip(r - NA_WIN_R // 2, 0, n_rows - NA_WIN_R)
        off = pl.multiple_of((r0 - (i - 1) * rows_per_blk) * GRID_W, GRID_W)
        cfg = r0 - r + NA_WIN_R - 1
        qrow = q_ref[pl.ds(pl.multiple_of(j * GRID_W, GRID_W), GRID_W), :]
        kwin = kbuf[pl.ds(off, win), :]
        vwin = vbuf[pl.ds(off, win), :]
        outs = []
        for hp in range(n_pairs):
            sl = slice(hp * LANES, (hp + 1) * LANES)
            qp, kp, vp = qrow[:, sl], kwin[:, sl], vwin[:, sl]
            kcp, vcp = kctx_ref[:, sl], vctx_ref[:, sl]
            o_pair = None
            for t, mask in enumerate(_half_masks(qp.shape)):
                qm = jnp.where(mask, qp, jnp.zeros_like(qp))
                s_lat = _dot_nt(qm, kp) + bias_ref[cfg, 2 * hp + t]
                s_ctx = _dot_nt(qm, kcp)
                vm = jnp.where(_half_masks(vp.shape)[t], vp, jnp.zeros_like(vp))
                vcm = jnp.where(_half_masks(vcp.shape)[t], vcp, jnp.zeros_like(vcp))
                o_t = _softmax_pv([s_lat, s_ctx], [vm, vcm])
                o_pair = o_t if o_pair is None else o_pair + o_t
            outs.append(o_pair)
        o_ref[pl.ds(pl.multiple_of(j * GRID_W, GRID_W), GRID_W), :] = jnp.concatenate(outs, axis=1).astype(o_ref.dtype)
        return carry

    lax.fori_loop(0, rows_per_blk, row_body, 0)


def _na_bias_table(rpb):
    heads = rpb.shape[0]
    qc = jnp.arange(GRID_W)
    kc = jnp.arange(GRID_W)
    qc0 = jnp.clip(qc - NA_WIN_C // 2, 0, GRID_W - NA_WIN_C)
    col_in = (kc[None, :] >= qc0[:, None]) & (kc[None, :] < qc0[:, None] + NA_WIN_C)
    dc = jnp.clip(kc[None, :] - qc[:, None] + NA_WIN_C - 1, 0, 2 * NA_WIN_C - 2)
    cfg = jnp.arange(NA_WIN_R)
    kr = jnp.arange(NA_WIN_R)
    dr = cfg[:, None] + kr[None, :]
    tab = rpb.astype(F32)[:, dr[:, :, None, None], dc[None, None, :, :]]
    tab = jnp.where(col_in[None, None, None], tab, NEG_BIG)
    tab = tab.transpose(1, 0, 3, 2, 4)
    return tab.reshape(NA_WIN_R, heads, GRID_W, NA_WIN_R * GRID_W)


def _neighborhood_attention(q, k, v, kc, vc, bias):
    bsz, s, w = q.shape
    n_rows = s // GRID_W
    rpb_rows = NA_WIN_R
    blk = rpb_rows * GRID_W
    nb = s // blk
    lc = kc.shape[1]
    cur = pl.BlockSpec((None, blk, w), lambda b, i: (b, i, 0))
    prev = pl.BlockSpec((None, blk, w), lambda b, i: (b, jnp.maximum(i - 1, 0), 0))
    nxt = pl.BlockSpec((None, blk, w), lambda b, i: (b, jnp.minimum(i + 1, nb - 1), 0))
    ctx = pl.BlockSpec((None, lc, w), lambda b, i: (b, 0, 0))
    return pl.pallas_call(
        functools.partial(_na_kernel, rows_per_blk=rpb_rows, n_rows=n_rows),
        grid=(bsz, nb),
        in_specs=[cur, prev, cur, nxt, prev, cur, nxt, ctx, ctx, _const_spec(bias.shape)],
        out_specs=cur,
        out_shape=jax.ShapeDtypeStruct(q.shape, BF16),
        scratch_shapes=[pltpu.VMEM((3 * blk, w), BF16), pltpu.VMEM((3 * blk, w), BF16)],
        compiler_params=_cp("parallel", "arbitrary"),
        name="neighborhood_attention",
    )(q, k, k, k, v, v, v, kc, vc, bias)


def _ctx_attn_kernel(q_ref, k_ref, v_ref, o_ref):
    n_pairs = q_ref.shape[-1] // LANES
    outs = []
    for hp in range(n_pairs):
        sl = slice(hp * LANES, (hp + 1) * LANES)
        qp, kp, vp = q_ref[:, sl], k_ref[:, sl], v_ref[:, sl]
        o_pair = None
        for t, mask in enumerate(_half_masks(qp.shape)):
            qm = jnp.where(mask, qp, jnp.zeros_like(qp))
            vm = jnp.where(mask, vp, jnp.zeros_like(vp))
            o_t = _softmax_pv([_dot_nt(qm, kp)], [vm])
            o_pair = o_t if o_pair is None else o_pair + o_t
        outs.append(o_pair)
    o_ref[...] = jnp.concatenate(outs, axis=1).astype(o_ref.dtype)


def _ctx_attention(q, k, v):
    bsz, lc, w = q.shape
    spec = pl.BlockSpec((None, lc, w), lambda b: (b, 0, 0))
    return pl.pallas_call(
        _ctx_attn_kernel, grid=(bsz,), in_specs=[spec, spec, spec], out_specs=spec,
        out_shape=jax.ShapeDtypeStruct(q.shape, BF16),
        compiler_params=_cp("parallel"), name="ctx_attention",
    )(q, k, v)


def _conv_kernel(prev_ref, x_ref, next_ref, w_ref, b_ref, o_ref, *, act):
    i = pl.program_id(1)
    n = pl.num_programs(1)
    t = x_ref.shape[0]
    halo = prev_ref.shape[0]
    prev = jnp.where(i > 0, prev_ref[...], 0.0)
    nxt = jnp.where(i < n - 1, next_ref[...], 0.0)
    xx = jnp.concatenate([prev, x_ref[...], nxt], axis=0)
    taps = w_ref.shape[0]
    left = taps // 2
    tot = t + 2 * halo
    y = b_ref[...]
    for j in range(taps):
        sh = (left - j) % tot
        xs = xx if sh == 0 else pltpu.roll(xx, sh, axis=0)
        y = y + w_ref[j:j + 1, :] * xs[halo:halo + t]
    o_ref[...] = _silu(y) if act else y


def _dwconv(x, w, b, act):
    bsz, length, c = x.shape
    t = min(512, length)
    halo = 8
    per = t // halo
    nh = length // halo
    return pl.pallas_call(
        functools.partial(_conv_kernel, act=act),
        grid=(bsz, length // t),
        in_specs=[pl.BlockSpec((None, halo, c), lambda bb, i: (bb, jnp.maximum(i * per - 1, 0), 0)),
                  pl.BlockSpec((None, t, c), lambda bb, i: (bb, i, 0)),
                  pl.BlockSpec((None, halo, c), lambda bb, i: (bb, jnp.minimum((i + 1) * per, nh - 1), 0)),
                  _const_spec(w.shape), _const_spec((1, c))],
        out_specs=pl.BlockSpec((None, t, c), lambda bb, i: (bb, i, 0)),
        out_shape=jax.ShapeDtypeStruct(x.shape, F32),
        compiler_params=_cp("parallel", "parallel"),
        name="dwconv",
    )(x, x, x, w.astype(F32), b.astype(F32).reshape(1, c))


def _ssd_direction(xbc_ref, dtc_ref, dtr_ref, pc_ref, pr_ref, state, y_ref, *, d, reverse, add_skip):
    q = xbc_ref.shape[0]
    hp_w = LANES
    n_heads = SSD_HEADS
    gw = n_heads * HEAD_DIM
    ii = lax.broadcasted_iota(jnp.int32, (q, q), 0)
    jj = lax.broadcasted_iota(jnp.int32, (q, q), 1)
    keep = (ii <= jj) if reverse else (ii >= jj)
    tri = keep.astype(F32)
    dt_c = _softplus(dtc_ref[:, d * n_heads:(d + 1) * n_heads] + pc_ref[0:1, :])
    dt_r = _softplus(dtr_ref[d * n_heads:(d + 1) * n_heads, :] + pr_ref[:, 0:1])
    da_c = dt_c * pc_ref[1:2, :]
    da_r = dt_r * pr_ref[:, 1:2]
    acs_c = jnp.dot(tri, da_c, preferred_element_type=F32, precision=HIGHEST)
    acs_r = lax.dot_general(da_r, tri, (((1,), (1,)), ((), ())), preferred_element_type=F32,
                            precision=HIGHEST)
    edge = 0 if reverse else q - 1
    tot_r = acs_r[:, edge:edge + 1]
    e_in_c = jnp.exp(acs_c)
    e_end_r = jnp.exp(tot_r - acs_r)
    e_tot_r = jnp.exp(tot_r)
    lane_lo, lane_hi = _half_masks((q, hp_w))
    for g in range(2):
        bm = xbc_ref[:, gw + g * SSD_STATE:gw + (g + 1) * SSD_STATE]
        cm = xbc_ref[:, gw + 2 * SSD_STATE + g * SSD_STATE:gw + 2 * SSD_STATE + (g + 1) * SSD_STATE]
        bm_t = bm.T
        cm16 = cm.astype(BF16)
        cb = _dot(cm16, bm_t.astype(BF16))
        for pp in range(2):
            pair = g * 2 + pp
            h0 = 2 * pair
            sl = slice(pair * hp_w, (pair + 1) * hp_w)
            xs = xbc_ref[:, sl]
            dt_l = jnp.where(lane_lo, dt_c[:, h0:h0 + 1], dt_c[:, h0 + 1:h0 + 2])
            xdt = xs * dt_l
            y_pair = None
            st_new = None
            for t, lmask in enumerate((lane_lo, lane_hi)):
                h = h0 + t
                seg = acs_c[:, h:h + 1] - acs_r[h:h + 1, :]
                dec = jnp.where(keep, jnp.exp(jnp.where(keep, seg, 0.0)), 0.0)
                xm = jnp.where(lmask, xdt, 0.0).astype(BF16)
                yd = _dot((cb * dec).astype(BF16), xm)
                sn = _dot((bm_t * e_end_r[h:h + 1, :]).astype(BF16), xm)
                y_pair = yd if y_pair is None else y_pair + yd
                st_new = sn if st_new is None else st_new + sn
            st_old = state[:, sl]
            e_in = jnp.where(lane_lo, e_in_c[:, h0:h0 + 1], e_in_c[:, h0 + 1:h0 + 2])
            y_pair = y_pair + _dot(cm16, st_old.astype(BF16)) * e_in
            if add_skip:
                d_l = jnp.where(lane_lo[0:1], pc_ref[2:3, h0:h0 + 1], pc_ref[2:3, h0 + 1:h0 + 2])
                y_pair = y_pair + d_l * xs
            y_ref[:, sl] = y_pair
            e_tot = jnp.where(lane_lo[0:1], e_tot_r[h0:h0 + 1, :], e_tot_r[h0 + 1:h0 + 2, :])
            state[:, sl] = st_old * e_tot + st_new


def _ssd_kernel(xf_ref, xb_ref, dtcf_ref, dtcb_ref, dtrf_ref, dtrb_ref, pc_ref, pr_ref, h0f_ref, h0b_ref,
                yf_ref, yb_ref, hf_ref, hb_ref, sf, sb):
    c = pl.program_id(1)

    @pl.when(c == 0)
    def _():
        sf[...] = h0f_ref[...]
        sb[...] = h0b_ref[...]

    _ssd_direction(xf_ref, dtcf_ref, dtrf_ref, pc_ref.at[0], pr_ref.at[0], sf, yf_ref, d=0, reverse=False, add_skip=True)
    _ssd_direction(xb_ref, dtcb_ref, dtrb_ref, pc_ref.at[1], pr_ref.at[1], sb, yb_ref, d=1, reverse=True, add_skip=False)
    hf_ref[...] = sf[...]
    hb_ref[...] = sb[...]


def _ssd(xbc, dt, pc, pr, h0f, h0b):
    bsz, length, cw = xbc.shape
    q = SSD_CHUNK
    nc = length // q
    gw = SSD_HEADS * HEAD_DIM
    dtw = dt.shape[-1]
    dt_t = jnp.swapaxes(dt, 1, 2)
    fwd3 = lambda b, c: (b, c, 0)
    bwd3 = lambda b, c: (b, nc - 1 - c, 0)
    st_spec = pl.BlockSpec((None, SSD_STATE, gw), lambda b, c: (b, 0, 0))
    return pl.pallas_call(
        _ssd_kernel,
        grid=(bsz, nc),
        in_specs=[pl.BlockSpec((None, q, cw), fwd3), pl.BlockSpec((None, q, cw), bwd3),
                  pl.BlockSpec((None, q, dtw), fwd3), pl.BlockSpec((None, q, dtw), bwd3),
                  pl.BlockSpec((None, dtw, q), lambda b, c: (b, 0, c)),
                  pl.BlockSpec((None, dtw, q), lambda b, c: (b, 0, nc - 1 - c)),
                  _const_spec(pc.shape), _const_spec(pr.shape), st_spec, st_spec],
        out_specs=[pl.BlockSpec((None, q, gw), fwd3), pl.BlockSpec((None, q, gw), bwd3), st_spec, st_spec],
        out_shape=[jax.ShapeDtypeStruct((bsz, length, gw), F32)] * 2
                  + [jax.ShapeDtypeStruct((bsz, SSD_STATE, gw), F32)] * 2,
        scratch_shapes=[pltpu.VMEM((SSD_STATE, gw), F32)] * 2,
        compiler_params=_cp("parallel", "arbitrary"),
        name="ssd_scan",
    )(xbc, xbc, dt, dt, dt_t, dt_t, pc, pr, h0f, h0b)


def _gelu_tanh(x):
    return 0.5 * x * (1.0 + jnp.tanh(math.sqrt(2.0 / math.pi) * (x + 0.044715 * (x * x * x))))


def _out_ab_kernel(x_ref, mod_ref, ona_ref, yf_ref, yb_ref, z_ref, ng_ref, w_ref, o_ref):
    gw = ona_ref.shape[-1]
    y = (yf_ref[...] + yb_ref[...]) * _silu(z_ref[...])
    gated = y * lax.rsqrt(jnp.mean(y * y, axis=-1, keepdims=True) + EPS) * ng_ref[...]
    out = _dot(ona_ref[...], w_ref[0:gw, :]) + _dot(gated.astype(BF16), w_ref[gw:2 * gw, :])
    o_ref[...] = x_ref[...] + mod_ref[5:6, :] * out


def _out_cd_kernel(x_ref, mod_ref, gate_ref, hf_ref, hb_ref, od_ref, w_ref, o_ref):
    gw = od_ref.shape[-1]
    lru = _gelu_tanh(gate_ref[...]) * (hf_ref[...] + hb_ref[...])
    out = _dot(lru.astype(BF16), w_ref[0:gw, :]) + _dot(od_ref[...], w_ref[gw:2 * gw, :])
    o_ref[...] = x_ref[...] + mod_ref[5:6, :] * out


def _mixer_out(kern, name, x, mod, parts, consts):
    bsz, length, d = x.shape
    tm = min(512, length)
    tok = lambda a: pl.BlockSpec((None, tm, a.shape[-1]), lambda b, i: (b, i, 0))
    return pl.pallas_call(
        kern,
        grid=(bsz, length // tm),
        in_specs=[tok(x), _mod_spec(mod)] + [tok(a) for a in parts] + [_const_spec(a.shape) for a in consts],
        out_specs=tok(x),
        out_shape=jax.ShapeDtypeStruct(x.shape, F32),
        compiler_params=_cp("parallel", "parallel"),
        name=name,
    )(x, mod, *parts, *consts)


def _lru_direction(x_ref, wa_ref, wx_ref, p_ref, carry, h_ref, *, reverse):
    t = x_ref.shape[0]
    x = x_ref[...]
    x16 = x.astype(BF16)
    r = _sigmoid(_dot(x16, wa_ref[...]) + p_ref[0:1, :])
    ig = _sigmoid(_dot(x16, wx_ref[...]) + p_ref[1:2, :])
    log_a = -LRU_C * r * _softplus(-p_ref[2:3, :])
    a = jnp.exp(log_a)
    b = jnp.sqrt(1.0 - a * a) * (ig * x)
    row = lax.broadcasted_iota(jnp.int32, a.shape, 0)
    s = 1
    while s < t:
        if reverse:
            fill = row >= t - s
            a_sh = jnp.where(fill, 1.0, pltpu.roll(a, t - s, axis=0))
            b_sh = jnp.where(fill, 0.0, pltpu.roll(b, t - s, axis=0))
        else:
            fill = row < s
            a_sh = jnp.where(fill, 1.0, pltpu.roll(a, s, axis=0))
            b_sh = jnp.where(fill, 0.0, pltpu.roll(b, s, axis=0))
        b = a * b_sh + b
        a = a * a_sh
        s *= 2
    h = a * carry[...] + b
    h_ref[...] = h
    carry[...] = h[0:1, :] if reverse else h[t - 1:t, :]


def _lru_kernel(xf_ref, xb_ref, wa_ref, wx_ref, p_ref, h0f_ref, h0b_ref, hf_ref, hb_ref, lf_ref, lb_ref, cf, cb):
    c = pl.program_id(1)

    @pl.when(c == 0)
    def _():
        cf[...] = h0f_ref[...]
        cb[...] = h0b_ref[...]

    _lru_direction(xf_ref, wa_ref.at[0], wx_ref.at[0], p_ref.at[0], cf, hf_ref, reverse=False)
    _lru_direction(xb_ref, wa_ref.at[1], wx_ref.at[1], p_ref.at[1], cb, hb_ref, reverse=True)
    lf_ref[...] = cf[...]
    lb_ref[...] = cb[...]


def _lru(x, wa, wx, p, h0f, h0b):
    bsz, length, w = x.shape
    t = min(256, length)
    nt = length // t
    fwd = lambda b, c: (b, c, 0)
    bwd = lambda b, c: (b, nt - 1 - c, 0)
    st = pl.BlockSpec((None, 1, w), lambda b, c: (b, 0, 0))
    return pl.pallas_call(
        _lru_kernel,
        grid=(bsz, nt),
        in_specs=[pl.BlockSpec((None, t, w), fwd), pl.BlockSpec((None, t, w), bwd),
                  _const_spec(wa.shape), _const_spec(wx.shape), _const_spec(p.shape), st, st],
        out_specs=[pl.BlockSpec((None, t, w), fwd), pl.BlockSpec((None, t, w), bwd), st, st],
        out_shape=[jax.ShapeDtypeStruct(x.shape, F32)] * 2 + [jax.ShapeDtypeStruct((bsz, 1, w), F32)] * 2,
        scratch_shapes=[pltpu.VMEM((1, w), F32)] * 2,
        compiler_params=_cp("parallel", "arbitrary"),
        name="rglru_scan",
    )(x, x, wa, wx, p, h0f, h0b)


def _block_diag(wb):
    nb, bs, _ = wb.shape
    eye = jnp.eye(nb, dtype=wb.dtype)
    return (wb[:, :, None, :] * eye[:, None, :, None]).reshape(nb * bs, nb * bs)


def _diff_attn_kernel(q_ref, k_ref, v_ref, lam_ref, sg_ref, o_ref, m_s, l_s, acc_s, *, lam_init, n_heads):
    kv = pl.program_id(2)

    @pl.when(kv == 0)
    def _():
        m_s[...] = jnp.full(m_s.shape, NEG_BIG, F32)
        l_s[...] = jnp.zeros(l_s.shape, F32)
        acc_s[...] = jnp.zeros(acc_s.shape, F32)

    for h in range(n_heads):
        sl = slice(h * LANES, (h + 1) * LANES)
        qp, kp, vp = q_ref[:, sl], k_ref[:, sl], v_ref[:, sl]
        for t, mask in enumerate(_half_masks(qp.shape)):
            idx = 2 * h + t
            s = _dot_nt(jnp.where(mask, qp, jnp.zeros_like(qp)), kp)
            m_old = m_s[idx]
            m_new = jnp.maximum(m_old, s.max(axis=-1, keepdims=True))
            alpha = jnp.exp(m_old - m_new)
            p = jnp.exp(s - m_new)
            l_s[idx] = alpha * l_s[idx] + p.sum(axis=-1, keepdims=True)
            acc_s[idx] = alpha * acc_s[idx] + _dot(p.astype(BF16), vp)
            m_s[idx] = m_new

    @pl.when(kv == pl.num_programs(2) - 1)
    def _():
        dl = lam_ref[...]
        lam = (jnp.exp(jnp.sum(dl[0:1] * dl[1:2], axis=-1, keepdims=True))
               - jnp.exp(jnp.sum(dl[2:3] * dl[3:4], axis=-1, keepdims=True)) + lam_init)
        for h in range(n_heads):
            o = acc_s[2 * h] / l_s[2 * h] - lam * (acc_s[2 * h + 1] / l_s[2 * h + 1])
            o = o * lax.rsqrt(jnp.mean(o * o, axis=-1, keepdims=True) + EPS) * sg_ref[...] * (1.0 - lam_init)
            o_ref[:, h * LANES:(h + 1) * LANES] = o.astype(o_ref.dtype)


def _diff_attention(q, k_all, v_all, diff_lam, subln_g, lam_init):
    bsz, s, w = q.shape
    lk = k_all.shape[1]
    n_heads = w // LANES
    tq = min(512, s)
    tk = next(c for c in (768, 512, 256, 128) if lk % c == 0)
    return pl.pallas_call(
        functools.partial(_diff_attn_kernel, lam_init=lam_init, n_heads=n_heads),
        grid=(bsz, s // tq, lk // tk),
        in_specs=[pl.BlockSpec((None, tq, w), lambda b, i, j: (b, i, 0)),
                  pl.BlockSpec((None, tk, w), lambda b, i, j: (b, j, 0)),
                  pl.BlockSpec((None, tk, w), lambda b, i, j: (b, j, 0)),
                  pl.BlockSpec(diff_lam.shape, lambda b, i, j: (0, 0)),
                  pl.BlockSpec((1, LANES), lambda b, i, j: (0, 0))],
        out_specs=pl.BlockSpec((None, tq, w), lambda b, i, j: (b, i, 0)),
        out_shape=jax.ShapeDtypeStruct(q.shape, BF16),
        scratch_shapes=[pltpu.VMEM((2 * n_heads, tq, 1), F32), pltpu.VMEM((2 * n_heads, tq, 1), F32),
                        pltpu.VMEM((2 * n_heads, tq, LANES), F32)],
        compiler_params=_cp("parallel", "parallel", "arbitrary"),
        name="diff_attention",
    )(q, k_all, v_all, diff_lam.astype(F32), subln_g.astype(F32).reshape(1, LANES))


def _pad_cols(w, total):
    return jnp.pad(w, ((0, 0), (0, total - w.shape[1])))


def _layer_ab(x, xc, m, mc, g_mix, w_in, w_out, q_g, k_g, rpb, conv_w, conv_b, dt_bias, a_log, d_skip, norm_g):
    gw = SSD_HEADS * HEAD_DIM
    widths = (gw, gw, gw, gw, 2 * gw, LANES)
    dtypes = (F32, F32, BF16, F32, F32, F32)
    w16 = _pad_cols(w_in, sum(widths)).astype(BF16)
    q, k, v, z, xbc, dt = _norm_proj(x, m, g_mix, w16, 1, widths, dtypes)
    q_c, k_c, v_c, z_c, xbc_c, dt_c = _norm_proj(xc, mc, g_mix, w16, 1, widths, dtypes)

    pmat = _group_mean_matrix(gw, HEAD_DIM)
    scale = HEAD_DIM ** -0.5
    q = _headnorm(q, q_g, pmat, scale)
    k = _headnorm(k, k_g, pmat, 1.0)
    q_c = _headnorm(q_c, q_g, pmat, scale)
    k_c = _headnorm(k_c, k_g, pmat, 1.0)
    o_na = _neighborhood_attention(q, k, v, k_c, v_c, _na_bias_table(rpb))
    o_c = _ctx_attention(q_c, k_c, v_c)

    a_neg = -jnp.exp(a_log.astype(F32))
    pc = jnp.stack([dt_bias.astype(F32), a_neg, jnp.broadcast_to(d_skip.astype(F32), a_neg.shape)], axis=1)
    pr = jnp.swapaxes(pc, 1, 2)
    cw = conv_w.astype(F32)
    xbc_c = _dwconv(xbc_c, cw, conv_b, act=True)
    xbc = _dwconv(xbc, cw, conv_b, act=True)
    zeros = jnp.zeros((x.shape[0], SSD_STATE, gw), F32)
    yf_c, yb_c, hf_c, hb_c = _ssd(xbc_c, dt_c, pc, pr, zeros, zeros)
    yf, yb, _, _ = _ssd(xbc, dt, pc, pr, hf_c, hb_c)

    ng = norm_g.astype(F32).reshape(1, gw)
    wo16 = w_out.astype(BF16)
    x = _mixer_out(_out_ab_kernel, "mixer_out_ab", x, m, [o_na, yf, yb, z], [ng, wo16])
    xc = _mixer_out(_out_ab_kernel, "mixer_out_ab", xc, mc, [o_c, yf_c, yb_c, z_c], [ng, wo16])
    return x, xc


def _layer_cd(x, xc, m, mc, g_mix, w_in, w_out, conv_w, conv_b, wa, ba, wx, bx, lam_p, q_g, k_g, diff_lam,
              subln_g, lam_init):
    gw = w_out.shape[0] // 2
    widths = (gw,) * 5
    w16 = w_in.astype(BF16)
    gate, xr, q, k, v = _norm_proj(x, m, g_mix, w16, 1, widths, (F32, F32, F32, F32, BF16))
    _, xr_c, _, k_c, v_c = _norm_proj(xc, mc, g_mix, w16, 1, widths, (F32, F32, F32, F32, BF16))

    cw = conv_w.astype(F32)
    xr_c = _dwconv(xr_c, cw, conv_b, act=False)
    xr = _dwconv(xr, cw, conv_b, act=False)
    wa_d = jnp.stack([_block_diag(wa[0]), _block_diag(wa[1])]).astype(BF16)
    wx_d = jnp.stack([_block_diag(wx[0]), _block_diag(wx[1])]).astype(BF16)
    p = jnp.stack([ba.astype(F32), bx.astype(F32), lam_p.astype(F32)], axis=1)
    zeros = jnp.zeros((x.shape[0], 1, gw), F32)
    _, _, lf_c, lb_c = _lru(xr_c, wa_d, wx_d, p, zeros, zeros)
    hf, hb, _, _ = _lru(xr, wa_d, wx_d, p, lf_c, lb_c)

    pmat = _group_mean_matrix(gw, HEAD_DIM)
    tabs = _rope_tables(x.shape[1])
    q = _headnorm(q, q_g, pmat, HEAD_DIM ** -0.5, tabs)
    k = _headnorm(k, k_g, pmat, 1.0, tabs)
    k_c = _headnorm(k_c, k_g, pmat, 1.0)
    k_all = jnp.concatenate([k_c, k], axis=1)
    v_all = jnp.concatenate([v_c, v], axis=1)
    o = _diff_attention(q, k_all, v_all, diff_lam, subln_g, lam_init)
    return _mixer_out(_out_cd_kernel, "mixer_out_cd", x, m, [gate, hf, hb, o], [w_out.astype(BF16)])


def kernel(x, c, ctx, c_ctx, w_mod, b_mod, norm_g, ffn_w1, ffn_w3, ffn_w2, ab_w_in, ab_w_out, na_q_g, na_k_g, na_rpb, ssd_conv_w, ssd_conv_b, ssd_dt_bias, ssd_a_log, ssd_d, ssd_norm_g, cd_w_in, cd_w_out, lru_conv_w, lru_conv_b, lru_wa, lru_ba, lru_wx, lru_bx, lru_lambda, diff_q_g, diff_k_g, diff_lambda, diff_subln_g):
    bsz, _, d = x.shape
    depth = w_mod.shape[0]
    cc = jnp.concatenate([c.astype(F32), c_ctx.astype(F32)[None], jnp.zeros((8 - bsz - 1, d), F32)], axis=0)
    mods = _modulation(cc, w_mod.astype(F32), b_mod.astype(F32))
    xc = ctx
    for i in range(depth):
        last = i == depth - 1
        j = i // 2
        m = mods[i, :bsz].reshape(bsz, N_MOD, d)
        mc = mods[i, bsz:bsz + 1].reshape(1, N_MOD, d)
        g = norm_g[i].astype(F32)
        w1 = ffn_w1[i].astype(BF16)
        w3 = ffn_w3[i].astype(BF16)
        w2 = ffn_w2[i].astype(BF16)
        x = _ffn(x, m, g[0], w1[0], w3[0], w2[0], 0)
        xc = _ffn(xc, mc, g[0], w1[0], w3[0], w2[0], 0)
        if i % 2 == 0:
            x, xc = _layer_ab(x, xc, m, mc, g[1], ab_w_in[j], ab_w_out[j], na_q_g[j], na_k_g[j], na_rpb[j],
                              ssd_conv_w[j], ssd_conv_b[j], ssd_dt_bias[j], ssd_a_log[j], ssd_d[j], ssd_norm_g[j])
        else:
            lam_init = 0.8 - 0.6 * math.exp(-0.3 * i)
            x = _layer_cd(x, xc, m, mc, g[1], cd_w_in[j], cd_w_out[j], lru_conv_w[j], lru_conv_b[j], lru_wa[j],
                          lru_ba[j], lru_wx[j], lru_bx[j], lru_lambda[j], diff_q_g[j], diff_k_g[j],
                          diff_lambda[j], diff_subln_g[j], lam_init)
            xc = None
        x = _ffn(x, m, g[2], w1[1], w3[1], w2[1], 2)
        if not last and xc is not None:
            xc = _ffn(xc, mc, g[2], w1[1], w3[1], w2[1], 2)
    return x
```

```python
import functools
import math

import jax
import jax.numpy as jnp
from jax import lax
from jax.experimental import pallas as pl
from jax.experimental.pallas import tpu as pltpu

F32 = jnp.float32
BF16 = jnp.bfloat16
HIGHEST = lax.Precision.HIGHEST

GRID_W = 64
EPS = 1e-6
HEAD_DIM = 64
N_MOD = 9
NA_WIN_R = 8
NA_WIN_C = 16
SSD_STATE = 128
SSD_CHUNK = 128
SSD_HEADS = 8
LRU_C = 8.0
ROPE_BASE = 10000.0
NEG_BIG = -1e30

LANES = 128
VMEM_LIMIT = 48 * 1024 * 1024
DIFF_VMEM_LIMIT = 56 * 1024 * 1024
DIFF_TQ = 512
LOG2E = math.log2(math.e)


def _cp(*sem):
    return pltpu.CompilerParams(dimension_semantics=sem, vmem_limit_bytes=VMEM_LIMIT)


def _const_spec(shape):
    nd = len(shape)
    return pl.BlockSpec(shape, lambda *_: (0,) * nd, pipeline_mode=pl.Buffered(1))


def _dot(a, b):
    return jnp.dot(a, b, preferred_element_type=F32)


def _dot_nt(a, b):
    return lax.dot_general(a, b, (((1,), (1,)), ((), ())), preferred_element_type=F32)


def _sigmoid(x):
    return 1.0 / (1.0 + jnp.exp(-x))


def _silu(x):
    return x * _sigmoid(x)


def _softplus(x):
    return jnp.maximum(x, 0.0) + jnp.log(1.0 + jnp.exp(-jnp.abs(x)))


def _rms_mod(x, g, shift, scale):
    ms = jnp.mean(x * x, axis=-1, keepdims=True)
    return (x * lax.rsqrt(ms + EPS) * g) * (1.0 + scale) + shift


def _mod_kernel(c_ref, w_ref, b_ref, o_ref):
    s = _silu(c_ref[...])
    o_ref[...] = jnp.dot(s, w_ref[...], preferred_element_type=F32, precision=HIGHEST) + b_ref[...]


def _modulation(cc, w_mod, b_mod):
    depth, d, n = w_mod.shape
    tn = 1024
    return pl.pallas_call(
        _mod_kernel,
        grid=(depth, n // tn),
        in_specs=[pl.BlockSpec((8, d), lambda l, j: (0, 0)),
                  pl.BlockSpec((None, d, tn), lambda l, j: (l, 0, j)),
                  pl.BlockSpec((None, 1, tn), lambda l, j: (l, 0, j))],
        out_specs=pl.BlockSpec((None, 8, tn), lambda l, j: (l, 0, j)),
        out_shape=jax.ShapeDtypeStruct((depth, 8, n), F32),
        compiler_params=_cp("parallel", "parallel"),
        name="modulation",
    )(cc, w_mod, b_mod.reshape(depth, 1, n))


def _mod_spec(mod):
    if mod.shape[0] == 1:
        return pl.BlockSpec((None, N_MOD, mod.shape[2]), lambda b, i: (0, 0, 0))
    return pl.BlockSpec((None, N_MOD, mod.shape[2]), lambda b, i: (b, 0, 0))


def _ffn_kernel(x_ref, mod_ref, g_ref, w1_ref, w3_ref, w2_ref, o_ref, *, k, chunk):
    x = x_ref[...]
    h = _rms_mod(x, g_ref[...], mod_ref[3 * k:3 * k + 1, :], mod_ref[3 * k + 1:3 * k + 2, :]).astype(BF16)
    ff = w1_ref.shape[1]
    acc = jnp.zeros(x.shape, F32)
    for c0 in range(0, ff, chunk):
        a = _dot(h, w1_ref[:, c0:c0 + chunk])
        b = _dot(h, w3_ref[:, c0:c0 + chunk])
        acc = acc + _dot((_silu(a) * b).astype(BF16), w2_ref[c0:c0 + chunk, :])
    o_ref[...] = x + (0.5 * mod_ref[3 * k + 2:3 * k + 3, :]) * acc


def _ffn(x, mod, g, w1, w3, w2, k):
    bsz, length, d = x.shape
    ff = w1.shape[1]
    tm = min(512, length)
    return pl.pallas_call(
        functools.partial(_ffn_kernel, k=k, chunk=ff // 2),
        grid=(bsz, length // tm),
        in_specs=[pl.BlockSpec((None, tm, d), lambda b, i: (b, i, 0)),
                  _mod_spec(mod),
                  _const_spec((1, d)),
                  _const_spec((d, ff)), _const_spec((d, ff)), _const_spec((ff, d))],
        out_specs=pl.BlockSpec((None, tm, d), lambda b, i: (b, i, 0)),
        out_shape=jax.ShapeDtypeStruct(x.shape, F32),
        compiler_params=_cp("parallel", "parallel"),
        name="half_ffn",
    )(x, mod, g.reshape(1, d), w1, w3, w2)


def _norm_proj_kernel(x_ref, mod_ref, g_ref, w_ref, *o_refs, k, offs):
    h = _rms_mod(x_ref[...], g_ref[...], mod_ref[3 * k:3 * k + 1, :], mod_ref[3 * k + 1:3 * k + 2, :]).astype(BF16)
    for o_ref, off in zip(o_refs, offs):
        wdt = o_ref.shape[-1]
        o_ref[...] = _dot(h, w_ref[:, off:off + wdt]).astype(o_ref.dtype)


def _norm_proj(x, mod, g, w, k, widths, dtypes):
    bsz, length, d = x.shape
    tm = min(512, length)
    offs = tuple(int(sum(widths[:j])) for j in range(len(widths)))
    return pl.pallas_call(
        functools.partial(_norm_proj_kernel, k=k, offs=offs),
        grid=(bsz, length // tm),
        in_specs=[pl.BlockSpec((None, tm, d), lambda b, i: (b, i, 0)),
                  _mod_spec(mod),
                  _const_spec((1, d)),
                  _const_spec(w.shape)],
        out_specs=[pl.BlockSpec((None, tm, wd), lambda b, i: (b, i, 0)) for wd in widths],
        out_shape=[jax.ShapeDtypeStruct((bsz, length, wd), dt) for wd, dt in zip(widths, dtypes)],
        compiler_params=_cp("parallel", "parallel"),
        name="norm_proj",
    )(x, mod, g.reshape(1, d), w)


def _swap_pairs(x):
    n = x.shape[-1]
    lane = lax.broadcasted_iota(jnp.int32, x.shape, 1)
    return jnp.where(lane % 2 == 0, pltpu.roll(x, n - 1, axis=1), pltpu.roll(x, 1, axis=1))


def _headnorm_kernel(x_ref, g_ref, p_ref, *rest, scale, rope):
    o_ref = rest[-1]
    x = x_ref[...]
    sq = x * x
    hi = sq.astype(BF16)
    lo = (sq - hi.astype(F32)).astype(BF16)
    ms = _dot(hi, p_ref[...]) + _dot(lo, p_ref[...])
    y = x * lax.rsqrt(ms + EPS) * g_ref[...]
    if rope:
        cos_ref, sin_ref = rest[0], rest[1]
        reps = x.shape[-1] // cos_ref.shape[-1]
        cs = jnp.concatenate([cos_ref[...]] * reps, axis=1)
        sn = jnp.concatenate([sin_ref[...]] * reps, axis=1)
        y = y * cs + _swap_pairs(y) * sn
    o_ref[...] = (y * scale).astype(o_ref.dtype)


def _headnorm(x, g, pmat, scale, rope_tabs=None):
    bsz, length, w = x.shape
    tm = min(512, length)
    in_specs = [pl.BlockSpec((None, tm, w), lambda b, i: (b, i, 0)),
                _const_spec((1, w)), _const_spec(pmat.shape)]
    args = [x, jnp.tile(g.astype(F32), w // g.shape[0]).reshape(1, w), pmat]
    if rope_tabs is not None:
        tw = rope_tabs[0].shape[-1]
        in_specs += [pl.BlockSpec((tm, tw), lambda b, i: (i, 0))] * 2
        args += list(rope_tabs)
    return pl.pallas_call(
        functools.partial(_headnorm_kernel, scale=scale, rope=rope_tabs is not None),
        grid=(bsz, length // tm),
        in_specs=in_specs,
        out_specs=pl.BlockSpec((None, tm, w), lambda b, i: (b, i, 0)),
        out_shape=jax.ShapeDtypeStruct(x.shape, BF16),
        compiler_params=_cp("parallel", "parallel"),
        name="headnorm",
    )(*args)


def _group_mean_matrix(w, group):
    idx = jnp.arange(w) // group
    return jnp.where(idx[:, None] == idx[None, :], 1.0 / group, 0.0).astype(BF16)


def _rope_tables(length):
    t = jnp.arange(length)
    row = (t // GRID_W).astype(F32)
    col = (t % GRID_W).astype(F32)
    n = HEAD_DIM // 4
    inv = ROPE_BASE ** (-jnp.arange(n, dtype=F32) / n)
    ang = jnp.concatenate([row[:, None] * inv, col[:, None] * inv], axis=-1)
    cos = jnp.repeat(jnp.cos(ang), 2, axis=-1)
    sin = jnp.repeat(jnp.sin(ang), 2, axis=-1)
    sign = jnp.tile(jnp.array([-1.0, 1.0], F32), HEAD_DIM // 2)
    reps = LANES // HEAD_DIM
    return jnp.tile(cos, (1, reps)), jnp.tile(sin * sign, (1, reps))


def _half_masks(shape):
    lane = lax.broadcasted_iota(jnp.int32, shape, len(shape) - 1)
    return lane < HEAD_DIM, lane >= HEAD_DIM


def _softmax_pv(s_list, v_list):
    m = s_list[0].max(axis=-1, keepdims=True)
    for s in s_list[1:]:
        m = jnp.maximum(m, s.max(axis=-1, keepdims=True))
    acc, l = None, None
    for s, v in zip(s_list, v_list):
        p = jnp.exp(s - m)
        ls = p.sum(axis=-1, keepdims=True)
        o = _dot(p.astype(BF16), v)
        acc = o if acc is None else acc + o
        l = ls if l is None else l + ls
    return acc / l


def _na_kernel(q_ref, kp_ref, kc_ref, kn_ref, vp_ref, vc_ref, vn_ref, kctx_ref, vctx_ref, bias_ref,
               o_ref, kbuf, vbuf, vcbuf, *, rows_per_blk, n_rows):
    i = pl.program_id(1)
    blk = rows_per_blk * GRID_W
    n_pairs = q_ref.shape[-1] // LANES
    vw = 2 * LANES
    kbuf[0:blk, :] = kp_ref[...]
    kbuf[blk:2 * blk, :] = kc_ref[...]
    kbuf[2 * blk:3 * blk, :] = kn_ref[...]
    for hp in range(n_pairs):
        sl = slice(hp * LANES, (hp + 1) * LANES)
        vbuf[0:blk, hp * vw:hp * vw + LANES] = vp_ref[:, sl]
        vbuf[blk:2 * blk, hp * vw:hp * vw + LANES] = vc_ref[:, sl]
        vbuf[2 * blk:3 * blk, hp * vw:hp * vw + LANES] = vn_ref[:, sl]
        vbuf[:, hp * vw + LANES:(hp + 1) * vw] = jnp.ones((3 * blk, LANES), BF16)
        vcbuf[:, hp * vw:hp * vw + LANES] = vctx_ref[:, sl]
        vcbuf[:, hp * vw + LANES:(hp + 1) * vw] = jnp.ones((vcbuf.shape[0], LANES), BF16)
    win = NA_WIN_R * GRID_W

    def row_body(j, carry):
        r = i * rows_per_blk + j
        r0 = jnp.clip(r - NA_WIN_R // 2, 0, n_rows - NA_WIN_R)
        off = pl.multiple_of((r0 - (i - 1) * rows_per_blk) * GRID_W, GRID_W)
        cfg = r0 - r + NA_WIN_R - 1
        qrow = q_ref[pl.ds(pl.multiple_of(j * GRID_W, GRID_W), GRID_W), :]
        lo, hi = _half_masks((GRID_W, LANES))
        s_lat, s_ctx = [], []
        for hp in range(n_pairs):
            sl = slice(hp * LANES, (hp + 1) * LANES)
            qp = qrow[:, sl]
            zero = jnp.zeros_like(qp)
            qst = jnp.concatenate([jnp.where(lo, qp, zero), jnp.where(hi, qp, zero)], axis=0)
            s_lat.append(_dot_nt(qst, kbuf[pl.ds(off, win), sl]) + bias_ref[cfg, hp])
            s_ctx.append(_dot_nt(qst, kctx_ref[:, sl]))
        p_lat, p_ctx = [], []
        for hp in range(n_pairs):
            m = jnp.maximum(s_lat[hp].max(axis=-1, keepdims=True), s_ctx[hp].max(axis=-1, keepdims=True))
            p_lat.append(jnp.exp2(s_lat[hp] - m).astype(BF16))
            p_ctx.append(jnp.exp2(s_ctx[hp] - m).astype(BF16))
        outs = []
        for hp in range(n_pairs):
            ov = (_dot(p_lat[hp], vbuf[pl.ds(off, win), hp * vw:(hp + 1) * vw])
                  + _dot(p_ctx[hp], vcbuf[:, hp * vw:(hp + 1) * vw]))
            o = ov[:, 0:LANES] / ov[:, LANES:vw]
            outs.append(jnp.where(lo, o[0:GRID_W], o[GRID_W:2 * GRID_W]))
        o_ref[pl.ds(pl.multiple_of(j * GRID_W, GRID_W), GRID_W), :] = jnp.concatenate(outs, axis=1).astype(o_ref.dtype)
        return carry

    lax.fori_loop(0, rows_per_blk, row_body, 0)


def _na_bias_table(rpb):
    heads = rpb.shape[0]
    qc = jnp.arange(GRID_W)
    kc = jnp.arange(GRID_W)
    qc0 = jnp.clip(qc - NA_WIN_C // 2, 0, GRID_W - NA_WIN_C)
    col_in = (kc[None, :] >= qc0[:, None]) & (kc[None, :] < qc0[:, None] + NA_WIN_C)
    pad = GRID_W - NA_WIN_C
    rp = jnp.pad(rpb.astype(F32), ((0, 0), (0, 0), (pad, pad)))
    toep = jnp.stack([rp[:, :, GRID_W - 1 - q:2 * GRID_W - 1 - q] for q in range(GRID_W)], axis=2)
    toep = jnp.where(col_in[None, None], toep, NEG_BIG)
    tab = jnp.stack([toep[:, c:c + NA_WIN_R] for c in range(NA_WIN_R)], axis=0)
    tab = tab.transpose(0, 1, 3, 2, 4) * LOG2E
    return tab.reshape(NA_WIN_R, heads // 2, 2 * GRID_W, NA_WIN_R * GRID_W)


def _neighborhood_attention(q, k, v, kc, vc, bias):
    bsz, s, w = q.shape
    n_rows = s // GRID_W
    rpb_rows = NA_WIN_R
    blk = rpb_rows * GRID_W
    nb = s // blk
    lc = kc.shape[1]
    cur = pl.BlockSpec((None, blk, w), lambda b, i: (b, i, 0))
    prev = pl.BlockSpec((None, blk, w), lambda b, i: (b, jnp.maximum(i - 1, 0), 0))
    nxt = pl.BlockSpec((None, blk, w), lambda b, i: (b, jnp.minimum(i + 1, nb - 1), 0))
    ctx = pl.BlockSpec((None, lc, w), lambda b, i: (b, 0, 0))
    return pl.pallas_call(
        functools.partial(_na_kernel, rows_per_blk=rpb_rows, n_rows=n_rows),
        grid=(bsz, nb),
        in_specs=[cur, prev, cur, nxt, prev, cur, nxt, ctx, ctx, _const_spec(bias.shape)],
        out_specs=cur,
        out_shape=jax.ShapeDtypeStruct(q.shape, BF16),
        scratch_shapes=[pltpu.VMEM((3 * blk, w), BF16), pltpu.VMEM((3 * blk, 2 * w), BF16),
                        pltpu.VMEM((lc, 2 * w), BF16)],
        compiler_params=_cp("parallel", "parallel"),
        name="neighborhood_attention",
    )(q, k, k, k, v, v, v, kc, vc, bias)


def _ctx_attn_kernel(q_ref, k_ref, v_ref, o_ref):
    n_pairs = q_ref.shape[-1] // LANES
    outs = []
    for hp in range(n_pairs):
        sl = slice(hp * LANES, (hp + 1) * LANES)
        qp, kp, vp = q_ref[:, sl], k_ref[:, sl], v_ref[:, sl]
        o_pair = None
        for t, mask in enumerate(_half_masks(qp.shape)):
            qm = jnp.where(mask, qp, jnp.zeros_like(qp))
            vm = jnp.where(mask, vp, jnp.zeros_like(vp))
            o_t = _softmax_pv([_dot_nt(qm, kp)], [vm])
            o_pair = o_t if o_pair is None else o_pair + o_t
        outs.append(o_pair)
    o_ref[...] = jnp.concatenate(outs, axis=1).astype(o_ref.dtype)


def _ctx_attention(q, k, v):
    bsz, lc, w = q.shape
    spec = pl.BlockSpec((None, lc, w), lambda b: (b, 0, 0))
    return pl.pallas_call(
        _ctx_attn_kernel, grid=(bsz,), in_specs=[spec, spec, spec], out_specs=spec,
        out_shape=jax.ShapeDtypeStruct(q.shape, BF16),
        compiler_params=_cp("parallel"), name="ctx_attention",
    )(q, k, v)


def _conv_kernel(prev_ref, x_ref, next_ref, w_ref, b_ref, o_ref, *, act):
    i = pl.program_id(1)
    n = pl.num_programs(1)
    t = x_ref.shape[0]
    halo = prev_ref.shape[0]
    prev = jnp.where(i > 0, prev_ref[...], 0.0)
    nxt = jnp.where(i < n - 1, next_ref[...], 0.0)
    xx = jnp.concatenate([prev, x_ref[...], nxt], axis=0)
    taps = w_ref.shape[0]
    left = taps // 2
    tot = t + 2 * halo
    y = b_ref[...]
    for j in range(taps):
        sh = (left - j) % tot
        xs = xx if sh == 0 else pltpu.roll(xx, sh, axis=0)
        y = y + w_ref[j:j + 1, :] * xs[halo:halo + t]
    o_ref[...] = _silu(y) if act else y


def _dwconv(x, w, b, act):
    bsz, length, c = x.shape
    t = min(512, length)
    halo = 8
    per = t // halo
    nh = length // halo
    return pl.pallas_call(
        functools.partial(_conv_kernel, act=act),
        grid=(bsz, length // t),
        in_specs=[pl.BlockSpec((None, halo, c), lambda bb, i: (bb, jnp.maximum(i * per - 1, 0), 0)),
                  pl.BlockSpec((None, t, c), lambda bb, i: (bb, i, 0)),
                  pl.BlockSpec((None, halo, c), lambda bb, i: (bb, jnp.minimum((i + 1) * per, nh - 1), 0)),
                  _const_spec(w.shape), _const_spec((1, c))],
        out_specs=pl.BlockSpec((None, t, c), lambda bb, i: (bb, i, 0)),
        out_shape=jax.ShapeDtypeStruct(x.shape, F32),
        compiler_params=_cp("parallel", "parallel"),
        name="dwconv",
    )(x, x, x, w.astype(F32), b.astype(F32).reshape(1, c))


def _ssd_direction(xbc_ref, dtc_ref, dtr_ref, pc_ref, pr_ref, state, y_ref, *, d, reverse, add_skip):
    q = xbc_ref.shape[0]
    hp_w = LANES
    n_heads = SSD_HEADS
    gw = n_heads * HEAD_DIM
    ii = lax.broadcasted_iota(jnp.int32, (q, q), 0)
    jj = lax.broadcasted_iota(jnp.int32, (q, q), 1)
    keep = (ii <= jj) if reverse else (ii >= jj)
    tri = keep.astype(F32)
    dt_c = _softplus(dtc_ref[:, d * n_heads:(d + 1) * n_heads] + pc_ref[0:1, :])
    dt_r = _softplus(dtr_ref[d * n_heads:(d + 1) * n_heads, :] + pr_ref[:, 0:1])
    da_c = dt_c * pc_ref[1:2, :]
    da_r = dt_r * pr_ref[:, 1:2]
    acs_c = jnp.dot(tri, da_c, preferred_element_type=F32, precision=HIGHEST)
    acs_r = lax.dot_general(da_r, tri, (((1,), (1,)), ((), ())), preferred_element_type=F32,
                            precision=HIGHEST)
    edge = 0 if reverse else q - 1
    tot_r = acs_r[:, edge:edge + 1]
    e_in_c = jnp.exp(acs_c)
    e_end_r = jnp.exp(tot_r - acs_r)
    e_tot_r = jnp.exp(tot_r)
    lane_lo, lane_hi = _half_masks((q, hp_w))
    for g in range(2):
        bm = xbc_ref[:, gw + g * SSD_STATE:gw + (g + 1) * SSD_STATE]
        cm = xbc_ref[:, gw + 2 * SSD_STATE + g * SSD_STATE:gw + 2 * SSD_STATE + (g + 1) * SSD_STATE]
        bm_t = bm.T
        cm16 = cm.astype(BF16)
        cb = _dot(cm16, bm_t.astype(BF16))
        for pp in range(2):
            pair = g * 2 + pp
            h0 = 2 * pair
            sl = slice(pair * hp_w, (pair + 1) * hp_w)
            xs = xbc_ref[:, sl]
            dt_l = jnp.where(lane_lo, dt_c[:, h0:h0 + 1], dt_c[:, h0 + 1:h0 + 2])
            xdt = xs * dt_l
            y_pair = None
            st_new = None
            for t, lmask in enumerate((lane_lo, lane_hi)):
                h = h0 + t
                seg = acs_c[:, h:h + 1] - acs_r[h:h + 1, :]
                dec = jnp.where(keep, jnp.exp(jnp.where(keep, seg, 0.0)), 0.0)
                xm = jnp.where(lmask, xdt, 0.0).astype(BF16)
                yd = _dot((cb * dec).astype(BF16), xm)
                sn = _dot((bm_t * e_end_r[h:h + 1, :]).astype(BF16), xm)
                y_pair = yd if y_pair is None else y_pair + yd
                st_new = sn if st_new is None else st_new + sn
            st_old = state[:, sl]
            e_in = jnp.where(lane_lo, e_in_c[:, h0:h0 + 1], e_in_c[:, h0 + 1:h0 + 2])
            y_pair = y_pair + _dot(cm16, st_old.astype(BF16)) * e_in
            if add_skip:
                d_l = jnp.where(lane_lo[0:1], pc_ref[2:3, h0:h0 + 1], pc_ref[2:3, h0 + 1:h0 + 2])
                y_pair = y_pair + d_l * xs
            y_ref[:, sl] = y_pair
            e_tot = jnp.where(lane_lo[0:1], e_tot_r[h0:h0 + 1, :], e_tot_r[h0 + 1:h0 + 2, :])
            state[:, sl] = st_old * e_tot + st_new


def _ssd_kernel(xf_ref, xb_ref, dtcf_ref, dtcb_ref, dtrf_ref, dtrb_ref, pc_ref, pr_ref, h0f_ref, h0b_ref,
                yf_ref, yb_ref, hf_ref, hb_ref, sf, sb):
    c = pl.program_id(1)

    @pl.when(c == 0)
    def _():
        sf[...] = h0f_ref[...]
        sb[...] = h0b_ref[...]

    _ssd_direction(xf_ref, dtcf_ref, dtrf_ref, pc_ref.at[0], pr_ref.at[0], sf, yf_ref, d=0, reverse=False, add_skip=True)
    _ssd_direction(xb_ref, dtcb_ref, dtrb_ref, pc_ref.at[1], pr_ref.at[1], sb, yb_ref, d=1, reverse=True, add_skip=False)
    hf_ref[...] = sf[...]
    hb_ref[...] = sb[...]


def _ssd(xbc, dt, pc, pr, h0f, h0b):
    bsz, length, cw = xbc.shape
    q = SSD_CHUNK
    nc = length // q
    gw = SSD_HEADS * HEAD_DIM
    dtw = dt.shape[-1]
    dt_t = jnp.swapaxes(dt, 1, 2)
    fwd3 = lambda b, c: (b, c, 0)
    bwd3 = lambda b, c: (b, nc - 1 - c, 0)
    st_spec = pl.BlockSpec((None, SSD_STATE, gw), lambda b, c: (b, 0, 0))
    return pl.pallas_call(
        _ssd_kernel,
        grid=(bsz, nc),
        in_specs=[pl.BlockSpec((None, q, cw), fwd3), pl.BlockSpec((None, q, cw), bwd3),
                  pl.BlockSpec((None, q, dtw), fwd3), pl.BlockSpec((None, q, dtw), bwd3),
                  pl.BlockSpec((None, dtw, q), lambda b, c: (b, 0, c)),
                  pl.BlockSpec((None, dtw, q), lambda b, c: (b, 0, nc - 1 - c)),
                  _const_spec(pc.shape), _const_spec(pr.shape), st_spec, st_spec],
        out_specs=[pl.BlockSpec((None, q, gw), fwd3), pl.BlockSpec((None, q, gw), bwd3), st_spec, st_spec],
        out_shape=[jax.ShapeDtypeStruct((bsz, length, gw), F32)] * 2
                  + [jax.ShapeDtypeStruct((bsz, SSD_STATE, gw), F32)] * 2,
        scratch_shapes=[pltpu.VMEM((SSD_STATE, gw), F32)] * 2,
        compiler_params=_cp("parallel", "arbitrary"),
        name="ssd_scan",
    )(xbc, xbc, dt, dt, dt_t, dt_t, pc, pr, h0f, h0b)


def _gelu_tanh(x):
    return 0.5 * x * (1.0 + jnp.tanh(math.sqrt(2.0 / math.pi) * (x + 0.044715 * (x * x * x))))


def _out_ab_kernel(x_ref, mod_ref, ona_ref, yf_ref, yb_ref, z_ref, ng_ref, w_ref, o_ref):
    gw = ona_ref.shape[-1]
    y = (yf_ref[...] + yb_ref[...]) * _silu(z_ref[...])
    gated = y * lax.rsqrt(jnp.mean(y * y, axis=-1, keepdims=True) + EPS) * ng_ref[...]
    out = _dot(ona_ref[...], w_ref[0:gw, :]) + _dot(gated.astype(BF16), w_ref[gw:2 * gw, :])
    o_ref[...] = x_ref[...] + mod_ref[5:6, :] * out


def _out_cd_kernel(x_ref, mod_ref, gate_ref, hf_ref, hb_ref, od_ref, w_ref, o_ref):
    gw = od_ref.shape[-1]
    lru = _gelu_tanh(gate_ref[...]) * (hf_ref[...] + hb_ref[...])
    out = _dot(lru.astype(BF16), w_ref[0:gw, :]) + _dot(od_ref[...], w_ref[gw:2 * gw, :])
    o_ref[...] = x_ref[...] + mod_ref[5:6, :] * out


def _mixer_out(kern, name, x, mod, parts, consts):
    bsz, length, d = x.shape
    tm = min(512, length)
    tok = lambda a: pl.BlockSpec((None, tm, a.shape[-1]), lambda b, i: (b, i, 0))
    return pl.pallas_call(
        kern,
        grid=(bsz, length // tm),
        in_specs=[tok(x), _mod_spec(mod)] + [tok(a) for a in parts] + [_const_spec(a.shape) for a in consts],
        out_specs=tok(x),
        out_shape=jax.ShapeDtypeStruct(x.shape, F32),
        compiler_params=_cp("parallel", "parallel"),
        name=name,
    )(x, mod, *parts, *consts)


def _lru_direction(x_ref, wa_ref, wx_ref, p_ref, carry, h_ref, *, reverse):
    t = x_ref.shape[0]
    x = x_ref[...]
    x16 = x.astype(BF16)
    r = _sigmoid(_dot(x16, wa_ref[...]) + p_ref[0:1, :])
    ig = _sigmoid(_dot(x16, wx_ref[...]) + p_ref[1:2, :])
    log_a = -LRU_C * r * _softplus(-p_ref[2:3, :])
    a = jnp.exp(log_a)
    b = jnp.sqrt(1.0 - a * a) * (ig * x)
    row = lax.broadcasted_iota(jnp.int32, a.shape, 0)
    s = 1
    while s < t:
        if reverse:
            fill = row >= t - s
            a_sh = jnp.where(fill, 1.0, pltpu.roll(a, t - s, axis=0))
            b_sh = jnp.where(fill, 0.0, pltpu.roll(b, t - s, axis=0))
        else:
            fill = row < s
            a_sh = jnp.where(fill, 1.0, pltpu.roll(a, s, axis=0))
            b_sh = jnp.where(fill, 0.0, pltpu.roll(b, s, axis=0))
        b = a * b_sh + b
        a = a * a_sh
        s *= 2
    h = a * carry[...] + b
    h_ref[...] = h
    carry[...] = h[0:1, :] if reverse else h[t - 1:t, :]


def _lru_kernel(xf_ref, xb_ref, wa_ref, wx_ref, p_ref, h0f_ref, h0b_ref, hf_ref, hb_ref, lf_ref, lb_ref, cf, cb):
    c = pl.program_id(1)

    @pl.when(c == 0)
    def _():
        cf[...] = h0f_ref[...]
        cb[...] = h0b_ref[...]

    _lru_direction(xf_ref, wa_ref.at[0], wx_ref.at[0], p_ref.at[0], cf, hf_ref, reverse=False)
    _lru_direction(xb_ref, wa_ref.at[1], wx_ref.at[1], p_ref.at[1], cb, hb_ref, reverse=True)
    lf_ref[...] = cf[...]
    lb_ref[...] = cb[...]


def _lru(x, wa, wx, p, h0f, h0b):
    bsz, length, w = x.shape
    t = min(256, length)
    nt = length // t
    fwd = lambda b, c: (b, c, 0)
    bwd = lambda b, c: (b, nt - 1 - c, 0)
    st = pl.BlockSpec((None, 1, w), lambda b, c: (b, 0, 0))
    return pl.pallas_call(
        _lru_kernel,
        grid=(bsz, nt),
        in_specs=[pl.BlockSpec((None, t, w), fwd), pl.BlockSpec((None, t, w), bwd),
                  _const_spec(wa.shape), _const_spec(wx.shape), _const_spec(p.shape), st, st],
        out_specs=[pl.BlockSpec((None, t, w), fwd), pl.BlockSpec((None, t, w), bwd), st, st],
        out_shape=[jax.ShapeDtypeStruct(x.shape, F32)] * 2 + [jax.ShapeDtypeStruct((bsz, 1, w), F32)] * 2,
        scratch_shapes=[pltpu.VMEM((1, w), F32)] * 2,
        compiler_params=_cp("parallel", "arbitrary"),
        name="rglru_scan",
    )(x, x, wa, wx, p, h0f, h0b)


def _block_diag(wb):
    nb, bs, _ = wb.shape
    eye = jnp.eye(nb, dtype=wb.dtype)
    return (wb[:, :, None, :] * eye[:, None, :, None]).reshape(nb * bs, nb * bs)


def _diff_attn_kernel(q_ref, kt_ref, v_ref, lam_ref, sg_ref, o_ref, qs, m_s, acc_s, *, lam_init, n_heads, tk):
    tq = q_ref.shape[0]
    vw = 2 * LANES
    for h in range(n_heads):
        qp = q_ref[:, h * LANES:(h + 1) * LANES]
        lo, hi = _half_masks(qp.shape)
        qs[h, 0:tq, :] = jnp.where(lo, qp, jnp.zeros_like(qp))
        qs[h, tq:2 * tq, :] = jnp.where(hi, qp, jnp.zeros_like(qp))
    m_s[...] = jnp.full(m_s.shape, NEG_BIG, F32)
    acc_s[...] = jnp.zeros(acc_s.shape, F32)

    def chunk(c, carry):
        k0 = pl.multiple_of(c * tk, tk)
        for h in range(n_heads):
            s = _dot(qs[h], kt_ref[h * LANES:(h + 1) * LANES, pl.ds(k0, tk)])
            m_old = m_s[h]
            m_new = jnp.maximum(m_old, s.max(axis=-1, keepdims=True))
            alpha = jnp.exp2(m_old - m_new)
            p = jnp.concatenate([jnp.exp2(s[:, j * LANES:(j + 1) * LANES] - m_new).astype(BF16)
                                 for j in range(tk // LANES)], axis=1)
            pv = _dot(p, v_ref[pl.ds(k0, tk), h * vw:(h + 1) * vw])
            acc_s[h] = jnp.concatenate([alpha, alpha], axis=1) * acc_s[h] + pv
            m_s[h] = m_new
        return carry

    lax.fori_loop(0, kt_ref.shape[1] // tk, chunk, 0)

    dl = lam_ref[...]
    lam = (jnp.exp(jnp.sum(dl[0:1] * dl[1:2], axis=-1, keepdims=True))
           - jnp.exp(jnp.sum(dl[2:3] * dl[3:4], axis=-1, keepdims=True)) + lam_init)
    for h in range(n_heads):
        a = acc_s[h]
        o = a[0:tq, 0:LANES] / a[0:tq, LANES:vw] - lam * (a[tq:2 * tq, 0:LANES] / a[tq:2 * tq, LANES:vw])
        o = o * lax.rsqrt(jnp.mean(o * o, axis=-1, keepdims=True) + EPS) * sg_ref[...] * (1.0 - lam_init)
        o_ref[:, h * LANES:(h + 1) * LANES] = o.astype(o_ref.dtype)


def _diff_attention(q, kt_all, v_ext, diff_lam, subln_g, lam_init):
    bsz, s, w = q.shape
    lk = kt_all.shape[2]
    n_heads = w // LANES
    tq = min(DIFF_TQ, s)
    tk = next(c for c in (768, 512, 256, 128) if lk % c == 0)
    return pl.pallas_call(
        functools.partial(_diff_attn_kernel, lam_init=lam_init, n_heads=n_heads, tk=tk),
        grid=(bsz, s // tq),
        in_specs=[pl.BlockSpec((None, tq, w), lambda b, i: (b, i, 0)),
                  pl.BlockSpec((None, w, lk), lambda b, i: (b, 0, 0), pipeline_mode=pl.Buffered(1)),
                  pl.BlockSpec((None, lk, v_ext.shape[2]), lambda b, i: (b, 0, 0), pipeline_mode=pl.Buffered(1)),
                  _const_spec(diff_lam.shape), _const_spec((1, LANES))],
        out_specs=pl.BlockSpec((None, tq, w), lambda b, i: (b, i, 0)),
        out_shape=jax.ShapeDtypeStruct(q.shape, BF16),
        scratch_shapes=[pltpu.VMEM((n_heads, 2 * tq, LANES), BF16), pltpu.VMEM((n_heads, 2 * tq, LANES), F32),
                        pltpu.VMEM((n_heads, 2 * tq, 2 * LANES), F32)],
        compiler_params=pltpu.CompilerParams(dimension_semantics=("parallel", "parallel"),
                                             vmem_limit_bytes=DIFF_VMEM_LIMIT),
        name="diff_attention",
    )(q, kt_all, v_ext, diff_lam.astype(F32), subln_g.astype(F32).reshape(1, LANES))


def _pad_cols(w, total):
    return jnp.pad(w, ((0, 0), (0, total - w.shape[1])))


def _layer_ab(x, xc, m, mc, g_mix, w_in, w_out, q_g, k_g, rpb, conv_w, conv_b, dt_bias, a_log, d_skip, norm_g):
    gw = SSD_HEADS * HEAD_DIM
    widths = (gw, gw, gw, gw, 2 * gw, LANES)
    dtypes = (F32, F32, BF16, F32, F32, F32)
    w16 = _pad_cols(w_in, sum(widths)).astype(BF16)
    q, k, v, z, xbc, dt = _norm_proj(x, m, g_mix, w16, 1, widths, dtypes)
    q_c, k_c, v_c, z_c, xbc_c, dt_c = _norm_proj(xc, mc, g_mix, w16, 1, widths, dtypes)

    pmat = _group_mean_matrix(gw, HEAD_DIM)
    scale = HEAD_DIM ** -0.5
    q = _headnorm(q, q_g, pmat, scale * LOG2E)
    k = _headnorm(k, k_g, pmat, 1.0)
    q_c = _headnorm(q_c, q_g, pmat, scale)
    k_c = _headnorm(k_c, k_g, pmat, 1.0)
    o_na = _neighborhood_attention(q, k, v, k_c, v_c, _na_bias_table(rpb))
    o_c = _ctx_attention(q_c, k_c, v_c)

    a_neg = -jnp.exp(a_log.astype(F32))
    pc = jnp.stack([dt_bias.astype(F32), a_neg, jnp.broadcast_to(d_skip.astype(F32), a_neg.shape)], axis=1)
    pr = jnp.swapaxes(pc, 1, 2)
    cw = conv_w.astype(F32)
    xbc_c = _dwconv(xbc_c, cw, conv_b, act=True)
    xbc = _dwconv(xbc, cw, conv_b, act=True)
    zeros = jnp.zeros((x.shape[0], SSD_STATE, gw), F32)
    yf_c, yb_c, hf_c, hb_c = _ssd(xbc_c, dt_c, pc, pr, zeros, zeros)
    yf, yb, _, _ = _ssd(xbc, dt, pc, pr, hf_c, hb_c)

    ng = norm_g.astype(F32).reshape(1, gw)
    wo16 = w_out.astype(BF16)
    x = _mixer_out(_out_ab_kernel, "mixer_out_ab", x, m, [o_na, yf, yb, z], [ng, wo16])
    xc = _mixer_out(_out_ab_kernel, "mixer_out_ab", xc, mc, [o_c, yf_c, yb_c, z_c], [ng, wo16])
    return x, xc


def _layer_cd(x, xc, m, mc, g_mix, w_in, w_out, conv_w, conv_b, wa, ba, wx, bx, lam_p, q_g, k_g, diff_lam,
              subln_g, lam_init):
    gw = w_out.shape[0] // 2
    widths = (gw,) * 5
    w16 = w_in.astype(BF16)
    gate, xr, q, k, v = _norm_proj(x, m, g_mix, w16, 1, widths, (F32, F32, F32, F32, BF16))
    _, xr_c, _, k_c, v_c = _norm_proj(xc, mc, g_mix, w16, 1, widths, (F32, F32, F32, F32, BF16))

    cw = conv_w.astype(F32)
    xr_c = _dwconv(xr_c, cw, conv_b, act=False)
    xr = _dwconv(xr, cw, conv_b, act=False)
    wa_d = jnp.stack([_block_diag(wa[0]), _block_diag(wa[1])]).astype(BF16)
    wx_d = jnp.stack([_block_diag(wx[0]), _block_diag(wx[1])]).astype(BF16)
    p = jnp.stack([ba.astype(F32), bx.astype(F32), lam_p.astype(F32)], axis=1)
    zeros = jnp.zeros((x.shape[0], 1, gw), F32)
    _, _, lf_c, lb_c = _lru(xr_c, wa_d, wx_d, p, zeros, zeros)
    hf, hb, _, _ = _lru(xr, wa_d, wx_d, p, lf_c, lb_c)

    pmat = _group_mean_matrix(gw, HEAD_DIM)
    tabs = _rope_tables(x.shape[1])
    q = _headnorm(q, q_g, pmat, HEAD_DIM ** -0.5 * LOG2E, tabs)
    k = _headnorm(k, k_g, pmat, 1.0, tabs)
    k_c = _headnorm(k_c, k_g, pmat, 1.0)
    kt_all = jnp.swapaxes(jnp.concatenate([k_c, k], axis=1), 1, 2)
    v_all = jnp.concatenate([v_c, v], axis=1)
    lk = v_all.shape[1]
    v_heads = v_all.reshape(x.shape[0], lk, gw // LANES, LANES)
    v_ext = jnp.concatenate([v_heads, jnp.ones_like(v_heads)], axis=-1).reshape(x.shape[0], lk, 2 * gw)
    o = _diff_attention(q, kt_all, v_ext, diff_lam, subln_g, lam_init)
    return _mixer_out(_out_cd_kernel, "mixer_out_cd", x, m, [gate, hf, hb, o], [w_out.astype(BF16)])


def kernel(x, c, ctx, c_ctx, w_mod, b_mod, norm_g, ffn_w1, ffn_w3, ffn_w2, ab_w_in, ab_w_out, na_q_g, na_k_g, na_rpb, ssd_conv_w, ssd_conv_b, ssd_dt_bias, ssd_a_log, ssd_d, ssd_norm_g, cd_w_in, cd_w_out, lru_conv_w, lru_conv_b, lru_wa, lru_ba, lru_wx, lru_bx, lru_lambda, diff_q_g, diff_k_g, diff_lambda, diff_subln_g):
    bsz, _, d = x.shape
    depth = w_mod.shape[0]
    cc = jnp.concatenate([c.astype(F32), c_ctx.astype(F32)[None], jnp.zeros((8 - bsz - 1, d), F32)], axis=0)
    mods = _modulation(cc, w_mod.astype(F32), b_mod.astype(F32))
    xc = ctx
    for i in range(depth):
        last = i == depth - 1
        j = i // 2
        m = mods[i, :bsz].reshape(bsz, N_MOD, d)
        mc = mods[i, bsz:bsz + 1].reshape(1, N_MOD, d)
        g = norm_g[i].astype(F32)
        w1 = ffn_w1[i].astype(BF16)
        w3 = ffn_w3[i].astype(BF16)
        w2 = ffn_w2[i].astype(BF16)
        x = _ffn(x, m, g[0], w1[0], w3[0], w2[0], 0)
        xc = _ffn(xc, mc, g[0], w1[0], w3[0], w2[0], 0)
        if i % 2 == 0:
            x, xc = _layer_ab(x, xc, m, mc, g[1], ab_w_in[j], ab_w_out[j], na_q_g[j], na_k_g[j], na_rpb[j],
                              ssd_conv_w[j], ssd_conv_b[j], ssd_dt_bias[j], ssd_a_log[j], ssd_d[j], ssd_norm_g[j])
        else:
            lam_init = 0.8 - 0.6 * math.exp(-0.3 * i)
            x = _layer_cd(x, xc, m, mc, g[1], cd_w_in[j], cd_w_out[j], lru_conv_w[j], lru_conv_b[j], lru_wa[j],
                          lru_ba[j], lru_wx[j], lru_bx[j], lru_lambda[j], diff_q_g[j], diff_k_g[j],
                          diff_lambda[j], diff_subln_g[j], lam_init)
            xc = None
        x = _ffn(x, m, g[2], w1[1], w3[1], w2[1], 2)
        if not last and xc is not None:
            xc = _ffn(xc, mc, g[2], w1[1], w3[1], w2[1], 2)
    return x
```

```python
import functools
import math

import jax
import jax.numpy as jnp
from jax import lax
from jax.experimental import pallas as pl
from jax.experimental.pallas import tpu as pltpu

F32 = jnp.float32
BF16 = jnp.bfloat16
HIGHEST = lax.Precision.HIGHEST

GRID_W = 64
EPS = 1e-6
HEAD_DIM = 64
N_MOD = 9
NA_WIN_R = 8
NA_WIN_C = 16
SSD_STATE = 128
SSD_CHUNK = 128
SSD_HEADS = 8
LRU_C = 8.0
ROPE_BASE = 10000.0
NEG_BIG = -1e30

LANES = 128
SUBLANES = 8
VMEM_LIMIT = 48 * 1024 * 1024
DIFF_VMEM_LIMIT = 56 * 1024 * 1024
DIFF_TQ = 512
LOG2E = math.log2(math.e)


def _cp(*sem):
    return pltpu.CompilerParams(dimension_semantics=sem, vmem_limit_bytes=VMEM_LIMIT)


def _const_spec(shape):
    nd = len(shape)
    return pl.BlockSpec(shape, lambda *_: (0,) * nd, pipeline_mode=pl.Buffered(1))


def _dot(a, b):
    return jnp.dot(a, b, preferred_element_type=F32)


def _dot_nt(a, b):
    return lax.dot_general(a, b, (((1,), (1,)), ((), ())), preferred_element_type=F32)


def _sigmoid(x):
    return 0.5 * jnp.tanh(0.5 * x) + 0.5


def _silu(x):
    return x * _sigmoid(x)


def _softplus(x):
    return jnp.maximum(x, 0.0) + jnp.log(1.0 + jnp.exp(-jnp.abs(x)))


def _rms_mod(x, g, shift, scale):
    ms = jnp.mean(x * x, axis=-1, keepdims=True)
    return (x * lax.rsqrt(ms + EPS) * g) * (1.0 + scale) + shift


def _mod_kernel(c_ref, w_ref, b_ref, o_ref):
    s = _silu(c_ref[...])
    o_ref[...] = jnp.dot(s, w_ref[...], preferred_element_type=F32, precision=HIGHEST) + b_ref[...]


def _modulation(cc, w_mod, b_mod):
    depth, d, n = w_mod.shape
    tn = 1024
    return pl.pallas_call(
        _mod_kernel,
        grid=(depth, n // tn),
        in_specs=[pl.BlockSpec((8, d), lambda l, j: (0, 0)),
                  pl.BlockSpec((None, d, tn), lambda l, j: (l, 0, j)),
                  pl.BlockSpec((None, 1, tn), lambda l, j: (l, 0, j))],
        out_specs=pl.BlockSpec((None, 8, tn), lambda l, j: (l, 0, j)),
        out_shape=jax.ShapeDtypeStruct((depth, 8, n), F32),
        compiler_params=_cp("parallel", "parallel"),
        name="modulation",
    )(cc, w_mod, b_mod.reshape(depth, 1, n))


def _mod_spec(mod):
    if mod.shape[0] == 1:
        return pl.BlockSpec((None, N_MOD, mod.shape[2]), lambda b, i: (0, 0, 0))
    return pl.BlockSpec((None, N_MOD, mod.shape[2]), lambda b, i: (b, 0, 0))


def _ffn_body(x, mod_ref, g_ref, w1_ref, w3_ref, w2_ref, k, chunk):
    h = _rms_mod(x, g_ref[...], mod_ref[3 * k:3 * k + 1, :], mod_ref[3 * k + 1:3 * k + 2, :]).astype(BF16)
    ff = w1_ref.shape[1]
    acc = jnp.zeros(x.shape, F32)
    for c0 in range(0, ff, chunk):
        a = _dot(h, w1_ref[:, c0:c0 + chunk])
        b = _dot(h, w3_ref[:, c0:c0 + chunk])
        acc = acc + _dot((_silu(a) * b).astype(BF16), w2_ref[c0:c0 + chunk, :])
    return x + (0.5 * mod_ref[3 * k + 2:3 * k + 3, :]) * acc


def _ffn_kernel(x_ref, mod_ref, g_ref, w1_ref, w3_ref, w2_ref, o_ref, *, k, chunk):
    o_ref[...] = _ffn_body(x_ref[...], mod_ref, g_ref, w1_ref, w3_ref, w2_ref, k, chunk)


def _ffn(x, mod, g, w1, w3, w2, k):
    bsz, length, d = x.shape
    ff = w1.shape[1]
    tm = min(512, length)
    return pl.pallas_call(
        functools.partial(_ffn_kernel, k=k, chunk=ff // 2),
        grid=(bsz, length // tm),
        in_specs=[pl.BlockSpec((None, tm, d), lambda b, i: (b, i, 0)),
                  _mod_spec(mod),
                  _const_spec((1, d)),
                  _const_spec((d, ff)), _const_spec((d, ff)), _const_spec((ff, d))],
        out_specs=pl.BlockSpec((None, tm, d), lambda b, i: (b, i, 0)),
        out_shape=jax.ShapeDtypeStruct(x.shape, F32),
        compiler_params=_cp("parallel", "parallel"),
        name="half_ffn",
    )(x, mod, g.reshape(1, d), w1, w3, w2)


class _Seg:
    def __init__(self, width, kind='plain', dtype=F32, gain=None, scale=1.0, rope=False):
        self.width, self.kind, self.dtype, self.gain, self.scale, self.rope = width, kind, dtype, gain, scale, rope


def _head_norm(y, gain, pmat, scale, rope_tabs):
    sq = y * y
    hi = sq.astype(BF16)
    lo = (sq - hi.astype(F32)).astype(BF16)
    ms = _dot(hi, pmat) + _dot(lo, pmat)
    y = y * lax.rsqrt(ms + EPS) * gain
    if rope_tabs is not None:
        reps = y.shape[-1] // rope_tabs[0].shape[-1]
        cs = jnp.concatenate([rope_tabs[0]] * reps, axis=1)
        sn = jnp.concatenate([rope_tabs[1]] * reps, axis=1)
        y = y * cs + _swap_pairs(y) * sn
    return y * scale


def _norm_proj_kernel(x_ref, mod_ref, g_ref, w_ref, *rest, k, segs, n_gain, has_rope):
    pos = 0
    pmat = None
    if n_gain:
        pmat = rest[0][...]
        pos = 1
    gains = rest[pos:pos + n_gain]
    pos += n_gain
    rope_tabs = None
    if has_rope:
        rope_tabs = (rest[pos][...], rest[pos + 1][...])
        pos += 2
    o_refs = rest[pos:]
    h = _rms_mod(x_ref[...], g_ref[...], mod_ref[3 * k:3 * k + 1, :], mod_ref[3 * k + 1:3 * k + 2, :]).astype(BF16)
    off, gi = 0, 0
    for seg, o_ref in zip(segs, o_refs):
        y = _dot(h, w_ref[:, off:off + seg.width])
        off += seg.width
        if seg.kind in ('norm', 'norm_t'):
            y = _head_norm(y, gains[gi][...], pmat, seg.scale, rope_tabs if seg.rope else None)
            gi += 1
            o_ref[...] = (y.T if seg.kind == 'norm_t' else y).astype(o_ref.dtype)
        elif seg.kind == 'vext':
            y16 = y.astype(BF16)
            ones = jnp.ones((y.shape[0], LANES), BF16)
            for hp in range(seg.width // LANES):
                o_ref[:, 2 * hp * LANES:(2 * hp + 1) * LANES] = y16[:, hp * LANES:(hp + 1) * LANES]
                o_ref[:, (2 * hp + 1) * LANES:(2 * hp + 2) * LANES] = ones
        else:
            o_ref[...] = y.astype(o_ref.dtype)


def _norm_proj(x, mod, g, w, k, segs, pmat=None, rope_tabs=None):
    bsz, length, d = x.shape
    tm = min(512, length)
    tok = lambda wd: pl.BlockSpec((None, tm, wd), lambda b, i: (b, i, 0))
    in_specs = [tok(d), _mod_spec(mod), _const_spec((1, d)), _const_spec(w.shape)]
    args = [x, mod, g.reshape(1, d), w]
    gains = [s for s in segs if s.kind in ('norm', 'norm_t')]
    if gains:
        in_specs.append(_const_spec(pmat.shape))
        args.append(pmat)
        for s in gains:
            in_specs.append(_const_spec((1, s.width)))
            args.append(jnp.tile(s.gain.astype(F32), s.width // s.gain.shape[0]).reshape(1, s.width))
    has_rope = any(s.rope for s in segs)
    if has_rope:
        in_specs += [pl.BlockSpec((tm, rope_tabs[0].shape[-1]), lambda b, i: (i, 0))] * 2
        args += list(rope_tabs)
    out_specs, out_shape = [], []
    for s in segs:
        if s.kind == 'norm_t':
            out_specs.append(pl.BlockSpec((None, s.width, tm), lambda b, i: (b, 0, i)))
            out_shape.append(jax.ShapeDtypeStruct((bsz, s.width, length), BF16))
        elif s.kind == 'vext':
            out_specs.append(tok(2 * s.width))
            out_shape.append(jax.ShapeDtypeStruct((bsz, length, 2 * s.width), BF16))
        else:
            out_specs.append(tok(s.width))
            out_shape.append(jax.ShapeDtypeStruct((bsz, length, s.width), BF16 if s.kind == 'norm' else s.dtype))
    return pl.pallas_call(
        functools.partial(_norm_proj_kernel, k=k, segs=tuple(segs), n_gain=len(gains), has_rope=has_rope),
        grid=(bsz, length // tm),
        in_specs=in_specs,
        out_specs=out_specs,
        out_shape=out_shape,
        compiler_params=_cp("parallel", "parallel"),
        name="norm_proj",
    )(*args)


def _swap_pairs(x):
    n = x.shape[-1]
    lane = lax.broadcasted_iota(jnp.int32, x.shape, 1)
    return jnp.where(lane % 2 == 0, pltpu.roll(x, n - 1, axis=1), pltpu.roll(x, 1, axis=1))


def _group_mean_matrix(w, group):
    idx = jnp.arange(w) // group
    return jnp.where(idx[:, None] == idx[None, :], 1.0 / group, 0.0).astype(BF16)


def _rope_tables(length):
    t = jnp.arange(length)
    row = (t // GRID_W).astype(F32)
    col = (t % GRID_W).astype(F32)
    n = HEAD_DIM // 4
    inv = ROPE_BASE ** (-jnp.arange(n, dtype=F32) / n)
    ang = jnp.concatenate([row[:, None] * inv, col[:, None] * inv], axis=-1)
    cos = jnp.repeat(jnp.cos(ang), 2, axis=-1)
    sin = jnp.repeat(jnp.sin(ang), 2, axis=-1)
    sign = jnp.tile(jnp.array([-1.0, 1.0], F32), HEAD_DIM // 2)
    reps = LANES // HEAD_DIM
    return jnp.tile(cos, (1, reps)), jnp.tile(sin * sign, (1, reps))


def _half_masks(shape):
    lane = lax.broadcasted_iota(jnp.int32, shape, len(shape) - 1)
    return lane < HEAD_DIM, lane >= HEAD_DIM


def _softmax_pv(s_list, v_list):
    m = s_list[0].max(axis=-1, keepdims=True)
    for s in s_list[1:]:
        m = jnp.maximum(m, s.max(axis=-1, keepdims=True))
    acc, l = None, None
    for s, v in zip(s_list, v_list):
        p = jnp.exp2(s - m)
        ls = p.sum(axis=-1, keepdims=True)
        o = _dot(p.astype(BF16), v)
        acc = o if acc is None else acc + o
        l = ls if l is None else l + ls
    return acc / l


def _na_kernel(q_ref, kp_ref, kc_ref, kn_ref, vp_ref, vc_ref, vn_ref, kctx_ref, vctx_ref, bias_ref,
               o_ref, kbuf, vbuf, vcbuf, *, rows_per_blk, n_rows):
    i = pl.program_id(1)
    blk = rows_per_blk * GRID_W
    n_pairs = q_ref.shape[-1] // LANES
    vw = 2 * LANES
    kbuf[0:blk, :] = kp_ref[...]
    kbuf[blk:2 * blk, :] = kc_ref[...]
    kbuf[2 * blk:3 * blk, :] = kn_ref[...]
    for hp in range(n_pairs):
        sl = slice(hp * LANES, (hp + 1) * LANES)
        vbuf[0:blk, hp * vw:hp * vw + LANES] = vp_ref[:, sl]
        vbuf[blk:2 * blk, hp * vw:hp * vw + LANES] = vc_ref[:, sl]
        vbuf[2 * blk:3 * blk, hp * vw:hp * vw + LANES] = vn_ref[:, sl]
        vbuf[:, hp * vw + LANES:(hp + 1) * vw] = jnp.ones((3 * blk, LANES), BF16)
        vcbuf[:, hp * vw:hp * vw + LANES] = vctx_ref[:, sl]
        vcbuf[:, hp * vw + LANES:(hp + 1) * vw] = jnp.ones((vcbuf.shape[0], LANES), BF16)
    win = NA_WIN_R * GRID_W

    def row_body(j, carry):
        r = i * rows_per_blk + j
        r0 = jnp.clip(r - NA_WIN_R // 2, 0, n_rows - NA_WIN_R)
        off = pl.multiple_of((r0 - (i - 1) * rows_per_blk) * GRID_W, GRID_W)
        cfg = r0 - r + NA_WIN_R - 1
        qrow = q_ref[pl.ds(pl.multiple_of(j * GRID_W, GRID_W), GRID_W), :]
        lo, hi = _half_masks((GRID_W, LANES))
        s_lat, s_ctx = [], []
        for hp in range(n_pairs):
            sl = slice(hp * LANES, (hp + 1) * LANES)
            qp = qrow[:, sl]
            zero = jnp.zeros_like(qp)
            qst = jnp.concatenate([jnp.where(lo, qp, zero), jnp.where(hi, qp, zero)], axis=0)
            s_lat.append(_dot_nt(qst, kbuf[pl.ds(off, win), sl]) + bias_ref[cfg, hp])
            s_ctx.append(_dot_nt(qst, kctx_ref[:, sl]))
        p_lat, p_ctx = [], []
        for hp in range(n_pairs):
            m = jnp.maximum(s_lat[hp].max(axis=-1, keepdims=True), s_ctx[hp].max(axis=-1, keepdims=True))
            p_lat.append(jnp.exp2(s_lat[hp] - m).astype(BF16))
            p_ctx.append(jnp.exp2(s_ctx[hp] - m).astype(BF16))
        outs = []
        for hp in range(n_pairs):
            ov = (_dot(p_lat[hp], vbuf[pl.ds(off, win), hp * vw:(hp + 1) * vw])
                  + _dot(p_ctx[hp], vcbuf[:, hp * vw:(hp + 1) * vw]))
            o = ov[:, 0:LANES] / ov[:, LANES:vw]
            outs.append(jnp.where(lo, o[0:GRID_W], o[GRID_W:2 * GRID_W]))
        o_ref[pl.ds(pl.multiple_of(j * GRID_W, GRID_W), GRID_W), :] = jnp.concatenate(outs, axis=1).astype(o_ref.dtype)
        return carry

    lax.fori_loop(0, rows_per_blk, row_body, 0)


def _na_bias_table(rpb):
    heads = rpb.shape[0]
    qc = jnp.arange(GRID_W)
    kc = jnp.arange(GRID_W)
    qc0 = jnp.clip(qc - NA_WIN_C // 2, 0, GRID_W - NA_WIN_C)
    col_in = (kc[None, :] >= qc0[:, None]) & (kc[None, :] < qc0[:, None] + NA_WIN_C)
    pad = GRID_W - NA_WIN_C
    rp = jnp.pad(rpb.astype(F32), ((0, 0), (0, 0), (pad, pad)))
    toep = jnp.stack([rp[:, :, GRID_W - 1 - q:2 * GRID_W - 1 - q] for q in range(GRID_W)], axis=2)
    toep = jnp.where(col_in[None, None], toep, NEG_BIG)
    tab = jnp.stack([toep[:, c:c + NA_WIN_R] for c in range(NA_WIN_R)], axis=0)
    tab = tab.transpose(0, 1, 3, 2, 4) * LOG2E
    return tab.reshape(NA_WIN_R, heads // 2, 2 * GRID_W, NA_WIN_R * GRID_W)


def _neighborhood_attention(q, k, v, kc, vc, bias):
    bsz, s, w = q.shape
    n_rows = s // GRID_W
    rpb_rows = NA_WIN_R
    blk = rpb_rows * GRID_W
    nb = s // blk
    lc = kc.shape[1]
    cur = pl.BlockSpec((None, blk, w), lambda b, i: (b, i, 0))
    prev = pl.BlockSpec((None, blk, w), lambda b, i: (b, jnp.maximum(i - 1, 0), 0))
    nxt = pl.BlockSpec((None, blk, w), lambda b, i: (b, jnp.minimum(i + 1, nb - 1), 0))
    ctx = pl.BlockSpec((None, lc, w), lambda b, i: (b, 0, 0))
    return pl.pallas_call(
        functools.partial(_na_kernel, rows_per_blk=rpb_rows, n_rows=n_rows),
        grid=(bsz, nb),
        in_specs=[cur, prev, cur, nxt, prev, cur, nxt, ctx, ctx, _const_spec(bias.shape)],
        out_specs=cur,
        out_shape=jax.ShapeDtypeStruct(q.shape, BF16),
        scratch_shapes=[pltpu.VMEM((3 * blk, w), BF16), pltpu.VMEM((3 * blk, 2 * w), BF16),
                        pltpu.VMEM((lc, 2 * w), BF16)],
        compiler_params=_cp("parallel", "parallel"),
        name="neighborhood_attention",
    )(q, k, k, k, v, v, v, kc, vc, bias)


def _ctx_attn_kernel(q_ref, k_ref, v_ref, o_ref):
    n_pairs = q_ref.shape[-1] // LANES
    outs = []
    for hp in range(n_pairs):
        sl = slice(hp * LANES, (hp + 1) * LANES)
        qp, kp, vp = q_ref[:, sl], k_ref[:, sl], v_ref[:, sl]
        o_pair = None
        for t, mask in enumerate(_half_masks(qp.shape)):
            qm = jnp.where(mask, qp, jnp.zeros_like(qp))
            vm = jnp.where(mask, vp, jnp.zeros_like(vp))
            o_t = _softmax_pv([_dot_nt(qm, kp)], [vm])
            o_pair = o_t if o_pair is None else o_pair + o_t
        outs.append(o_pair)
    o_ref[...] = jnp.concatenate(outs, axis=1).astype(o_ref.dtype)


def _ctx_attention(q, k, v):
    bsz, lc, w = q.shape
    spec = pl.BlockSpec((None, lc, w), lambda b: (b, 0, 0))
    return pl.pallas_call(
        _ctx_attn_kernel, grid=(bsz,), in_specs=[spec, spec, spec], out_specs=spec,
        out_shape=jax.ShapeDtypeStruct(q.shape, BF16),
        compiler_params=_cp("parallel"), name="ctx_attention",
    )(q, k, v)


def _conv_kernel(prev_ref, x_ref, next_ref, w_ref, b_ref, o_ref, *, act):
    i = pl.program_id(1)
    n = pl.num_programs(1)
    t = x_ref.shape[0]
    halo = prev_ref.shape[0]
    prev = jnp.where(i > 0, prev_ref[...], 0.0)
    nxt = jnp.where(i < n - 1, next_ref[...], 0.0)
    xx = jnp.concatenate([prev, x_ref[...], nxt], axis=0)
    taps = w_ref.shape[0]
    left = taps // 2
    tot = t + 2 * halo
    y = b_ref[...]
    for j in range(taps):
        sh = (left - j) % tot
        xs = xx if sh == 0 else pltpu.roll(xx, sh, axis=0)
        y = y + w_ref[j:j + 1, :] * xs[halo:halo + t]
    o_ref[...] = _silu(y) if act else y


def _dwconv(x, w, b, act):
    bsz, length, c = x.shape
    t = min(512, length)
    halo = 8
    per = t // halo
    nh = length // halo
    return pl.pallas_call(
        functools.partial(_conv_kernel, act=act),
        grid=(bsz, length // t),
        in_specs=[pl.BlockSpec((None, halo, c), lambda bb, i: (bb, jnp.maximum(i * per - 1, 0), 0)),
                  pl.BlockSpec((None, t, c), lambda bb, i: (bb, i, 0)),
                  pl.BlockSpec((None, halo, c), lambda bb, i: (bb, jnp.minimum((i + 1) * per, nh - 1), 0)),
                  _const_spec(w.shape), _const_spec((1, c))],
        out_specs=pl.BlockSpec((None, t, c), lambda bb, i: (bb, i, 0)),
        out_shape=jax.ShapeDtypeStruct(x.shape, F32),
        compiler_params=_cp("parallel", "parallel"),
        name="dwconv",
    )(x, x, x, w.astype(F32), b.astype(F32).reshape(1, c))


def _ssd_direction(xbc_ref, dtc_ref, dtr_ref, pc_ref, pr_ref, dl_ref, state, y_ref, *, d, reverse, add_skip):
    q = xbc_ref.shape[0]
    hp_w = LANES
    n_heads = SSD_HEADS
    gw = n_heads * HEAD_DIM
    ii = lax.broadcasted_iota(jnp.int32, (q, q), 0)
    jj = lax.broadcasted_iota(jnp.int32, (q, q), 1)
    keep = (ii <= jj) if reverse else (ii >= jj)
    tri = keep.astype(F32)
    dt_c = _softplus(dtc_ref[:, d * n_heads:(d + 1) * n_heads] + pc_ref[0:1, :])
    dt_r = _softplus(dtr_ref[d * n_heads:(d + 1) * n_heads, :] + pr_ref[:, 0:1])
    da_c = dt_c * pc_ref[1:2, :]
    da_r = dt_r * pr_ref[:, 1:2]
    acs_c = jnp.dot(tri, da_c, preferred_element_type=F32, precision=HIGHEST)
    acs_r = lax.dot_general(da_r, tri, (((1,), (1,)), ((), ())), preferred_element_type=F32,
                            precision=HIGHEST)
    edge = 0 if reverse else q - 1
    tot_r = acs_r[:, edge:edge + 1]
    e_in_c = jnp.exp(acs_c)
    e_end_r = jnp.exp(tot_r - acs_r)
    e_tot_r = jnp.exp(tot_r)
    lane_lo, lane_hi = _half_masks((q, hp_w))
    for g in range(2):
        bm = xbc_ref[:, gw + g * SSD_STATE:gw + (g + 1) * SSD_STATE]
        cm = xbc_ref[:, gw + 2 * SSD_STATE + g * SSD_STATE:gw + 2 * SSD_STATE + (g + 1) * SSD_STATE]
        bm_t = bm.T
        cm16 = cm.astype(BF16)
        cb = _dot(cm16, bm_t.astype(BF16))
        for pp in range(2):
            pair = g * 2 + pp
            h0 = 2 * pair
            sl = slice(pair * hp_w, (pair + 1) * hp_w)
            xs = xbc_ref[:, sl]
            xdt = xs * jnp.where(lane_lo, dt_c[:, h0:h0 + 1], dt_c[:, h0 + 1:h0 + 2])
            y_pair = None
            st_new = None
            for t, lmask in enumerate((lane_lo, lane_hi)):
                h = h0 + t
                seg = acs_c[:, h:h + 1] - acs_r[h:h + 1, :]
                dec = jnp.where(keep, jnp.exp(jnp.where(keep, seg, 0.0)), 0.0)
                xm = jnp.where(lmask, xdt, 0.0).astype(BF16)
                yd = _dot((cb * dec).astype(BF16), xm)
                sn = _dot((bm_t * e_end_r[h:h + 1, :]).astype(BF16), xm)
                y_pair = yd if y_pair is None else y_pair + yd
                st_new = sn if st_new is None else st_new + sn
            st_old = state[:, sl]
            e_in = jnp.where(lane_lo, e_in_c[:, h0:h0 + 1], e_in_c[:, h0 + 1:h0 + 2])
            y_pair = y_pair + _dot(cm16, st_old.astype(BF16)) * e_in
            if add_skip:
                y_pair = y_pair + dl_ref[:, sl] * xs
            y_ref[:, sl] = y_pair
            e_tot = jnp.where(lane_lo[0:1], e_tot_r[h0:h0 + 1, :], e_tot_r[h0 + 1:h0 + 2, :])
            state[:, sl] = st_old * e_tot + st_new


def _ssd_kernel(xf_ref, xb_ref, dtcf_ref, dtcb_ref, dtrf_ref, dtrb_ref, pc_ref, pr_ref, dl_ref, h0f_ref, h0b_ref,
                yf_ref, yb_ref, hf_ref, hb_ref, sf, sb):
    c = pl.program_id(1)

    @pl.when(c == 0)
    def _():
        sf[...] = h0f_ref[...]
        sb[...] = h0b_ref[...]

    _ssd_direction(xf_ref, dtcf_ref, dtrf_ref, pc_ref.at[0], pr_ref.at[0], dl_ref, sf, yf_ref,
                   d=0, reverse=False, add_skip=True)
    _ssd_direction(xb_ref, dtcb_ref, dtrb_ref, pc_ref.at[1], pr_ref.at[1], dl_ref, sb, yb_ref,
                   d=1, reverse=True, add_skip=False)
    hf_ref[...] = sf[...]
    hb_ref[...] = sb[...]


def _ssd(xbc, dt, pc, pr, dl, h0f, h0b):
    bsz, length, cw = xbc.shape
    q = SSD_CHUNK
    nc = length // q
    gw = SSD_HEADS * HEAD_DIM
    dtw = dt.shape[-1]
    dt_t = jnp.swapaxes(dt, 1, 2)
    fwd3 = lambda b, c: (b, c, 0)
    bwd3 = lambda b, c: (b, nc - 1 - c, 0)
    st_spec = pl.BlockSpec((None, SSD_STATE, gw), lambda b, c: (b, 0, 0))
    return pl.pallas_call(
        _ssd_kernel,
        grid=(bsz, nc),
        in_specs=[pl.BlockSpec((None, q, cw), fwd3), pl.BlockSpec((None, q, cw), bwd3),
                  pl.BlockSpec((None, q, dtw), fwd3), pl.BlockSpec((None, q, dtw), bwd3),
                  pl.BlockSpec((None, dtw, q), lambda b, c: (b, 0, c)),
                  pl.BlockSpec((None, dtw, q), lambda b, c: (b, 0, nc - 1 - c)),
                  _const_spec(pc.shape), _const_spec(pr.shape), _const_spec(dl.shape), st_spec, st_spec],
        out_specs=[pl.BlockSpec((None, q, gw), fwd3), pl.BlockSpec((None, q, gw), bwd3), st_spec, st_spec],
        out_shape=[jax.ShapeDtypeStruct((bsz, length, gw), F32)] * 2
                  + [jax.ShapeDtypeStruct((bsz, SSD_STATE, gw), F32)] * 2,
        scratch_shapes=[pltpu.VMEM((SSD_STATE, gw), F32)] * 2,
        compiler_params=_cp("parallel", "arbitrary"),
        name="ssd_scan",
    )(xbc, xbc, dt, dt, dt_t, dt_t, pc, pr, dl, h0f, h0b)


def _gelu_tanh(x):
    return 0.5 * x * (1.0 + jnp.tanh(math.sqrt(2.0 / math.pi) * (x + 0.044715 * (x * x * x))))


def _mix_ab(ona_ref, yf_ref, yb_ref, z_ref, ng_ref, w_ref):
    gw = ona_ref.shape[-1]
    y = (yf_ref[...] + yb_ref[...]) * _silu(z_ref[...])
    gated = y * lax.rsqrt(jnp.mean(y * y, axis=-1, keepdims=True) + EPS) * ng_ref[...]
    return _dot(ona_ref[...], w_ref[0:gw, :]) + _dot(gated.astype(BF16), w_ref[gw:2 * gw, :])


def _mix_cd(gate_ref, hf_ref, hb_ref, od_ref, w_ref):
    gw = od_ref.shape[-1]
    lru = _gelu_tanh(gate_ref[...]) * (hf_ref[...] + hb_ref[...])
    return _dot(lru.astype(BF16), w_ref[0:gw, :]) + _dot(od_ref[...], w_ref[gw:2 * gw, :])


def _mix_ffn_kernel(x_ref, mod_ref, *rest, mix, n_mix, k, chunk):
    g_ref, w1_ref, w3_ref, w2_ref, o_ref = rest[n_mix:]
    x = x_ref[...] + mod_ref[5:6, :] * mix(*rest[:n_mix])
    o_ref[...] = _ffn_body(x, mod_ref, g_ref, w1_ref, w3_ref, w2_ref, k, chunk)


def _mixer_out_ffn(mix, name, x, mod, parts, consts, g, w1, w3, w2, k):
    bsz, length, d = x.shape
    ff = w1.shape[1]
    tm = min(512, length)
    tok = lambda a: pl.BlockSpec((None, tm, a.shape[-1]), lambda b, i: (b, i, 0))
    return pl.pallas_call(
        functools.partial(_mix_ffn_kernel, mix=mix, n_mix=len(parts) + len(consts), k=k, chunk=ff // 2),
        grid=(bsz, length // tm),
        in_specs=[tok(x), _mod_spec(mod)] + [tok(a) for a in parts] + [_const_spec(a.shape) for a in consts]
                 + [_const_spec((1, d)), _const_spec((d, ff)), _const_spec((d, ff)), _const_spec((ff, d))],
        out_specs=tok(x),
        out_shape=jax.ShapeDtypeStruct(x.shape, F32),
        compiler_params=pltpu.CompilerParams(dimension_semantics=("parallel", "parallel"),
                                             vmem_limit_bytes=DIFF_VMEM_LIMIT),
        name=name,
    )(x, mod, *parts, *consts, g.reshape(1, d), w1, w3, w2)


def _lru_direction(x_ref, wa_ref, wx_ref, p_ref, carry, h_ref, *, reverse):
    t = x_ref.shape[0]
    x = x_ref[...]
    x16 = x.astype(BF16)
    r = _sigmoid(_dot(x16, wa_ref[...]) + p_ref[0:1, :])
    ig = _sigmoid(_dot(x16, wx_ref[...]) + p_ref[1:2, :])
    log_a = -LRU_C * r * _softplus(-p_ref[2:3, :])
    a = jnp.exp(log_a)
    b = jnp.sqrt(1.0 - a * a) * (ig * x)
    row = lax.broadcasted_iota(jnp.int32, a.shape, 0) % SUBLANES
    s = 1
    while s < SUBLANES:
        if reverse:
            fill = row >= SUBLANES - s
            a_sh = jnp.where(fill, 1.0, pltpu.roll(a, t - s, axis=0))
            b_sh = jnp.where(fill, 0.0, pltpu.roll(b, t - s, axis=0))
        else:
            fill = row < s
            a_sh = jnp.where(fill, 1.0, pltpu.roll(a, s, axis=0))
            b_sh = jnp.where(fill, 0.0, pltpu.roll(b, s, axis=0))
        b = a * b_sh + b
        a = a * a_sh
        s *= 2
    h_prev = carry[...]
    n_groups = t // SUBLANES
    for gi in (range(n_groups - 1, -1, -1) if reverse else range(n_groups)):
        r0 = gi * SUBLANES
        hg = a[r0:r0 + SUBLANES] * h_prev + b[r0:r0 + SUBLANES]
        h_ref[r0:r0 + SUBLANES, :] = hg
        h_prev = hg[0:1] if reverse else hg[SUBLANES - 1:SUBLANES]
    carry[...] = h_prev


def _lru_kernel(xf_ref, xb_ref, wa_ref, wx_ref, p_ref, h0f_ref, h0b_ref, hf_ref, hb_ref, lf_ref, lb_ref, cf, cb):
    c = pl.program_id(1)

    @pl.when(c == 0)
    def _():
        cf[...] = h0f_ref[...]
        cb[...] = h0b_ref[...]

    _lru_direction(xf_ref, wa_ref.at[0], wx_ref.at[0], p_ref.at[0], cf, hf_ref, reverse=False)
    _lru_direction(xb_ref, wa_ref.at[1], wx_ref.at[1], p_ref.at[1], cb, hb_ref, reverse=True)
    lf_ref[...] = cf[...]
    lb_ref[...] = cb[...]


def _lru(x, wa, wx, p, h0f, h0b):
    bsz, length, w = x.shape
    t = min(256, length)
    nt = length // t
    fwd = lambda b, c: (b, c, 0)
    bwd = lambda b, c: (b, nt - 1 - c, 0)
    st = pl.BlockSpec((None, 1, w), lambda b, c: (b, 0, 0))
    return pl.pallas_call(
        _lru_kernel,
        grid=(bsz, nt),
        in_specs=[pl.BlockSpec((None, t, w), fwd), pl.BlockSpec((None, t, w), bwd),
                  _const_spec(wa.shape), _const_spec(wx.shape), _const_spec(p.shape), st, st],
        out_specs=[pl.BlockSpec((None, t, w), fwd), pl.BlockSpec((None, t, w), bwd), st, st],
        out_shape=[jax.ShapeDtypeStruct(x.shape, F32)] * 2 + [jax.ShapeDtypeStruct((bsz, 1, w), F32)] * 2,
        scratch_shapes=[pltpu.VMEM((1, w), F32)] * 2,
        compiler_params=_cp("parallel", "arbitrary"),
        name="rglru_scan",
    )(x, x, wa, wx, p, h0f, h0b)


def _block_diag(wb):
    nb, bs, _ = wb.shape
    eye = jnp.eye(nb, dtype=wb.dtype)
    return (wb[:, :, None, :] * eye[:, None, :, None]).reshape(nb * bs, nb * bs)


def _diff_attn_kernel(q_ref, kt_ref, v_ref, lam_ref, sg_ref, o_ref, qs, m_s, acc_s, s_buf, *, lam_init, n_heads, tk):
    tq = q_ref.shape[0]
    vw = 2 * LANES
    for h in range(n_heads):
        qp = q_ref[:, h * LANES:(h + 1) * LANES]
        lo, hi = _half_masks(qp.shape)
        qs[h, 0:tq, :] = jnp.where(lo, qp, jnp.zeros_like(qp))
        qs[h, tq:2 * tq, :] = jnp.where(hi, qp, jnp.zeros_like(qp))
    m_s[...] = jnp.full(m_s.shape, NEG_BIG, F32)
    acc_s[...] = jnp.zeros(acc_s.shape, F32)

    n_chunks = kt_ref.shape[1] // tk

    def scores(h, k0):
        return _dot(qs[h], kt_ref[h * LANES:(h + 1) * LANES, pl.ds(k0, tk)])

    s_buf[0] = scores(0, 0)

    def chunk(c, carry):
        k0 = pl.multiple_of(c * tk, tk)
        k_next = pl.multiple_of(jnp.minimum(c + 1, n_chunks - 1) * tk, tk)
        for h in range(n_heads):
            if h + 1 < n_heads:
                s_buf[(h + 1) % 2] = scores(h + 1, k0)
            else:
                s_buf[0] = scores(0, k_next)
            s = s_buf[h % 2]
            m_old = m_s[h]
            m_new = jnp.maximum(m_old, s.max(axis=-1, keepdims=True))
            alpha = jnp.exp2(m_old - m_new)
            p = jnp.concatenate([jnp.exp2(s[:, j * LANES:(j + 1) * LANES] - m_new).astype(BF16)
                                 for j in range(tk // LANES)], axis=1)
            pv = _dot(p, v_ref[pl.ds(k0, tk), h * vw:(h + 1) * vw])
            acc_s[h] = jnp.concatenate([alpha, alpha], axis=1) * acc_s[h] + pv
            m_s[h] = m_new
        return carry

    lax.fori_loop(0, n_chunks, chunk, 0)

    dl = lam_ref[...]
    lam = (jnp.exp(jnp.sum(dl[0:1] * dl[1:2], axis=-1, keepdims=True))
           - jnp.exp(jnp.sum(dl[2:3] * dl[3:4], axis=-1, keepdims=True)) + lam_init)
    for h in range(n_heads):
        a = acc_s[h]
        o = a[0:tq, 0:LANES] / a[0:tq, LANES:vw] - lam * (a[tq:2 * tq, 0:LANES] / a[tq:2 * tq, LANES:vw])
        o = o * lax.rsqrt(jnp.mean(o * o, axis=-1, keepdims=True) + EPS) * sg_ref[...] * (1.0 - lam_init)
        o_ref[:, h * LANES:(h + 1) * LANES] = o.astype(o_ref.dtype)


def _diff_attention(q, kt_all, v_ext, diff_lam, subln_g, lam_init):
    bsz, s, w = q.shape
    lk = kt_all.shape[2]
    n_heads = w // LANES
    tq = min(DIFF_TQ, s)
    tk = next(c for c in (768, 512, 256, 128) if lk % c == 0)
    return pl.pallas_call(
        functools.partial(_diff_attn_kernel, lam_init=lam_init, n_heads=n_heads, tk=tk),
        grid=(bsz, s // tq),
        in_specs=[pl.BlockSpec((None, tq, w), lambda b, i: (b, i, 0)),
                  pl.BlockSpec((None, w, lk), lambda b, i: (b, 0, 0), pipeline_mode=pl.Buffered(1)),
                  pl.BlockSpec((None, lk, v_ext.shape[2]), lambda b, i: (b, 0, 0), pipeline_mode=pl.Buffered(1)),
                  _const_spec(diff_lam.shape), _const_spec((1, LANES))],
        out_specs=pl.BlockSpec((None, tq, w), lambda b, i: (b, i, 0)),
        out_shape=jax.ShapeDtypeStruct(q.shape, BF16),
        scratch_shapes=[pltpu.VMEM((n_heads, 2 * tq, LANES), BF16), pltpu.VMEM((n_heads, 2 * tq, LANES), F32),
                        pltpu.VMEM((n_heads, 2 * tq, 2 * LANES), F32), pltpu.VMEM((2, 2 * tq, tk), F32)],
        compiler_params=pltpu.CompilerParams(dimension_semantics=("parallel", "parallel"),
                                             vmem_limit_bytes=DIFF_VMEM_LIMIT),
        name="diff_attention",
    )(q, kt_all, v_ext, diff_lam.astype(F32), subln_g.astype(F32).reshape(1, LANES))


def _pad_cols(w, total):
    return jnp.pad(w, ((0, 0), (0, total - w.shape[1])))


def _layer_ab(x, xc, m, mc, g_mix, w_in, w_out, q_g, k_g, rpb, conv_w, conv_b, dt_bias, a_log, d_skip, norm_g,
              ffn2):
    gw = SSD_HEADS * HEAD_DIM
    segs = [_Seg(gw, 'norm', gain=q_g, scale=HEAD_DIM ** -0.5 * LOG2E), _Seg(gw, 'norm', gain=k_g),
            _Seg(gw, dtype=BF16), _Seg(gw), _Seg(2 * gw), _Seg(LANES)]
    w16 = _pad_cols(w_in, sum(s.width for s in segs)).astype(BF16)
    pmat = _group_mean_matrix(gw, HEAD_DIM)
    q, k, v, z, xbc, dt = _norm_proj(x, m, g_mix, w16, 1, segs, pmat)
    q_c, k_c, v_c, z_c, xbc_c, dt_c = _norm_proj(xc, mc, g_mix, w16, 1, segs, pmat)
    o_na = _neighborhood_attention(q, k, v, k_c, v_c, _na_bias_table(rpb))
    o_c = _ctx_attention(q_c, k_c, v_c)

    a_neg = -jnp.exp(a_log.astype(F32))
    pc = jnp.stack([dt_bias.astype(F32), a_neg], axis=1)
    pr = jnp.swapaxes(pc, 1, 2)
    dl = jnp.repeat(d_skip.astype(F32), HEAD_DIM).reshape(1, gw)
    cw = conv_w.astype(F32)
    xbc_c = _dwconv(xbc_c, cw, conv_b, act=True)
    xbc = _dwconv(xbc, cw, conv_b, act=True)
    zeros = jnp.zeros((x.shape[0], SSD_STATE, gw), F32)
    yf_c, yb_c, hf_c, hb_c = _ssd(xbc_c, dt_c, pc, pr, dl, zeros, zeros)
    yf, yb, _, _ = _ssd(xbc, dt, pc, pr, dl, hf_c, hb_c)

    ng = norm_g.astype(F32).reshape(1, gw)
    wo16 = w_out.astype(BF16)
    x = _mixer_out_ffn(_mix_ab, "mixer_out_ab_ffn", x, m, [o_na, yf, yb, z], [ng, wo16], *ffn2, 2)
    xc = _mixer_out_ffn(_mix_ab, "mixer_out_ab_ffn", xc, mc, [o_c, yf_c, yb_c, z_c], [ng, wo16], *ffn2, 2)
    return x, xc


def _layer_cd(x, xc, m, mc, g_mix, w_in, w_out, conv_w, conv_b, wa, ba, wx, bx, lam_p, q_g, k_g, diff_lam,
              subln_g, lam_init, ffn2):
    gw = w_out.shape[0] // 2
    w16 = w_in.astype(BF16)
    pmat = _group_mean_matrix(gw, HEAD_DIM)
    tabs = _rope_tables(x.shape[1])
    q_scale = HEAD_DIM ** -0.5 * LOG2E

    def segs(rope):
        return [_Seg(gw), _Seg(gw), _Seg(gw, 'norm', gain=q_g, scale=q_scale, rope=rope),
                _Seg(gw, 'norm_t', gain=k_g, rope=rope), _Seg(gw, 'vext')]

    gate, xr, q, kt, v_ext = _norm_proj(x, m, g_mix, w16, 1, segs(True), pmat, tabs)
    _, xr_c, _, kt_c, v_ext_c = _norm_proj(xc, mc, g_mix, w16, 1, segs(False), pmat)

    cw = conv_w.astype(F32)
    xr_c = _dwconv(xr_c, cw, conv_b, act=False)
    xr = _dwconv(xr, cw, conv_b, act=False)
    wa_d = jnp.stack([_block_diag(wa[0]), _block_diag(wa[1])]).astype(BF16)
    wx_d = jnp.stack([_block_diag(wx[0]), _block_diag(wx[1])]).astype(BF16)
    p = jnp.stack([ba.astype(F32), bx.astype(F32), lam_p.astype(F32)], axis=1)
    zeros = jnp.zeros((x.shape[0], 1, gw), F32)
    _, _, lf_c, lb_c = _lru(xr_c, wa_d, wx_d, p, zeros, zeros)
    hf, hb, _, _ = _lru(xr, wa_d, wx_d, p, lf_c, lb_c)

    kt_all = jnp.concatenate([kt_c, kt], axis=2)
    v_all = jnp.concatenate([v_ext_c, v_ext], axis=1)
    o = _diff_attention(q, kt_all, v_all, diff_lam, subln_g, lam_init)
    return _mixer_out_ffn(_mix_cd, "mixer_out_cd_ffn", x, m, [gate, hf, hb, o], [w_out.astype(BF16)], *ffn2, 2)


def kernel(x, c, ctx, c_ctx, w_mod, b_mod, norm_g, ffn_w1, ffn_w3, ffn_w2, ab_w_in, ab_w_out, na_q_g, na_k_g, na_rpb, ssd_conv_w, ssd_conv_b, ssd_dt_bias, ssd_a_log, ssd_d, ssd_norm_g, cd_w_in, cd_w_out, lru_conv_w, lru_conv_b, lru_wa, lru_ba, lru_wx, lru_bx, lru_lambda, diff_q_g, diff_k_g, diff_lambda, diff_subln_g):
    bsz, _, d = x.shape
    depth = w_mod.shape[0]
    cc = jnp.concatenate([c.astype(F32), c_ctx.astype(F32)[None], jnp.zeros((8 - bsz - 1, d), F32)], axis=0)
    mods = _modulation(cc, w_mod.astype(F32), b_mod.astype(F32))
    xc = ctx
    for i in range(depth):
        last = i == depth - 1
        j = i // 2
        m = mods[i, :bsz].reshape(bsz, N_MOD, d)
        mc = mods[i, bsz:bsz + 1].reshape(1, N_MOD, d)
        g = norm_g[i].astype(F32)
        w1 = ffn_w1[i].astype(BF16)
        w3 = ffn_w3[i].astype(BF16)
        w2 = ffn_w2[i].astype(BF16)
        x = _ffn(x, m, g[0], w1[0], w3[0], w2[0], 0)
        xc = _ffn(xc, mc, g[0], w1[0], w3[0], w2[0], 0)
        ffn2 = (g[2], w1[1], w3[1], w2[1])
        if i % 2 == 0:
            x, xc = _layer_ab(x, xc, m, mc, g[1], ab_w_in[j], ab_w_out[j], na_q_g[j], na_k_g[j], na_rpb[j],
                              ssd_conv_w[j], ssd_conv_b[j], ssd_dt_bias[j], ssd_a_log[j], ssd_d[j], ssd_norm_g[j],
                              ffn2)
        else:
            assert last, "a C|D layer that is not the last one would also need the context stream's mixer output"
            lam_init = 0.8 - 0.6 * math.exp(-0.3 * i)
            x = _layer_cd(x, xc, m, mc, g[1], cd_w_in[j], cd_w_out[j], lru_conv_w[j], lru_conv_b[j], lru_wa[j],
                          lru_ba[j], lru_wx[j], lru_bx[j], lru_lambda[j], diff_q_g[j], diff_k_g[j],
                          diff_lambda[j], diff_subln_g[j], lam_init, ffn2)
    return x
```

```python
import functools
import math

import jax
import jax.numpy as jnp
from jax import lax
from jax.experimental import pallas as pl
from jax.experimental.pallas import tpu as pltpu

F32 = jnp.float32
BF16 = jnp.bfloat16
HIGHEST = lax.Precision.HIGHEST

GRID_W = 64
EPS = 1e-6
HEAD_DIM = 64
N_MOD = 9
NA_WIN_R = 8
NA_WIN_C = 16
SSD_STATE = 128
SSD_CHUNK = 128
SSD_HEADS = 8
LRU_C = 8.0
ROPE_BASE = 10000.0
NEG_BIG = -1e30

LANES = 128
SUBLANES = 8
VMEM_LIMIT = 48 * 1024 * 1024
DIFF_VMEM_LIMIT = 56 * 1024 * 1024
DIFF_TQ = 512
FFN_ROW_BLOCKS = 2
NA_ROWS_PER_ITER = 2
MXU_TILE = 256
FFN_CHUNK = 6 * MXU_TILE
LOG2E = math.log2(math.e)


def _cp(*sem):
    return pltpu.CompilerParams(dimension_semantics=sem, vmem_limit_bytes=VMEM_LIMIT)


def _const_spec(shape):
    nd = len(shape)
    return pl.BlockSpec(shape, lambda *_: (0,) * nd, pipeline_mode=pl.Buffered(1))


def _dot(a, b):
    return jnp.dot(a, b, preferred_element_type=F32)


def _dot_nt(a, b):
    return lax.dot_general(a, b, (((1,), (1,)), ((), ())), preferred_element_type=F32)


def _sigmoid(x):
    return 0.5 * jnp.tanh(0.5 * x) + 0.5


def _silu(x):
    return x * _sigmoid(x)


def _softplus(x):
    return jnp.maximum(x, 0.0) + jnp.log(1.0 + jnp.exp(-jnp.abs(x)))


def _rms_mod(x, g, shift, scale):
    ms = jnp.mean(x * x, axis=-1, keepdims=True)
    return (x * lax.rsqrt(ms + EPS) * g) * (1.0 + scale) + shift


def _mod_kernel(c_ref, w_ref, b_ref, o_ref):
    s = _silu(c_ref[...])
    o_ref[...] = jnp.dot(s, w_ref[...], preferred_element_type=F32, precision=HIGHEST) + b_ref[...]


def _modulation(cc, w_mod, b_mod):
    depth, d, n = w_mod.shape
    tn = 1024
    return pl.pallas_call(
        _mod_kernel,
        grid=(depth, n // tn),
        in_specs=[pl.BlockSpec((8, d), lambda l, j: (0, 0)),
                  pl.BlockSpec((None, d, tn), lambda l, j: (l, 0, j)),
                  pl.BlockSpec((None, 1, tn), lambda l, j: (l, 0, j))],
        out_specs=pl.BlockSpec((None, 8, tn), lambda l, j: (l, 0, j)),
        out_shape=jax.ShapeDtypeStruct((depth, 8, n), F32),
        compiler_params=_cp("parallel", "parallel"),
        name="modulation",
    )(cc, w_mod, b_mod.reshape(depth, 1, n))


def _mod_spec(mod):
    if mod.shape[0] == 1:
        return pl.BlockSpec((None, N_MOD, mod.shape[2]), lambda b, i: (0, 0, 0))
    return pl.BlockSpec((None, N_MOD, mod.shape[2]), lambda b, i: (b, 0, 0))


def _ffn_body(x, mod_ref, g_ref, w1_ref, w3_ref, w2_ref, k, chunk):
    tm = x.shape[0]
    rb = tm // FFN_ROW_BLOCKS if tm % (FFN_ROW_BLOCKS * SUBLANES) == 0 else tm
    ff = w1_ref.shape[1]
    outs = []
    for r0 in range(0, tm, rb):
        xr = x[r0:r0 + rb]
        h = _rms_mod(xr, g_ref[...], mod_ref[3 * k:3 * k + 1, :], mod_ref[3 * k + 1:3 * k + 2, :]).astype(BF16)
        acc = jnp.zeros(xr.shape, F32)
        for c0 in range(0, ff, chunk):
            c1 = min(c0 + chunk, ff)
            a = _dot(h, w1_ref[:, c0:c1])
            b = _dot(h, w3_ref[:, c0:c1])
            acc = acc + _dot((_silu(a) * b).astype(BF16), w2_ref[c0:c1, :])
        outs.append(xr + (0.5 * mod_ref[3 * k + 2:3 * k + 3, :]) * acc)
    return outs[0] if len(outs) == 1 else jnp.concatenate(outs, axis=0)


def _ffn_kernel(x_ref, mod_ref, g_ref, w1_ref, w3_ref, w2_ref, o_ref, *, k, chunk):
    o_ref[...] = _ffn_body(x_ref[...], mod_ref, g_ref, w1_ref, w3_ref, w2_ref, k, chunk)


def _ffn(x, mod, g, w1, w3, w2, k):
    bsz, length, d = x.shape
    ff = w1.shape[1]
    tm = min(512, length)
    return pl.pallas_call(
        functools.partial(_ffn_kernel, k=k, chunk=FFN_CHUNK),
        grid=(bsz, length // tm),
        in_specs=[pl.BlockSpec((None, tm, d), lambda b, i: (b, i, 0)),
                  _mod_spec(mod),
                  _const_spec((1, d)),
                  _const_spec((d, ff)), _const_spec((d, ff)), _const_spec((ff, d))],
        out_specs=pl.BlockSpec((None, tm, d), lambda b, i: (b, i, 0)),
        out_shape=jax.ShapeDtypeStruct(x.shape, F32),
        compiler_params=_cp("parallel", "parallel"),
        name="half_ffn",
    )(x, mod, g.reshape(1, d), w1, w3, w2)


class _Seg:
    def __init__(self, width, kind='plain', dtype=F32, gain=None, scale=1.0, rope=False):
        self.width, self.kind, self.dtype, self.gain, self.scale, self.rope = width, kind, dtype, gain, scale, rope


def _head_norm(y, gain, pmat, scale, rope_tabs):
    ms = _dot((y * y).astype(BF16), pmat)
    y = y * lax.rsqrt(ms + EPS) * gain
    if rope_tabs is not None:
        reps = y.shape[-1] // rope_tabs[0].shape[-1]
        cs = jnp.concatenate([rope_tabs[0]] * reps, axis=1)
        sn = jnp.concatenate([rope_tabs[1]] * reps, axis=1)
        y = y * cs + _swap_pairs(y) * sn
    return y * scale


def _norm_proj_kernel(x_ref, mod_ref, g_ref, w_ref, *rest, k, segs, n_gain, has_rope):
    pos = 0
    pmat = None
    if n_gain:
        pmat = rest[0][...]
        pos = 1
    gains = rest[pos:pos + n_gain]
    pos += n_gain
    rope_tabs = None
    if has_rope:
        rope_tabs = (rest[pos][...], rest[pos + 1][...])
        pos += 2
    o_refs = rest[pos:]
    h = _rms_mod(x_ref[...], g_ref[...], mod_ref[3 * k:3 * k + 1, :], mod_ref[3 * k + 1:3 * k + 2, :]).astype(BF16)
    off, gi = 0, 0
    for seg, o_ref in zip(segs, o_refs):
        y = _dot(h, w_ref[:, off:off + seg.width])
        off += seg.width
        if seg.kind in ('norm', 'norm_t'):
            y = _head_norm(y, gains[gi][...], pmat, seg.scale, rope_tabs if seg.rope else None)
            gi += 1
            o_ref[...] = (y.T if seg.kind == 'norm_t' else y).astype(o_ref.dtype)
        elif seg.kind == 'vext':
            y16 = y.astype(BF16)
            ones = jnp.ones((y.shape[0], LANES), BF16)
            for hp in range(seg.width // LANES):
                o_ref[:, 2 * hp * LANES:(2 * hp + 1) * LANES] = y16[:, hp * LANES:(hp + 1) * LANES]
                o_ref[:, (2 * hp + 1) * LANES:(2 * hp + 2) * LANES] = ones
        else:
            o_ref[...] = y.astype(o_ref.dtype)


def _norm_proj(x, mod, g, w, k, segs, pmat=None, rope_tabs=None):
    bsz, length, d = x.shape
    tm = min(512, length)
    tok = lambda wd: pl.BlockSpec((None, tm, wd), lambda b, i: (b, i, 0))
    in_specs = [tok(d), _mod_spec(mod), _const_spec((1, d)), _const_spec(w.shape)]
    args = [x, mod, g.reshape(1, d), w]
    gains = [s for s in segs if s.kind in ('norm', 'norm_t')]
    if gains:
        in_specs.append(_const_spec(pmat.shape))
        args.append(pmat)
        for s in gains:
            in_specs.append(_const_spec((1, s.width)))
            args.append(jnp.tile(s.gain.astype(F32), s.width // s.gain.shape[0]).reshape(1, s.width))
    has_rope = any(s.rope for s in segs)
    if has_rope:
        in_specs += [pl.BlockSpec((tm, rope_tabs[0].shape[-1]), lambda b, i: (i, 0))] * 2
        args += list(rope_tabs)
    out_specs, out_shape = [], []
    for s in segs:
        if s.kind == 'norm_t':
            out_specs.append(pl.BlockSpec((None, s.width, tm), lambda b, i: (b, 0, i)))
            out_shape.append(jax.ShapeDtypeStruct((bsz, s.width, length), BF16))
        elif s.kind == 'vext':
            out_specs.append(tok(2 * s.width))
            out_shape.append(jax.ShapeDtypeStruct((bsz, length, 2 * s.width), BF16))
        else:
            out_specs.append(tok(s.width))
            out_shape.append(jax.ShapeDtypeStruct((bsz, length, s.width), BF16 if s.kind == 'norm' else s.dtype))
    return pl.pallas_call(
        functools.partial(_norm_proj_kernel, k=k, segs=tuple(segs), n_gain=len(gains), has_rope=has_rope),
        grid=(bsz, length // tm),
        in_specs=in_specs,
        out_specs=out_specs,
        out_shape=out_shape,
        compiler_params=_cp("parallel", "parallel"),
        name="norm_proj",
    )(*args)


def _swap_pairs(x):
    n = x.shape[-1]
    lane = lax.broadcasted_iota(jnp.int32, x.shape, 1)
    return jnp.where(lane % 2 == 0, pltpu.roll(x, n - 1, axis=1), pltpu.roll(x, 1, axis=1))


def _group_mean_matrix(w, group):
    idx = jnp.arange(w) // group
    return jnp.where(idx[:, None] == idx[None, :], 1.0 / group, 0.0).astype(BF16)


def _rope_tables(length):
    t = jnp.arange(length)
    row = (t // GRID_W).astype(F32)
    col = (t % GRID_W).astype(F32)
    n = HEAD_DIM // 4
    inv = ROPE_BASE ** (-jnp.arange(n, dtype=F32) / n)
    ang = jnp.concatenate([row[:, None] * inv, col[:, None] * inv], axis=-1)
    cos = jnp.repeat(jnp.cos(ang), 2, axis=-1)
    sin = jnp.repeat(jnp.sin(ang), 2, axis=-1)
    sign = jnp.tile(jnp.array([-1.0, 1.0], F32), HEAD_DIM // 2)
    reps = LANES // HEAD_DIM
    return jnp.tile(cos, (1, reps)), jnp.tile(sin * sign, (1, reps))


def _half_masks(shape):
    lane = lax.broadcasted_iota(jnp.int32, shape, len(shape) - 1)
    return lane < HEAD_DIM, lane >= HEAD_DIM


def _softmax_pv(s_list, v_list):
    m = s_list[0].max(axis=-1, keepdims=True)
    for s in s_list[1:]:
        m = jnp.maximum(m, s.max(axis=-1, keepdims=True))
    acc, l = None, None
    for s, v in zip(s_list, v_list):
        p = jnp.exp2(s - m)
        ls = p.sum(axis=-1, keepdims=True)
        o = _dot(p.astype(BF16), v)
        acc = o if acc is None else acc + o
        l = ls if l is None else l + ls
    return acc / l


def _na_kernel(q_ref, kp_ref, kc_ref, kn_ref, vp_ref, vc_ref, vn_ref, kctx_ref, vctx_ref, bias_ref,
               o_ref, kbuf, vbuf, vcbuf, *, rows_per_blk, n_rows):
    i = pl.program_id(1)
    blk = rows_per_blk * GRID_W
    n_pairs = q_ref.shape[-1] // LANES
    vw = 2 * LANES
    kbuf[0:blk, :] = kp_ref[...]
    kbuf[blk:2 * blk, :] = kc_ref[...]
    kbuf[2 * blk:3 * blk, :] = kn_ref[...]
    for hp in range(n_pairs):
        sl = slice(hp * LANES, (hp + 1) * LANES)
        vbuf[0:blk, hp * vw:hp * vw + LANES] = vp_ref[:, sl]
        vbuf[blk:2 * blk, hp * vw:hp * vw + LANES] = vc_ref[:, sl]
        vbuf[2 * blk:3 * blk, hp * vw:hp * vw + LANES] = vn_ref[:, sl]
        vbuf[:, hp * vw + LANES:(hp + 1) * vw] = jnp.ones((3 * blk, LANES), BF16)
        vcbuf[:, hp * vw:hp * vw + LANES] = vctx_ref[:, sl]
        vcbuf[:, hp * vw + LANES:(hp + 1) * vw] = jnp.ones((vcbuf.shape[0], LANES), BF16)
    win = NA_WIN_R * GRID_W

    lo, hi = _half_masks((GRID_W, LANES))

    def rows_body(jj, carry):
        rows = [jj * NA_ROWS_PER_ITER + u for u in range(NA_ROWS_PER_ITER)]
        offs, units = [], []
        for j in rows:
            r = i * rows_per_blk + j
            r0 = jnp.clip(r - NA_WIN_R // 2, 0, n_rows - NA_WIN_R)
            off = pl.multiple_of((r0 - (i - 1) * rows_per_blk) * GRID_W, GRID_W)
            cfg = r0 - r + NA_WIN_R - 1
            offs.append(off)
            qrow = q_ref[pl.ds(pl.multiple_of(j * GRID_W, GRID_W), GRID_W), :]
            for hp in range(n_pairs):
                sl = slice(hp * LANES, (hp + 1) * LANES)
                qp = qrow[:, sl]
                zero = jnp.zeros_like(qp)
                qst = jnp.concatenate([jnp.where(lo, qp, zero), jnp.where(hi, qp, zero)], axis=0)
                units.append((_dot_nt(qst, kbuf[pl.ds(off, win), sl]) + bias_ref[cfg, hp],
                              _dot_nt(qst, kctx_ref[:, sl])))
        probs = []
        for s_lat, s_ctx in units:
            m = jnp.maximum(s_lat.max(axis=-1, keepdims=True), s_ctx.max(axis=-1, keepdims=True))
            probs.append((jnp.exp2(s_lat - m).astype(BF16), jnp.exp2(s_ctx - m).astype(BF16)))
        for u, j in enumerate(rows):
            outs = []
            for hp in range(n_pairs):
                p_lat, p_ctx = probs[u * n_pairs + hp]
                ov = (_dot(p_lat, vbuf[pl.ds(offs[u], win), hp * vw:(hp + 1) * vw])
                      + _dot(p_ctx, vcbuf[:, hp * vw:(hp + 1) * vw]))
                o = ov[:, 0:LANES] / ov[:, LANES:vw]
                outs.append(jnp.where(lo, o[0:GRID_W], o[GRID_W:2 * GRID_W]))
            o_ref[pl.ds(pl.multiple_of(j * GRID_W, GRID_W), GRID_W), :] = (
                jnp.concatenate(outs, axis=1).astype(o_ref.dtype))
        return carry

    lax.fori_loop(0, rows_per_blk // NA_ROWS_PER_ITER, rows_body, 0)


def _na_bias_table(rpb):
    heads = rpb.shape[0]
    qc = jnp.arange(GRID_W)
    kc = jnp.arange(GRID_W)
    qc0 = jnp.clip(qc - NA_WIN_C // 2, 0, GRID_W - NA_WIN_C)
    col_in = (kc[None, :] >= qc0[:, None]) & (kc[None, :] < qc0[:, None] + NA_WIN_C)
    pad = GRID_W - NA_WIN_C
    rp = jnp.pad(rpb.astype(F32), ((0, 0), (0, 0), (pad, pad)))
    toep = jnp.stack([rp[:, :, GRID_W - 1 - q:2 * GRID_W - 1 - q] for q in range(GRID_W)], axis=2)
    toep = jnp.where(col_in[None, None], toep, NEG_BIG)
    tab = jnp.stack([toep[:, c:c + NA_WIN_R] for c in range(NA_WIN_R)], axis=0)
    tab = tab.transpose(0, 1, 3, 2, 4) * LOG2E
    return tab.reshape(NA_WIN_R, heads // 2, 2 * GRID_W, NA_WIN_R * GRID_W)


def _neighborhood_attention(q, k, v, kc, vc, bias):
    bsz, s, w = q.shape
    n_rows = s // GRID_W
    rpb_rows = NA_WIN_R
    blk = rpb_rows * GRID_W
    nb = s // blk
    lc = kc.shape[1]
    cur = pl.BlockSpec((None, blk, w), lambda b, i: (b, i, 0))
    prev = pl.BlockSpec((None, blk, w), lambda b, i: (b, jnp.maximum(i - 1, 0), 0))
    nxt = pl.BlockSpec((None, blk, w), lambda b, i: (b, jnp.minimum(i + 1, nb - 1), 0))
    ctx = pl.BlockSpec((None, lc, w), lambda b, i: (b, 0, 0))
    return pl.pallas_call(
        functools.partial(_na_kernel, rows_per_blk=rpb_rows, n_rows=n_rows),
        grid=(bsz, nb),
        in_specs=[cur, prev, cur, nxt, prev, cur, nxt, ctx, ctx, _const_spec(bias.shape)],
        out_specs=cur,
        out_shape=jax.ShapeDtypeStruct(q.shape, BF16),
        scratch_shapes=[pltpu.VMEM((3 * blk, w), BF16), pltpu.VMEM((3 * blk, 2 * w), BF16),
                        pltpu.VMEM((lc, 2 * w), BF16)],
        compiler_params=_cp("parallel", "parallel"),
        name="neighborhood_attention",
    )(q, k, k, k, v, v, v, kc, vc, bias)


def _ctx_attn_kernel(q_ref, k_ref, v_ref, o_ref):
    n_pairs = q_ref.shape[-1] // LANES
    outs = []
    for hp in range(n_pairs):
        sl = slice(hp * LANES, (hp + 1) * LANES)
        qp, kp, vp = q_ref[:, sl], k_ref[:, sl], v_ref[:, sl]
        o_pair = None
        for t, mask in enumerate(_half_masks(qp.shape)):
            qm = jnp.where(mask, qp, jnp.zeros_like(qp))
            vm = jnp.where(mask, vp, jnp.zeros_like(vp))
            o_t = _softmax_pv([_dot_nt(qm, kp)], [vm])
            o_pair = o_t if o_pair is None else o_pair + o_t
        outs.append(o_pair)
    o_ref[...] = jnp.concatenate(outs, axis=1).astype(o_ref.dtype)


def _ctx_attention(q, k, v):
    bsz, lc, w = q.shape
    spec = pl.BlockSpec((None, lc, w), lambda b: (b, 0, 0))
    return pl.pallas_call(
        _ctx_attn_kernel, grid=(bsz,), in_specs=[spec, spec, spec], out_specs=spec,
        out_shape=jax.ShapeDtypeStruct(q.shape, BF16),
        compiler_params=_cp("parallel"), name="ctx_attention",
    )(q, k, v)


def _conv_kernel(prev_ref, x_ref, next_ref, w_ref, b_ref, o_ref, *, act):
    i = pl.program_id(1)
    n = pl.num_programs(1)
    t = x_ref.shape[0]
    halo = prev_ref.shape[0]
    prev = jnp.where(i > 0, prev_ref[...], 0.0)
    nxt = jnp.where(i < n - 1, next_ref[...], 0.0)
    xx = jnp.concatenate([prev, x_ref[...], nxt], axis=0)
    taps = w_ref.shape[0]
    left = taps // 2
    tot = t + 2 * halo
    y = b_ref[...]
    for j in range(taps):
        sh = (left - j) % tot
        xs = xx if sh == 0 else pltpu.roll(xx, sh, axis=0)
        y = y + w_ref[j:j + 1, :] * xs[halo:halo + t]
    o_ref[...] = _silu(y) if act else y


def _dwconv(x, w, b, act):
    bsz, length, c = x.shape
    t = min(512, length)
    halo = 8
    per = t // halo
    nh = length // halo
    return pl.pallas_call(
        functools.partial(_conv_kernel, act=act),
        grid=(bsz, length // t),
        in_specs=[pl.BlockSpec((None, halo, c), lambda bb, i: (bb, jnp.maximum(i * per - 1, 0), 0)),
                  pl.BlockSpec((None, t, c), lambda bb, i: (bb, i, 0)),
                  pl.BlockSpec((None, halo, c), lambda bb, i: (bb, jnp.minimum((i + 1) * per, nh - 1), 0)),
                  _const_spec(w.shape), _const_spec((1, c))],
        out_specs=pl.BlockSpec((None, t, c), lambda bb, i: (bb, i, 0)),
        out_shape=jax.ShapeDtypeStruct(x.shape, F32),
        compiler_params=_cp("parallel", "parallel"),
        name="dwconv",
    )(x, x, x, w.astype(F32), b.astype(F32).reshape(1, c))


def _ssd_direction(xbc_ref, dtc_ref, dtr_ref, pc_ref, pr_ref, dl_ref, state, y_ref, *, d, reverse, add_skip):
    q = xbc_ref.shape[0]
    hp_w = LANES
    n_heads = SSD_HEADS
    gw = n_heads * HEAD_DIM
    ii = lax.broadcasted_iota(jnp.int32, (q, q), 0)
    jj = lax.broadcasted_iota(jnp.int32, (q, q), 1)
    keep = (ii <= jj) if reverse else (ii >= jj)
    tri = keep.astype(F32)
    dt_c = _softplus(dtc_ref[:, d * n_heads:(d + 1) * n_heads] + pc_ref[0:1, :])
    dt_r = _softplus(dtr_ref[d * n_heads:(d + 1) * n_heads, :] + pr_ref[:, 0:1])
    da_c = dt_c * pc_ref[1:2, :]
    da_r = dt_r * pr_ref[:, 1:2]
    acs_c = jnp.dot(tri, da_c, preferred_element_type=F32, precision=HIGHEST)
    acs_r = lax.dot_general(da_r, tri, (((1,), (1,)), ((), ())), preferred_element_type=F32,
                            precision=HIGHEST)
    edge = 0 if reverse else q - 1
    tot_r = acs_r[:, edge:edge + 1]
    w_end_r = jnp.exp(tot_r - acs_r) * dt_r
    e_tot_r = jnp.exp(tot_r)
    lane_lo, lane_hi = _half_masks((q, hp_w))
    for g in range(2):
        bm = xbc_ref[:, gw + g * SSD_STATE:gw + (g + 1) * SSD_STATE]
        cm = xbc_ref[:, gw + 2 * SSD_STATE + g * SSD_STATE:gw + 2 * SSD_STATE + (g + 1) * SSD_STATE]
        bm_t = bm.T
        cm16 = cm.astype(BF16)
        cb = _dot(cm16, bm_t.astype(BF16))
        for pp in range(2):
            pair = g * 2 + pp
            h0 = 2 * pair
            sl = slice(pair * hp_w, (pair + 1) * hp_w)
            xs = xbc_ref[:, sl]
            y_pair = None
            st_new = None
            e_in = []
            for t, lmask in enumerate((lane_lo, lane_hi)):
                h = h0 + t
                a_bc = jnp.broadcast_to(acs_c[:, h:h + 1], (q, q))
                seg = a_bc - acs_r[h:h + 1, :]
                dec = jnp.where(keep, jnp.exp(jnp.where(keep, seg, 0.0)), 0.0) * dt_r[h:h + 1, :]
                xm = jnp.where(lmask, xs, 0.0).astype(BF16)
                yd = _dot((cb * dec).astype(BF16), xm)
                sn = _dot((bm_t * w_end_r[h:h + 1, :]).astype(BF16), xm)
                y_pair = yd if y_pair is None else y_pair + yd
                st_new = sn if st_new is None else st_new + sn
                e_in.append(jnp.exp(a_bc))
            st_old = state[:, sl]
            y_pair = y_pair + _dot(cm16, st_old.astype(BF16)) * jnp.where(lane_lo, e_in[0], e_in[1])
            if add_skip:
                y_pair = y_pair + dl_ref[:, sl] * xs
            y_ref[:, sl] = y_pair
            e_tot = jnp.where(lane_lo[0:1], e_tot_r[h0:h0 + 1, :], e_tot_r[h0 + 1:h0 + 2, :])
            state[:, sl] = st_old * e_tot + st_new


def _ssd_kernel(xf_ref, xb_ref, dtcf_ref, dtcb_ref, dtrf_ref, dtrb_ref, pc_ref, pr_ref, dl_ref, h0f_ref, h0b_ref,
                yf_ref, yb_ref, hf_ref, hb_ref, sf, sb):
    c = pl.program_id(1)

    @pl.when(c == 0)
    def _():
        sf[...] = h0f_ref[...]
        sb[...] = h0b_ref[...]

    _ssd_direction(xf_ref, dtcf_ref, dtrf_ref, pc_ref.at[0], pr_ref.at[0], dl_ref, sf, yf_ref,
                   d=0, reverse=False, add_skip=True)
    _ssd_direction(xb_ref, dtcb_ref, dtrb_ref, pc_ref.at[1], pr_ref.at[1], dl_ref, sb, yb_ref,
                   d=1, reverse=True, add_skip=False)
    hf_ref[...] = sf[...]
    hb_ref[...] = sb[...]


def _ssd(xbc, dt, pc, pr, dl, h0f, h0b):
    bsz, length, cw = xbc.shape
    q = SSD_CHUNK
    nc = length // q
    gw = SSD_HEADS * HEAD_DIM
    dtw = dt.shape[-1]
    dt_t = jnp.swapaxes(dt, 1, 2)
    fwd3 = lambda b, c: (b, c, 0)
    bwd3 = lambda b, c: (b, nc - 1 - c, 0)
    st_spec = pl.BlockSpec((None, SSD_STATE, gw), lambda b, c: (b, 0, 0))
    return pl.pallas_call(
        _ssd_kernel,
        grid=(bsz, nc),
        in_specs=[pl.BlockSpec((None, q, cw), fwd3), pl.BlockSpec((None, q, cw), bwd3),
                  pl.BlockSpec((None, q, dtw), fwd3), pl.BlockSpec((None, q, dtw), bwd3),
                  pl.BlockSpec((None, dtw, q), lambda b, c: (b, 0, c)),
                  pl.BlockSpec((None, dtw, q), lambda b, c: (b, 0, nc - 1 - c)),
                  _const_spec(pc.shape), _const_spec(pr.shape), _const_spec(dl.shape), st_spec, st_spec],
        out_specs=[pl.BlockSpec((None, q, gw), fwd3), pl.BlockSpec((None, q, gw), bwd3), st_spec, st_spec],
        out_shape=[jax.ShapeDtypeStruct((bsz, length, gw), F32)] * 2
                  + [jax.ShapeDtypeStruct((bsz, SSD_STATE, gw), F32)] * 2,
        scratch_shapes=[pltpu.VMEM((SSD_STATE, gw), F32)] * 2,
        compiler_params=_cp("parallel", "arbitrary"),
        name="ssd_scan",
    )(xbc, xbc, dt, dt, dt_t, dt_t, pc, pr, dl, h0f, h0b)


def _gelu_tanh(x):
    return 0.5 * x * (1.0 + jnp.tanh(math.sqrt(2.0 / math.pi) * (x + 0.044715 * (x * x * x))))


def _mix_ab(ona_ref, yf_ref, yb_ref, z_ref, ng_ref, w_ref):
    gw = ona_ref.shape[-1]
    y = (yf_ref[...] + yb_ref[...]) * _silu(z_ref[...])
    gated = y * lax.rsqrt(jnp.mean(y * y, axis=-1, keepdims=True) + EPS) * ng_ref[...]
    return _dot(ona_ref[...], w_ref[0:gw, :]) + _dot(gated.astype(BF16), w_ref[gw:2 * gw, :])


def _mix_cd(gate_ref, hf_ref, hb_ref, od_ref, w_ref):
    gw = od_ref.shape[-1]
    lru = _gelu_tanh(gate_ref[...]) * (hf_ref[...] + hb_ref[...])
    return _dot(lru.astype(BF16), w_ref[0:gw, :]) + _dot(od_ref[...], w_ref[gw:2 * gw, :])


def _mix_ffn_kernel(x_ref, mod_ref, *rest, mix, n_mix, k, chunk):
    g_ref, w1_ref, w3_ref, w2_ref, o_ref = rest[n_mix:]
    x = x_ref[...] + mod_ref[5:6, :] * mix(*rest[:n_mix])
    o_ref[...] = _ffn_body(x, mod_ref, g_ref, w1_ref, w3_ref, w2_ref, k, chunk)


def _mixer_out_ffn(mix, name, x, mod, parts, consts, g, w1, w3, w2, k):
    bsz, length, d = x.shape
    ff = w1.shape[1]
    tm = min(512, length)
    tok = lambda a: pl.BlockSpec((None, tm, a.shape[-1]), lambda b, i: (b, i, 0))
    return pl.pallas_call(
        functools.partial(_mix_ffn_kernel, mix=mix, n_mix=len(parts) + len(consts), k=k, chunk=FFN_CHUNK),
        grid=(bsz, length // tm),
        in_specs=[tok(x), _mod_spec(mod)] + [tok(a) for a in parts] + [_const_spec(a.shape) for a in consts]
                 + [_const_spec((1, d)), _const_spec((d, ff)), _const_spec((d, ff)), _const_spec((ff, d))],
        out_specs=tok(x),
        out_shape=jax.ShapeDtypeStruct(x.shape, F32),
        compiler_params=pltpu.CompilerParams(dimension_semantics=("parallel", "parallel"),
                                             vmem_limit_bytes=DIFF_VMEM_LIMIT),
        name=name,
    )(x, mod, *parts, *consts, g.reshape(1, d), w1, w3, w2)


def _lru_direction(x_ref, wa_ref, wx_ref, p_ref, carry, h_ref, *, reverse):
    t = x_ref.shape[0]
    x = x_ref[...]
    x16 = x.astype(BF16)
    r = _sigmoid(_dot(x16, wa_ref[...]) + p_ref[0:1, :])
    ig = _sigmoid(_dot(x16, wx_ref[...]) + p_ref[1:2, :])
    log_a = -LRU_C * r * _softplus(-p_ref[2:3, :])
    a = jnp.exp(log_a)
    b = jnp.sqrt(1.0 - a * a) * (ig * x)
    n_groups = t // SUBLANES
    a = a.reshape(n_groups, SUBLANES, a.shape[-1])
    b = b.reshape(n_groups, SUBLANES, b.shape[-1])
    row = lax.broadcasted_iota(jnp.int32, a.shape, 1)
    s = 1
    while s < SUBLANES:
        fill = (row >= SUBLANES - s) if reverse else (row < s)
        shift = SUBLANES - s if reverse else s
        a_sh = jnp.where(fill, 1.0, pltpu.roll(a, shift, axis=1))
        b_sh = jnp.where(fill, 0.0, pltpu.roll(b, shift, axis=1))
        b = a * b_sh + b
        a = a * a_sh
        s *= 2
    h_prev = carry[...]
    for gi in (range(n_groups - 1, -1, -1) if reverse else range(n_groups)):
        r0 = gi * SUBLANES
        hg = a[gi] * h_prev + b[gi]
        h_ref[r0:r0 + SUBLANES, :] = hg
        h_prev = hg[0:1] if reverse else hg[SUBLANES - 1:SUBLANES]
    carry[...] = h_prev


def _lru_kernel(xf_ref, xb_ref, wa_ref, wx_ref, p_ref, h0f_ref, h0b_ref, hf_ref, hb_ref, lf_ref, lb_ref, cf, cb):
    c = pl.program_id(1)

    @pl.when(c == 0)
    def _():
        cf[...] = h0f_ref[...]
        cb[...] = h0b_ref[...]

    _lru_direction(xf_ref, wa_ref.at[0], wx_ref.at[0], p_ref.at[0], cf, hf_ref, reverse=False)
    _lru_direction(xb_ref, wa_ref.at[1], wx_ref.at[1], p_ref.at[1], cb, hb_ref, reverse=True)
    lf_ref[...] = cf[...]
    lb_ref[...] = cb[...]


def _lru(x, wa, wx, p, h0f, h0b):
    bsz, length, w = x.shape
    t = min(512, length)
    nt = length // t
    fwd = lambda b, c: (b, c, 0)
    bwd = lambda b, c: (b, nt - 1 - c, 0)
    st = pl.BlockSpec((None, 1, w), lambda b, c: (b, 0, 0))
    return pl.pallas_call(
        _lru_kernel,
        grid=(bsz, nt),
        in_specs=[pl.BlockSpec((None, t, w), fwd), pl.BlockSpec((None, t, w), bwd),
                  _const_spec(wa.shape), _const_spec(wx.shape), _const_spec(p.shape), st, st],
        out_specs=[pl.BlockSpec((None, t, w), fwd), pl.BlockSpec((None, t, w), bwd), st, st],
        out_shape=[jax.ShapeDtypeStruct(x.shape, F32)] * 2 + [jax.ShapeDtypeStruct((bsz, 1, w), F32)] * 2,
        scratch_shapes=[pltpu.VMEM((1, w), F32)] * 2,
        compiler_params=_cp("parallel", "arbitrary"),
        name="rglru_scan",
    )(x, x, wa, wx, p, h0f, h0b)


def _block_diag(wb):
    nb, bs, _ = wb.shape
    eye = jnp.eye(nb, dtype=wb.dtype)
    return (wb[:, :, None, :] * eye[:, None, :, None]).reshape(nb * bs, nb * bs)


def _diff_attn_kernel(q_ref, kt_ref, v_ref, lam_ref, sg_ref, o_ref, qs, m_s, acc_s, s_buf, *, lam_init, n_heads, tk):
    tq = q_ref.shape[0]
    vw = 2 * LANES
    for h in range(n_heads):
        qp = q_ref[:, h * LANES:(h + 1) * LANES]
        lo, hi = _half_masks(qp.shape)
        qs[h, 0:tq, :] = jnp.where(lo, qp, jnp.zeros_like(qp))
        qs[h, tq:2 * tq, :] = jnp.where(hi, qp, jnp.zeros_like(qp))
    m_s[...] = jnp.full(m_s.shape, NEG_BIG, F32)
    acc_s[...] = jnp.zeros(acc_s.shape, F32)

    n_chunks = kt_ref.shape[1] // tk

    def scores(h, k0):
        return _dot(qs[h], kt_ref[h * LANES:(h + 1) * LANES, pl.ds(k0, tk)])

    s_buf[0] = scores(0, 0)

    def chunk(c, carry):
        k0 = pl.multiple_of(c * tk, tk)
        k_next = pl.multiple_of(jnp.minimum(c + 1, n_chunks - 1) * tk, tk)
        for h in range(n_heads):
            if h + 1 < n_heads:
                s_buf[(h + 1) % 2] = scores(h + 1, k0)
            else:
                s_buf[0] = scores(0, k_next)
            s = s_buf[h % 2]
            m_old = m_s[h]
            m_new = jnp.maximum(m_old, s.max(axis=-1, keepdims=True))
            alpha = jnp.exp2(m_old - m_new)
            p = jnp.concatenate([jnp.exp2(s[:, j * LANES:(j + 1) * LANES] - m_new).astype(BF16)
                                 for j in range(tk // LANES)], axis=1)
            pv = _dot(p, v_ref[pl.ds(k0, tk), h * vw:(h + 1) * vw])
            acc_s[h] = jnp.concatenate([alpha, alpha], axis=1) * acc_s[h] + pv
            m_s[h] = m_new
        return carry

    lax.fori_loop(0, n_chunks, chunk, 0)

    dl = lam_ref[...]
    lam = (jnp.exp(jnp.sum(dl[0:1] * dl[1:2], axis=-1, keepdims=True))
           - jnp.exp(jnp.sum(dl[2:3] * dl[3:4], axis=-1, keepdims=True)) + lam_init)
    for h in range(n_heads):
        a = acc_s[h]
        o = a[0:tq, 0:LANES] / a[0:tq, LANES:vw] - lam * (a[tq:2 * tq, 0:LANES] / a[tq:2 * tq, LANES:vw])
        o = o * lax.rsqrt(jnp.mean(o * o, axis=-1, keepdims=True) + EPS) * sg_ref[...] * (1.0 - lam_init)
        o_ref[:, h * LANES:(h + 1) * LANES] = o.astype(o_ref.dtype)


def _diff_attention(q, kt_all, v_ext, diff_lam, subln_g, lam_init):
    bsz, s, w = q.shape
    lk = kt_all.shape[2]
    n_heads = w // LANES
    tq = min(DIFF_TQ, s)
    tk = next(c for c in (768, 512, 256, 128) if lk % c == 0)
    return pl.pallas_call(
        functools.partial(_diff_attn_kernel, lam_init=lam_init, n_heads=n_heads, tk=tk),
        grid=(bsz, s // tq),
        in_specs=[pl.BlockSpec((None, tq, w), lambda b, i: (b, i, 0)),
                  pl.BlockSpec((None, w, lk), lambda b, i: (b, 0, 0), pipeline_mode=pl.Buffered(1)),
                  pl.BlockSpec((None, lk, v_ext.shape[2]), lambda b, i: (b, 0, 0), pipeline_mode=pl.Buffered(1)),
                  _const_spec(diff_lam.shape), _const_spec((1, LANES))],
        out_specs=pl.BlockSpec((None, tq, w), lambda b, i: (b, i, 0)),
        out_shape=jax.ShapeDtypeStruct(q.shape, BF16),
        scratch_shapes=[pltpu.VMEM((n_heads, 2 * tq, LANES), BF16), pltpu.VMEM((n_heads, 2 * tq, LANES), F32),
                        pltpu.VMEM((n_heads, 2 * tq, 2 * LANES), F32), pltpu.VMEM((2, 2 * tq, tk), F32)],
        compiler_params=pltpu.CompilerParams(dimension_semantics=("parallel", "parallel"),
                                             vmem_limit_bytes=DIFF_VMEM_LIMIT),
        name="diff_attention",
    )(q, kt_all, v_ext, diff_lam.astype(F32), subln_g.astype(F32).reshape(1, LANES))


def _pad_cols(w, total):
    return jnp.pad(w, ((0, 0), (0, total - w.shape[1])))


def _layer_ab(x, xc, m, mc, g_mix, w_in, w_out, q_g, k_g, rpb, conv_w, conv_b, dt_bias, a_log, d_skip, norm_g,
              ffn2):
    gw = SSD_HEADS * HEAD_DIM
    segs = [_Seg(gw, 'norm', gain=q_g, scale=HEAD_DIM ** -0.5 * LOG2E), _Seg(gw, 'norm', gain=k_g),
            _Seg(gw, dtype=BF16), _Seg(gw), _Seg(2 * gw), _Seg(LANES)]
    w16 = _pad_cols(w_in, sum(s.width for s in segs)).astype(BF16)
    pmat = _group_mean_matrix(gw, HEAD_DIM)
    q, k, v, z, xbc, dt = _norm_proj(x, m, g_mix, w16, 1, segs, pmat)
    q_c, k_c, v_c, z_c, xbc_c, dt_c = _norm_proj(xc, mc, g_mix, w16, 1, segs, pmat)
    o_na = _neighborhood_attention(q, k, v, k_c, v_c, _na_bias_table(rpb))
    o_c = _ctx_attention(q_c, k_c, v_c)

    a_neg = -jnp.exp(a_log.astype(F32))
    pc = jnp.stack([dt_bias.astype(F32), a_neg], axis=1)
    pr = jnp.swapaxes(pc, 1, 2)
    dl = jnp.repeat(d_skip.astype(F32), HEAD_DIM).reshape(1, gw)
    cw = conv_w.astype(F32)
    xbc_c = _dwconv(xbc_c, cw, conv_b, act=True)
    xbc = _dwconv(xbc, cw, conv_b, act=True)
    zeros = jnp.zeros((x.shape[0], SSD_STATE, gw), F32)
    yf_c, yb_c, hf_c, hb_c = _ssd(xbc_c, dt_c, pc, pr, dl, zeros, zeros)
    yf, yb, _, _ = _ssd(xbc, dt, pc, pr, dl, hf_c, hb_c)

    ng = norm_g.astype(F32).reshape(1, gw)
    wo16 = w_out.astype(BF16)
    x = _mixer_out_ffn(_mix_ab, "mixer_out_ab_ffn", x, m, [o_na, yf, yb, z], [ng, wo16], *ffn2, 2)
    xc = _mixer_out_ffn(_mix_ab, "mixer_out_ab_ffn", xc, mc, [o_c, yf_c, yb_c, z_c], [ng, wo16], *ffn2, 2)
    return x, xc


def _layer_cd(x, xc, m, mc, g_mix, w_in, w_out, conv_w, conv_b, wa, ba, wx, bx, lam_p, q_g, k_g, diff_lam,
              subln_g, lam_init, ffn2):
    gw = w_out.shape[0] // 2
    w16 = w_in.astype(BF16)
    pmat = _group_mean_matrix(gw, HEAD_DIM)
    tabs = _rope_tables(x.shape[1])
    q_scale = HEAD_DIM ** -0.5 * LOG2E

    def segs(rope):
        return [_Seg(gw), _Seg(gw), _Seg(gw, 'norm', gain=q_g, scale=q_scale, rope=rope),
                _Seg(gw, 'norm_t', gain=k_g, rope=rope), _Seg(gw, 'vext')]

    gate, xr, q, kt, v_ext = _norm_proj(x, m, g_mix, w16, 1, segs(True), pmat, tabs)
    _, xr_c, _, kt_c, v_ext_c = _norm_proj(xc, mc, g_mix, w16, 1, segs(False), pmat)

    cw = conv_w.astype(F32)
    xr_c = _dwconv(xr_c, cw, conv_b, act=False)
    xr = _dwconv(xr, cw, conv_b, act=False)
    wa_d = jnp.stack([_block_diag(wa[0]), _block_diag(wa[1])]).astype(BF16)
    wx_d = jnp.stack([_block_diag(wx[0]), _block_diag(wx[1])]).astype(BF16)
    p = jnp.stack([ba.astype(F32), bx.astype(F32), lam_p.astype(F32)], axis=1)
    zeros = jnp.zeros((x.shape[0], 1, gw), F32)
    _, _, lf_c, lb_c = _lru(xr_c, wa_d, wx_d, p, zeros, zeros)
    hf, hb, _, _ = _lru(xr, wa_d, wx_d, p, lf_c, lb_c)

    kt_all = jnp.concatenate([kt_c, kt], axis=2)
    v_all = jnp.concatenate([v_ext_c, v_ext], axis=1)
    o = _diff_attention(q, kt_all, v_all, diff_lam, subln_g, lam_init)
    return _mixer_out_ffn(_mix_cd, "mixer_out_cd_ffn", x, m, [gate, hf, hb, o], [w_out.astype(BF16)], *ffn2, 2)


def kernel(x, c, ctx, c_ctx, w_mod, b_mod, norm_g, ffn_w1, ffn_w3, ffn_w2, ab_w_in, ab_w_out, na_q_g, na_k_g, na_rpb, ssd_conv_w, ssd_conv_b, ssd_dt_bias, ssd_a_log, ssd_d, ssd_norm_g, cd_w_in, cd_w_out, lru_conv_w, lru_conv_b, lru_wa, lru_ba, lru_wx, lru_bx, lru_lambda, diff_q_g, diff_k_g, diff_lambda, diff_subln_g):
    bsz, _, d = x.shape
    depth = w_mod.shape[0]
    cc = jnp.concatenate([c.astype(F32), c_ctx.astype(F32)[None], jnp.zeros((8 - bsz - 1, d), F32)], axis=0)
    mods = _modulation(cc, w_mod.astype(F32), b_mod.astype(F32))
    xc = ctx
    for i in range(depth):
        last = i == depth - 1
        j = i // 2
        m = mods[i, :bsz].reshape(bsz, N_MOD, d)
        mc = mods[i, bsz:bsz + 1].reshape(1, N_MOD, d)
        g = norm_g[i].astype(F32)
        w1 = ffn_w1[i].astype(BF16)
        w3 = ffn_w3[i].astype(BF16)
        w2 = ffn_w2[i].astype(BF16)
        x = _ffn(x, m, g[0], w1[0], w3[0], w2[0], 0)
        xc = _ffn(xc, mc, g[0], w1[0], w3[0], w2[0], 0)
        ffn2 = (g[2], w1[1], w3[1], w2[1])
        if i % 2 == 0:
            x, xc = _layer_ab(x, xc, m, mc, g[1], ab_w_in[j], ab_w_out[j], na_q_g[j], na_k_g[j], na_rpb[j],
                              ssd_conv_w[j], ssd_conv_b[j], ssd_dt_bias[j], ssd_a_log[j], ssd_d[j], ssd_norm_g[j],
                              ffn2)
        else:
            assert last, "a C|D layer that is not the last one would also need the context stream's mixer output"
            lam_init = 0.8 - 0.6 * math.exp(-0.3 * i)
            x = _layer_cd(x, xc, m, mc, g[1], cd_w_in[j], cd_w_out[j], lru_conv_w[j], lru_conv_b[j], lru_wa[j],
                          lru_ba[j], lru_wx[j], lru_bx[j], lru_lambda[j], diff_q_g[j], diff_k_g[j],
                          diff_lambda[j], diff_subln_g[j], lam_init, ffn2)
    return x
```

```python
import functools
import math

import jax
import jax.numpy as jnp
from jax import lax
from jax.experimental import pallas as pl
from jax.experimental.pallas import tpu as pltpu

F32 = jnp.float32
BF16 = jnp.bfloat16
HIGHEST = lax.Precision.HIGHEST

GRID_W = 64
EPS = 1e-6
HEAD_DIM = 64
N_MOD = 9
NA_WIN_R = 8
NA_WIN_C = 16
SSD_STATE = 128
SSD_CHUNK = 128
SSD_HEADS = 8
LRU_C = 8.0
ROPE_BASE = 10000.0
NEG_BIG = -1e30

LANES = 128
SUBLANES = 8
VMEM_LIMIT = 48 * 1024 * 1024
DIFF_VMEM_LIMIT = 56 * 1024 * 1024
DIFF_TQ = 512
FFN_ROW_BLOCKS = 2
NA_ROWS_PER_ITER = 4
SSD_CHUNKS_PER_STEP = 4
MXU_TILE = 256
FFN_CHUNK = 6 * MXU_TILE
LOG2E = math.log2(math.e)


def _cp(*sem):
    return pltpu.CompilerParams(dimension_semantics=sem, vmem_limit_bytes=VMEM_LIMIT)


def _const_spec(shape):
    nd = len(shape)
    return pl.BlockSpec(shape, lambda *_: (0,) * nd, pipeline_mode=pl.Buffered(1))


def _dot(a, b):
    return jnp.dot(a, b, preferred_element_type=F32)


def _dot_nt(a, b):
    return lax.dot_general(a, b, (((1,), (1,)), ((), ())), preferred_element_type=F32)


def _sigmoid(x):
    return 0.5 * jnp.tanh(0.5 * x) + 0.5


def _silu(x):
    return x * _sigmoid(x)


def _softplus(x):
    return jnp.maximum(x, 0.0) + jnp.log(1.0 + jnp.exp(-jnp.abs(x)))


def _rms_mod(x, g, shift, scale):
    ms = jnp.mean(x * x, axis=-1, keepdims=True)
    return (x * lax.rsqrt(ms + EPS) * g) * (1.0 + scale) + shift


def _mod_kernel(c_ref, w_ref, b_ref, o_ref):
    s = _silu(c_ref[...])
    o_ref[...] = jnp.dot(s, w_ref[...], preferred_element_type=F32, precision=HIGHEST) + b_ref[...]


def _modulation(cc, w_mod, b_mod):
    depth, d, n = w_mod.shape
    tn = 1024
    return pl.pallas_call(
        _mod_kernel,
        grid=(depth, n // tn),
        in_specs=[pl.BlockSpec((8, d), lambda l, j: (0, 0)),
                  pl.BlockSpec((None, d, tn), lambda l, j: (l, 0, j)),
                  pl.BlockSpec((None, 1, tn), lambda l, j: (l, 0, j))],
        out_specs=pl.BlockSpec((None, 8, tn), lambda l, j: (l, 0, j)),
        out_shape=jax.ShapeDtypeStruct((depth, 8, n), F32),
        compiler_params=_cp("parallel", "parallel"),
        name="modulation",
    )(cc, w_mod, b_mod.reshape(depth, 1, n))


def _mod_spec(mod):
    if mod.shape[0] == 1:
        return pl.BlockSpec((None, N_MOD, mod.shape[2]), lambda b, i: (0, 0, 0))
    return pl.BlockSpec((None, N_MOD, mod.shape[2]), lambda b, i: (b, 0, 0))


def _ffn_body(x, mod_ref, g_ref, w1_ref, w3_ref, w2_ref, k, chunk):
    tm = x.shape[0]
    rb = tm // FFN_ROW_BLOCKS if tm % (FFN_ROW_BLOCKS * SUBLANES) == 0 else tm
    ff = w1_ref.shape[1]
    outs = []
    for r0 in range(0, tm, rb):
        xr = x[r0:r0 + rb]
        h = _rms_mod(xr, g_ref[...], mod_ref[3 * k:3 * k + 1, :], mod_ref[3 * k + 1:3 * k + 2, :]).astype(BF16)
        acc = jnp.zeros(xr.shape, F32)
        for c0 in range(0, ff, chunk):
            c1 = min(c0 + chunk, ff)
            a = _dot(h, w1_ref[:, c0:c1])
            b = _dot(h, w3_ref[:, c0:c1])
            acc = acc + _dot((_silu(a) * b).astype(BF16), w2_ref[c0:c1, :])
        outs.append(xr + (0.5 * mod_ref[3 * k + 2:3 * k + 3, :]) * acc)
    return outs[0] if len(outs) == 1 else jnp.concatenate(outs, axis=0)


def _ffn_kernel(x_ref, mod_ref, g_ref, w1_ref, w3_ref, w2_ref, o_ref, *, k, chunk):
    o_ref[...] = _ffn_body(x_ref[...], mod_ref, g_ref, w1_ref, w3_ref, w2_ref, k, chunk)


def _ffn(x, mod, g, w1, w3, w2, k):
    bsz, length, d = x.shape
    ff = w1.shape[1]
    tm = min(512, length)
    return pl.pallas_call(
        functools.partial(_ffn_kernel, k=k, chunk=FFN_CHUNK),
        grid=(bsz, length // tm),
        in_specs=[pl.BlockSpec((None, tm, d), lambda b, i: (b, i, 0)),
                  _mod_spec(mod),
                  _const_spec((1, d)),
                  _const_spec((d, ff)), _const_spec((d, ff)), _const_spec((ff, d))],
        out_specs=pl.BlockSpec((None, tm, d), lambda b, i: (b, i, 0)),
        out_shape=jax.ShapeDtypeStruct(x.shape, F32),
        compiler_params=_cp("parallel", "parallel"),
        name="half_ffn",
    )(x, mod, g.reshape(1, d), w1, w3, w2)


class _Seg:
    def __init__(self, width, kind='plain', dtype=F32, gain=None, scale=1.0, rope=False):
        self.width, self.kind, self.dtype, self.gain, self.scale, self.rope = width, kind, dtype, gain, scale, rope


def _head_norm(y, gain, pmat, scale, rope_tabs):
    ms = _dot((y * y).astype(BF16), pmat)
    y = y * lax.rsqrt(ms + EPS) * gain
    if rope_tabs is not None:
        reps = y.shape[-1] // rope_tabs[0].shape[-1]
        cs = jnp.concatenate([rope_tabs[0]] * reps, axis=1)
        sn = jnp.concatenate([rope_tabs[1]] * reps, axis=1)
        y = y * cs + _swap_pairs(y) * sn
    return y * scale


def _norm_proj_kernel(x_ref, mod_ref, g_ref, w_ref, *rest, k, segs, n_gain, has_rope):
    pos = 0
    pmat = None
    if n_gain:
        pmat = rest[0][...]
        pos = 1
    gains = rest[pos:pos + n_gain]
    pos += n_gain
    rope_tabs = None
    if has_rope:
        rope_tabs = (rest[pos][...], rest[pos + 1][...])
        pos += 2
    o_refs = rest[pos:]
    h = _rms_mod(x_ref[...], g_ref[...], mod_ref[3 * k:3 * k + 1, :], mod_ref[3 * k + 1:3 * k + 2, :]).astype(BF16)
    off, gi = 0, 0
    for seg, o_ref in zip(segs, o_refs):
        y = _dot(h, w_ref[:, off:off + seg.width])
        off += seg.width
        if seg.kind in ('norm', 'norm_t'):
            y = _head_norm(y, gains[gi][...], pmat, seg.scale, rope_tabs if seg.rope else None)
            gi += 1
            o_ref[...] = (y.T if seg.kind == 'norm_t' else y).astype(o_ref.dtype)
        elif seg.kind == 'vext':
            y16 = y.astype(BF16)
            ones = jnp.ones((y.shape[0], LANES), BF16)
            for hp in range(seg.width // LANES):
                o_ref[:, 2 * hp * LANES:(2 * hp + 1) * LANES] = y16[:, hp * LANES:(hp + 1) * LANES]
                o_ref[:, (2 * hp + 1) * LANES:(2 * hp + 2) * LANES] = ones
        else:
            o_ref[...] = y.astype(o_ref.dtype)


def _norm_proj(x, mod, g, w, k, segs, pmat=None, rope_tabs=None):
    bsz, length, d = x.shape
    tm = min(512, length)
    tok = lambda wd: pl.BlockSpec((None, tm, wd), lambda b, i: (b, i, 0))
    in_specs = [tok(d), _mod_spec(mod), _const_spec((1, d)), _const_spec(w.shape)]
    args = [x, mod, g.reshape(1, d), w]
    gains = [s for s in segs if s.kind in ('norm', 'norm_t')]
    if gains:
        in_specs.append(_const_spec(pmat.shape))
        args.append(pmat)
        for s in gains:
            in_specs.append(_const_spec((1, s.width)))
            args.append(jnp.tile(s.gain.astype(F32), s.width // s.gain.shape[0]).reshape(1, s.width))
    has_rope = any(s.rope for s in segs)
    if has_rope:
        in_specs += [pl.BlockSpec((tm, rope_tabs[0].shape[-1]), lambda b, i: (i, 0))] * 2
        args += list(rope_tabs)
    out_specs, out_shape = [], []
    for s in segs:
        if s.kind == 'norm_t':
            out_specs.append(pl.BlockSpec((None, s.width, tm), lambda b, i: (b, 0, i)))
            out_shape.append(jax.ShapeDtypeStruct((bsz, s.width, length), BF16))
        elif s.kind == 'vext':
            out_specs.append(tok(2 * s.width))
            out_shape.append(jax.ShapeDtypeStruct((bsz, length, 2 * s.width), BF16))
        else:
            out_specs.append(tok(s.width))
            out_shape.append(jax.ShapeDtypeStruct((bsz, length, s.width), BF16 if s.kind == 'norm' else s.dtype))
    return pl.pallas_call(
        functools.partial(_norm_proj_kernel, k=k, segs=tuple(segs), n_gain=len(gains), has_rope=has_rope),
        grid=(bsz, length // tm),
        in_specs=in_specs,
        out_specs=out_specs,
        out_shape=out_shape,
        compiler_params=_cp("parallel", "parallel"),
        name="norm_proj",
    )(*args)


def _swap_pairs(x):
    n = x.shape[-1]
    lane = lax.broadcasted_iota(jnp.int32, x.shape, 1)
    return jnp.where(lane % 2 == 0, pltpu.roll(x, n - 1, axis=1), pltpu.roll(x, 1, axis=1))


def _group_mean_matrix(w, group):
    idx = jnp.arange(w) // group
    return jnp.where(idx[:, None] == idx[None, :], 1.0 / group, 0.0).astype(BF16)


def _rope_tables(length):
    t = jnp.arange(length)
    row = (t // GRID_W).astype(F32)
    col = (t % GRID_W).astype(F32)
    n = HEAD_DIM // 4
    inv = ROPE_BASE ** (-jnp.arange(n, dtype=F32) / n)
    ang = jnp.concatenate([row[:, None] * inv, col[:, None] * inv], axis=-1)
    cos = jnp.repeat(jnp.cos(ang), 2, axis=-1)
    sin = jnp.repeat(jnp.sin(ang), 2, axis=-1)
    sign = jnp.tile(jnp.array([-1.0, 1.0], F32), HEAD_DIM // 2)
    reps = LANES // HEAD_DIM
    return jnp.tile(cos, (1, reps)), jnp.tile(sin * sign, (1, reps))


def _half_masks(shape):
    lane = lax.broadcasted_iota(jnp.int32, shape, len(shape) - 1)
    return lane < HEAD_DIM, lane >= HEAD_DIM


def _softmax_pv(s_list, v_list):
    m = s_list[0].max(axis=-1, keepdims=True)
    for s in s_list[1:]:
        m = jnp.maximum(m, s.max(axis=-1, keepdims=True))
    acc, l = None, None
    for s, v in zip(s_list, v_list):
        p = jnp.exp2(s - m)
        ls = p.sum(axis=-1, keepdims=True)
        o = _dot(p.astype(BF16), v)
        acc = o if acc is None else acc + o
        l = ls if l is None else l + ls
    return acc / l


def _na_kernel(q_ref, kp_ref, kc_ref, kn_ref, vp_ref, vc_ref, vn_ref, kctx_ref, vctx_ref, bias_ref,
               o_ref, kbuf, vbuf, vcbuf, *, rows_per_blk, n_rows):
    i = pl.program_id(1)
    blk = rows_per_blk * GRID_W
    n_pairs = q_ref.shape[-1] // LANES
    vw = 2 * LANES
    kbuf[0:blk, :] = kp_ref[...]
    kbuf[blk:2 * blk, :] = kc_ref[...]
    kbuf[2 * blk:3 * blk, :] = kn_ref[...]
    for hp in range(n_pairs):
        sl = slice(hp * LANES, (hp + 1) * LANES)
        vbuf[0:blk, hp * vw:hp * vw + LANES] = vp_ref[:, sl]
        vbuf[blk:2 * blk, hp * vw:hp * vw + LANES] = vc_ref[:, sl]
        vbuf[2 * blk:3 * blk, hp * vw:hp * vw + LANES] = vn_ref[:, sl]
        vbuf[:, hp * vw + LANES:(hp + 1) * vw] = jnp.ones((3 * blk, LANES), BF16)
        vcbuf[:, hp * vw:hp * vw + LANES] = vctx_ref[:, sl]
        vcbuf[:, hp * vw + LANES:(hp + 1) * vw] = jnp.ones((vcbuf.shape[0], LANES), BF16)
    win = NA_WIN_R * GRID_W

    lo, hi = _half_masks((GRID_W, LANES))

    def rows_body(jj, carry):
        rows = [jj * NA_ROWS_PER_ITER + u for u in range(NA_ROWS_PER_ITER)]
        offs, units = [], []
        for j in rows:
            r = i * rows_per_blk + j
            r0 = jnp.clip(r - NA_WIN_R // 2, 0, n_rows - NA_WIN_R)
            off = pl.multiple_of((r0 - (i - 1) * rows_per_blk) * GRID_W, GRID_W)
            cfg = r0 - r + NA_WIN_R - 1
            offs.append(off)
            qrow = q_ref[pl.ds(pl.multiple_of(j * GRID_W, GRID_W), GRID_W), :]
            for hp in range(n_pairs):
                sl = slice(hp * LANES, (hp + 1) * LANES)
                qp = qrow[:, sl]
                zero = jnp.zeros_like(qp)
                qst = jnp.concatenate([jnp.where(lo, qp, zero), jnp.where(hi, qp, zero)], axis=0)
                units.append((_dot_nt(qst, kbuf[pl.ds(off, win), sl]) + bias_ref[cfg, hp],
                              _dot_nt(qst, kctx_ref[:, sl])))
        probs = []
        for s_lat, s_ctx in units:
            m = jnp.maximum(s_lat.max(axis=-1, keepdims=True), s_ctx.max(axis=-1, keepdims=True))
            probs.append((jnp.exp2(s_lat - m).astype(BF16), jnp.exp2(s_ctx - m).astype(BF16)))
        for u, j in enumerate(rows):
            outs = []
            for hp in range(n_pairs):
                p_lat, p_ctx = probs[u * n_pairs + hp]
                ov = (_dot(p_lat, vbuf[pl.ds(offs[u], win), hp * vw:(hp + 1) * vw])
                      + _dot(p_ctx, vcbuf[:, hp * vw:(hp + 1) * vw]))
                o = ov[:, 0:LANES] / ov[:, LANES:vw]
                outs.append(jnp.where(lo, o[0:GRID_W], o[GRID_W:2 * GRID_W]))
            o_ref[pl.ds(pl.multiple_of(j * GRID_W, GRID_W), GRID_W), :] = (
                jnp.concatenate(outs, axis=1).astype(o_ref.dtype))
        return carry

    lax.fori_loop(0, rows_per_blk // NA_ROWS_PER_ITER, rows_body, 0)


def _na_bias_table(rpb):
    heads = rpb.shape[0]
    qc = jnp.arange(GRID_W)
    kc = jnp.arange(GRID_W)
    qc0 = jnp.clip(qc - NA_WIN_C // 2, 0, GRID_W - NA_WIN_C)
    col_in = (kc[None, :] >= qc0[:, None]) & (kc[None, :] < qc0[:, None] + NA_WIN_C)
    pad = GRID_W - NA_WIN_C
    rp = jnp.pad(rpb.astype(F32), ((0, 0), (0, 0), (pad, pad)))
    toep = jnp.stack([rp[:, :, GRID_W - 1 - q:2 * GRID_W - 1 - q] for q in range(GRID_W)], axis=2)
    toep = jnp.where(col_in[None, None], toep, NEG_BIG)
    tab = jnp.stack([toep[:, c:c + NA_WIN_R] for c in range(NA_WIN_R)], axis=0)
    tab = tab.transpose(0, 1, 3, 2, 4) * LOG2E
    return tab.reshape(NA_WIN_R, heads // 2, 2 * GRID_W, NA_WIN_R * GRID_W)


def _neighborhood_attention(q, k, v, kc, vc, bias):
    bsz, s, w = q.shape
    n_rows = s // GRID_W
    rpb_rows = NA_WIN_R
    blk = rpb_rows * GRID_W
    nb = s // blk
    lc = kc.shape[1]
    cur = pl.BlockSpec((None, blk, w), lambda b, i: (b, i, 0))
    prev = pl.BlockSpec((None, blk, w), lambda b, i: (b, jnp.maximum(i - 1, 0), 0))
    nxt = pl.BlockSpec((None, blk, w), lambda b, i: (b, jnp.minimum(i + 1, nb - 1), 0))
    ctx = pl.BlockSpec((None, lc, w), lambda b, i: (b, 0, 0))
    return pl.pallas_call(
        functools.partial(_na_kernel, rows_per_blk=rpb_rows, n_rows=n_rows),
        grid=(bsz, nb),
        in_specs=[cur, prev, cur, nxt, prev, cur, nxt, ctx, ctx, _const_spec(bias.shape)],
        out_specs=cur,
        out_shape=jax.ShapeDtypeStruct(q.shape, BF16),
        scratch_shapes=[pltpu.VMEM((3 * blk, w), BF16), pltpu.VMEM((3 * blk, 2 * w), BF16),
                        pltpu.VMEM((lc, 2 * w), BF16)],
        compiler_params=_cp("parallel", "parallel"),
        name="neighborhood_attention",
    )(q, k, k, k, v, v, v, kc, vc, bias)


def _ctx_attn_kernel(q_ref, k_ref, v_ref, o_ref):
    n_pairs = q_ref.shape[-1] // LANES
    outs = []
    for hp in range(n_pairs):
        sl = slice(hp * LANES, (hp + 1) * LANES)
        qp, kp, vp = q_ref[:, sl], k_ref[:, sl], v_ref[:, sl]
        o_pair = None
        for t, mask in enumerate(_half_masks(qp.shape)):
            qm = jnp.where(mask, qp, jnp.zeros_like(qp))
            vm = jnp.where(mask, vp, jnp.zeros_like(vp))
            o_t = _softmax_pv([_dot_nt(qm, kp)], [vm])
            o_pair = o_t if o_pair is None else o_pair + o_t
        outs.append(o_pair)
    o_ref[...] = jnp.concatenate(outs, axis=1).astype(o_ref.dtype)


def _ctx_attention(q, k, v):
    bsz, lc, w = q.shape
    spec = pl.BlockSpec((None, lc, w), lambda b: (b, 0, 0))
    return pl.pallas_call(
        _ctx_attn_kernel, grid=(bsz,), in_specs=[spec, spec, spec], out_specs=spec,
        out_shape=jax.ShapeDtypeStruct(q.shape, BF16),
        compiler_params=_cp("parallel"), name="ctx_attention",
    )(q, k, v)


def _conv_kernel(prev_ref, x_ref, next_ref, w_ref, b_ref, o_ref, *, act):
    i = pl.program_id(1)
    n = pl.num_programs(1)
    t = x_ref.shape[0]
    halo = prev_ref.shape[0]
    prev = jnp.where(i > 0, prev_ref[...], 0.0)
    nxt = jnp.where(i < n - 1, next_ref[...], 0.0)
    xx = jnp.concatenate([prev, x_ref[...], nxt], axis=0)
    taps = w_ref.shape[0]
    left = taps // 2
    tot = t + 2 * halo
    y = b_ref[...]
    for j in range(taps):
        sh = (left - j) % tot
        xs = xx if sh == 0 else pltpu.roll(xx, sh, axis=0)
        y = y + w_ref[j:j + 1, :] * xs[halo:halo + t]
    o_ref[...] = _silu(y) if act else y


def _dwconv(x, w, b, act):
    bsz, length, c = x.shape
    t = min(512, length)
    halo = 8
    per = t // halo
    nh = length // halo
    return pl.pallas_call(
        functools.partial(_conv_kernel, act=act),
        grid=(bsz, length // t),
        in_specs=[pl.BlockSpec((None, halo, c), lambda bb, i: (bb, jnp.maximum(i * per - 1, 0), 0)),
                  pl.BlockSpec((None, t, c), lambda bb, i: (bb, i, 0)),
                  pl.BlockSpec((None, halo, c), lambda bb, i: (bb, jnp.minimum((i + 1) * per, nh - 1), 0)),
                  _const_spec(w.shape), _const_spec((1, c))],
        out_specs=pl.BlockSpec((None, t, c), lambda bb, i: (bb, i, 0)),
        out_shape=jax.ShapeDtypeStruct(x.shape, F32),
        compiler_params=_cp("parallel", "parallel"),
        name="dwconv",
    )(x, x, x, w.astype(F32), b.astype(F32).reshape(1, c))


class _SsdUnit:
    def __init__(self, xbc_ref, dtc_ref, dtr_ref, pc_ref, pr_ref, y_ref, d, reverse):
        self.xbc_ref, self.dtc_ref, self.dtr_ref, self.pc_ref, self.pr_ref = xbc_ref, dtc_ref, dtr_ref, pc_ref, pr_ref
        self.y_ref, self.d, self.reverse = y_ref, d, reverse


def _ssd_stage_decay(u):
    q = u.xbc_ref.shape[0]
    n_heads = SSD_HEADS
    ii = lax.broadcasted_iota(jnp.int32, (q, q), 0)
    jj = lax.broadcasted_iota(jnp.int32, (q, q), 1)
    u.keep = (ii <= jj) if u.reverse else (ii >= jj)
    tri = u.keep.astype(F32)
    d = u.d
    dt_c = _softplus(u.dtc_ref[:, d * n_heads:(d + 1) * n_heads] + u.pc_ref[0:1, :])
    u.dt_r = _softplus(u.dtr_ref[d * n_heads:(d + 1) * n_heads, :] + u.pr_ref[:, 0:1])
    da_c = dt_c * u.pc_ref[1:2, :]
    da_r = u.dt_r * u.pr_ref[:, 1:2]
    u.acs_c = jnp.dot(tri, da_c, preferred_element_type=F32, precision=HIGHEST)
    u.acs_r = lax.dot_general(da_r, tri, (((1,), (1,)), ((), ())), preferred_element_type=F32,
                              precision=HIGHEST)
    edge = 0 if u.reverse else q - 1
    tot_r = u.acs_r[:, edge:edge + 1]
    u.w_end_r = jnp.exp(tot_r - u.acs_r) * u.dt_r
    u.e_tot_r = jnp.exp(tot_r)


def _ssd_stage_cb(u):
    gw = SSD_HEADS * HEAD_DIM
    u.bm_t, u.cm16, u.cb = [], [], []
    for g in range(2):
        bm = u.xbc_ref[:, gw + g * SSD_STATE:gw + (g + 1) * SSD_STATE]
        cm = u.xbc_ref[:, gw + 2 * SSD_STATE + g * SSD_STATE:gw + 2 * SSD_STATE + (g + 1) * SSD_STATE]
        u.bm_t.append(bm.T)
        u.cm16.append(cm.astype(BF16))
        u.cb.append(_dot(u.cm16[g], u.bm_t[g].astype(BF16)))


def _ssd_stage_local(u):
    q = u.xbc_ref.shape[0]
    lane_lo, lane_hi = _half_masks((q, LANES))
    u.y_diag, u.st_new, u.e_in, u.e_tot = [], [], [], []
    for pair in range(SSD_HEADS // 2):
        g = pair // 2
        h0 = 2 * pair
        xs = u.xbc_ref[:, pair * LANES:(pair + 1) * LANES]
        y_pair, st_new, e_in = None, None, []
        for t, lmask in enumerate((lane_lo, lane_hi)):
            h = h0 + t
            a_bc = jnp.broadcast_to(u.acs_c[:, h:h + 1], (q, q))
            seg = a_bc - u.acs_r[h:h + 1, :]
            dec = jnp.where(u.keep, jnp.exp(jnp.where(u.keep, seg, 0.0)), 0.0) * u.dt_r[h:h + 1, :]
            xm = jnp.where(lmask, xs, 0.0).astype(BF16)
            yd = _dot((u.cb[g] * dec).astype(BF16), xm)
            sn = _dot((u.bm_t[g] * u.w_end_r[h:h + 1, :]).astype(BF16), xm)
            y_pair = yd if y_pair is None else y_pair + yd
            st_new = sn if st_new is None else st_new + sn
            e_in.append(jnp.exp(a_bc))
        u.y_diag.append(y_pair)
        u.st_new.append(st_new)
        u.e_in.append(jnp.where(lane_lo, e_in[0], e_in[1]))
        u.e_tot.append(jnp.where(lane_lo[0:1], u.e_tot_r[h0:h0 + 1, :], u.e_tot_r[h0 + 1:h0 + 2, :]))


def _ssd_stage_state(u, state, dl_ref):
    new_state = []
    for pair in range(SSD_HEADS // 2):
        sl = slice(pair * LANES, (pair + 1) * LANES)
        y_pair = u.y_diag[pair] + _dot(u.cm16[pair // 2], state[pair].astype(BF16)) * u.e_in[pair]
        if dl_ref is not None:
            y_pair = y_pair + dl_ref[:, sl] * u.xbc_ref[:, sl]
        u.y_ref[:, sl] = y_pair
        new_state.append(state[pair] * u.e_tot[pair] + u.st_new[pair])
    return new_state


def _ssd_kernel(xf_ref, xb_ref, dtcf_ref, dtcb_ref, dtrf_ref, dtrb_ref, pc_ref, pr_ref, dl_ref, h0f_ref, h0b_ref,
                yf_ref, yb_ref, hf_ref, hb_ref, sf, sb):
    c = pl.program_id(1)

    @pl.when(c == 0)
    def _():
        sf[...] = h0f_ref[...]
        sb[...] = h0b_ref[...]

    q = SSD_CHUNK
    n_sub = xf_ref.shape[0] // q
    n_pairs = sf.shape[1] // LANES
    fwd, bwd = [], []
    for s in range(n_sub):
        f = slice(s * q, (s + 1) * q)
        r = slice((n_sub - 1 - s) * q, (n_sub - s) * q)
        fwd.append(_SsdUnit(xf_ref.at[f], dtcf_ref.at[f], dtrf_ref.at[:, f], pc_ref.at[0], pr_ref.at[0],
                            yf_ref.at[f], 0, False))
        bwd.append(_SsdUnit(xb_ref.at[r], dtcb_ref.at[r], dtrb_ref.at[:, r], pc_ref.at[1], pr_ref.at[1],
                            yb_ref.at[r], 1, True))
    units = [u for pair in zip(fwd, bwd) for u in pair]
    for stage in (_ssd_stage_decay, _ssd_stage_cb, _ssd_stage_local):
        for u in units:
            stage(u)
    st_f = [sf[:, p * LANES:(p + 1) * LANES] for p in range(n_pairs)]
    st_b = [sb[:, p * LANES:(p + 1) * LANES] for p in range(n_pairs)]
    for uf, ub in zip(fwd, bwd):
        st_f = _ssd_stage_state(uf, st_f, dl_ref)
        st_b = _ssd_stage_state(ub, st_b, None)
    for p in range(n_pairs):
        sf[:, p * LANES:(p + 1) * LANES] = st_f[p]
        sb[:, p * LANES:(p + 1) * LANES] = st_b[p]
    hf_ref[...] = sf[...]
    hb_ref[...] = sb[...]


def _ssd(xbc, dt, pc, pr, dl, h0f, h0b):
    bsz, length, cw = xbc.shape
    assert SSD_CHUNK == LANES, "the per-head decay matrix is built as one (chunk, 128-lane) tile"
    n_chunks = length // SSD_CHUNK
    per_step = next(c for c in range(SSD_CHUNKS_PER_STEP, 0, -1) if n_chunks % c == 0)
    q = SSD_CHUNK * per_step
    nc = length // q
    gw = SSD_HEADS * HEAD_DIM
    dtw = dt.shape[-1]
    dt_t = jnp.swapaxes(dt, 1, 2)
    fwd3 = lambda b, c: (b, c, 0)
    bwd3 = lambda b, c: (b, nc - 1 - c, 0)
    st_spec = pl.BlockSpec((None, SSD_STATE, gw), lambda b, c: (b, 0, 0))
    return pl.pallas_call(
        _ssd_kernel,
        grid=(bsz, nc),
        in_specs=[pl.BlockSpec((None, q, cw), fwd3), pl.BlockSpec((None, q, cw), bwd3),
                  pl.BlockSpec((None, q, dtw), fwd3), pl.BlockSpec((None, q, dtw), bwd3),
                  pl.BlockSpec((None, dtw, q), lambda b, c: (b, 0, c)),
                  pl.BlockSpec((None, dtw, q), lambda b, c: (b, 0, nc - 1 - c)),
                  _const_spec(pc.shape), _const_spec(pr.shape), _const_spec(dl.shape), st_spec, st_spec],
        out_specs=[pl.BlockSpec((None, q, gw), fwd3), pl.BlockSpec((None, q, gw), bwd3), st_spec, st_spec],
        out_shape=[jax.ShapeDtypeStruct((bsz, length, gw), F32)] * 2
                  + [jax.ShapeDtypeStruct((bsz, SSD_STATE, gw), F32)] * 2,
        scratch_shapes=[pltpu.VMEM((SSD_STATE, gw), F32)] * 2,
        compiler_params=_cp("parallel", "arbitrary"),
        name="ssd_scan",
    )(xbc, xbc, dt, dt, dt_t, dt_t, pc, pr, dl, h0f, h0b)


def _gelu_tanh(x):
    return 0.5 * x * (1.0 + jnp.tanh(math.sqrt(2.0 / math.pi) * (x + 0.044715 * (x * x * x))))


def _mix_ab(ona_ref, yf_ref, yb_ref, z_ref, ng_ref, w_ref):
    gw = ona_ref.shape[-1]
    y = (yf_ref[...] + yb_ref[...]) * _silu(z_ref[...])
    gated = y * lax.rsqrt(jnp.mean(y * y, axis=-1, keepdims=True) + EPS) * ng_ref[...]
    return _dot(ona_ref[...], w_ref[0:gw, :]) + _dot(gated.astype(BF16), w_ref[gw:2 * gw, :])


def _mix_cd(gate_ref, hf_ref, hb_ref, od_ref, w_ref):
    gw = od_ref.shape[-1]
    lru = _gelu_tanh(gate_ref[...]) * (hf_ref[...] + hb_ref[...])
    return _dot(lru.astype(BF16), w_ref[0:gw, :]) + _dot(od_ref[...], w_ref[gw:2 * gw, :])


def _mix_ffn_kernel(x_ref, mod_ref, *rest, mix, n_mix, k, chunk):
    g_ref, w1_ref, w3_ref, w2_ref, o_ref = rest[n_mix:]
    x = x_ref[...] + mod_ref[5:6, :] * mix(*rest[:n_mix])
    o_ref[...] = _ffn_body(x, mod_ref, g_ref, w1_ref, w3_ref, w2_ref, k, chunk)


def _mixer_out_ffn(mix, name, x, mod, parts, consts, g, w1, w3, w2, k):
    bsz, length, d = x.shape
    ff = w1.shape[1]
    tm = min(512, length)
    tok = lambda a: pl.BlockSpec((None, tm, a.shape[-1]), lambda b, i: (b, i, 0))
    return pl.pallas_call(
        functools.partial(_mix_ffn_kernel, mix=mix, n_mix=len(parts) + len(consts), k=k, chunk=FFN_CHUNK),
        grid=(bsz, length // tm),
        in_specs=[tok(x), _mod_spec(mod)] + [tok(a) for a in parts] + [_const_spec(a.shape) for a in consts]
                 + [_const_spec((1, d)), _const_spec((d, ff)), _const_spec((d, ff)), _const_spec((ff, d))],
        out_specs=tok(x),
        out_shape=jax.ShapeDtypeStruct(x.shape, F32),
        compiler_params=pltpu.CompilerParams(dimension_semantics=("parallel", "parallel"),
                                             vmem_limit_bytes=DIFF_VMEM_LIMIT),
        name=name,
    )(x, mod, *parts, *consts, g.reshape(1, d), w1, w3, w2)


def _lru_direction(x_ref, wa_ref, wx_ref, p_ref, carry, h_ref, *, reverse):
    t = x_ref.shape[0]
    x = x_ref[...]
    x16 = x.astype(BF16)
    r = _sigmoid(_dot(x16, wa_ref[...]) + p_ref[0:1, :])
    ig = _sigmoid(_dot(x16, wx_ref[...]) + p_ref[1:2, :])
    log_a = -LRU_C * r * _softplus(-p_ref[2:3, :])
    a = jnp.exp(log_a)
    b = jnp.sqrt(1.0 - a * a) * (ig * x)
    n_groups = t // SUBLANES
    a = a.reshape(n_groups, SUBLANES, a.shape[-1])
    b = b.reshape(n_groups, SUBLANES, b.shape[-1])
    row = lax.broadcasted_iota(jnp.int32, a.shape, 1)
    s = 1
    while s < SUBLANES:
        fill = (row >= SUBLANES - s) if reverse else (row < s)
        shift = SUBLANES - s if reverse else s
        a_sh = jnp.where(fill, 1.0, pltpu.roll(a, shift, axis=1))
        b_sh = jnp.where(fill, 0.0, pltpu.roll(b, shift, axis=1))
        b = a * b_sh + b
        a = a * a_sh
        s *= 2
    h_prev = carry[...]
    for gi in (range(n_groups - 1, -1, -1) if reverse else range(n_groups)):
        r0 = gi * SUBLANES
        hg = a[gi] * h_prev + b[gi]
        h_ref[r0:r0 + SUBLANES, :] = hg
        h_prev = hg[0:1] if reverse else hg[SUBLANES - 1:SUBLANES]
    carry[...] = h_prev


def _lru_kernel(xf_ref, xb_ref, wa_ref, wx_ref, p_ref, h0f_ref, h0b_ref, hf_ref, hb_ref, lf_ref, lb_ref, cf, cb):
    c = pl.program_id(1)

    @pl.when(c == 0)
    def _():
        cf[...] = h0f_ref[...]
        cb[...] = h0b_ref[...]

    _lru_direction(xf_ref, wa_ref.at[0], wx_ref.at[0], p_ref.at[0], cf, hf_ref, reverse=False)
    _lru_direction(xb_ref, wa_ref.at[1], wx_ref.at[1], p_ref.at[1], cb, hb_ref, reverse=True)
    lf_ref[...] = cf[...]
    lb_ref[...] = cb[...]


def _lru(x, wa, wx, p, h0f, h0b):
    bsz, length, w = x.shape
    t = min(512, length)
    nt = length // t
    fwd = lambda b, c: (b, c, 0)
    bwd = lambda b, c: (b, nt - 1 - c, 0)
    st = pl.BlockSpec((None, 1, w), lambda b, c: (b, 0, 0))
    return pl.pallas_call(
        _lru_kernel,
        grid=(bsz, nt),
        in_specs=[pl.BlockSpec((None, t, w), fwd), pl.BlockSpec((None, t, w), bwd),
                  _const_spec(wa.shape), _const_spec(wx.shape), _const_spec(p.shape), st, st],
        out_specs=[pl.BlockSpec((None, t, w), fwd), pl.BlockSpec((None, t, w), bwd), st, st],
        out_shape=[jax.ShapeDtypeStruct(x.shape, F32)] * 2 + [jax.ShapeDtypeStruct((bsz, 1, w), F32)] * 2,
        scratch_shapes=[pltpu.VMEM((1, w), F32)] * 2,
        compiler_params=_cp("parallel", "arbitrary"),
        name="rglru_scan",
    )(x, x, wa, wx, p, h0f, h0b)


def _block_diag(wb):
    nb, bs, _ = wb.shape
    eye = jnp.eye(nb, dtype=wb.dtype)
    return (wb[:, :, None, :] * eye[:, None, :, None]).reshape(nb * bs, nb * bs)


def _diff_attn_kernel(q_ref, kt_ref, v_ref, lam_ref, sg_ref, o_ref, qs, m_s, acc_s, s_buf, *, lam_init, n_heads, tk):
    tq = q_ref.shape[0]
    vw = 2 * LANES
    for h in range(n_heads):
        qp = q_ref[:, h * LANES:(h + 1) * LANES]
        lo, hi = _half_masks(qp.shape)
        qs[h, 0:tq, :] = jnp.where(lo, qp, jnp.zeros_like(qp))
        qs[h, tq:2 * tq, :] = jnp.where(hi, qp, jnp.zeros_like(qp))
    m_s[...] = jnp.full(m_s.shape, NEG_BIG, F32)
    acc_s[...] = jnp.zeros(acc_s.shape, F32)

    n_chunks = kt_ref.shape[1] // tk

    def scores(h, k0):
        return _dot(qs[h], kt_ref[h * LANES:(h + 1) * LANES, pl.ds(k0, tk)])

    s_buf[0] = scores(0, 0)

    def chunk(c, carry):
        k0 = pl.multiple_of(c * tk, tk)
        k_next = pl.multiple_of(jnp.minimum(c + 1, n_chunks - 1) * tk, tk)
        for h in range(n_heads):
            if h + 1 < n_heads:
                s_buf[(h + 1) % 2] = scores(h + 1, k0)
            else:
                s_buf[0] = scores(0, k_next)
            s = s_buf[h % 2]
            m_old = m_s[h]
            m_new = jnp.maximum(m_old, s.max(axis=-1, keepdims=True))
            alpha = jnp.exp2(m_old - m_new)
            p = jnp.concatenate([jnp.exp2(s[:, j * LANES:(j + 1) * LANES] - m_new).astype(BF16)
                                 for j in range(tk // LANES)], axis=1)
            pv = _dot(p, v_ref[pl.ds(k0, tk), h * vw:(h + 1) * vw])
            acc_s[h] = jnp.concatenate([alpha, alpha], axis=1) * acc_s[h] + pv
            m_s[h] = m_new
        return carry

    lax.fori_loop(0, n_chunks, chunk, 0)

    dl = lam_ref[...]
    lam = (jnp.exp(jnp.sum(dl[0:1] * dl[1:2], axis=-1, keepdims=True))
           - jnp.exp(jnp.sum(dl[2:3] * dl[3:4], axis=-1, keepdims=True)) + lam_init)
    for h in range(n_heads):
        a = acc_s[h]
        o = a[0:tq, 0:LANES] / a[0:tq, LANES:vw] - lam * (a[tq:2 * tq, 0:LANES] / a[tq:2 * tq, LANES:vw])
        o = o * lax.rsqrt(jnp.mean(o * o, axis=-1, keepdims=True) + EPS) * sg_ref[...] * (1.0 - lam_init)
        o_ref[:, h * LANES:(h + 1) * LANES] = o.astype(o_ref.dtype)


def _diff_attention(q, kt_all, v_ext, diff_lam, subln_g, lam_init):
    bsz, s, w = q.shape
    lk = kt_all.shape[2]
    n_heads = w // LANES
    tq = min(DIFF_TQ, s)
    tk = next(c for c in (768, 512, 256, 128) if lk % c == 0)
    return pl.pallas_call(
        functools.partial(_diff_attn_kernel, lam_init=lam_init, n_heads=n_heads, tk=tk),
        grid=(bsz, s // tq),
        in_specs=[pl.BlockSpec((None, tq, w), lambda b, i: (b, i, 0)),
                  pl.BlockSpec((None, w, lk), lambda b, i: (b, 0, 0), pipeline_mode=pl.Buffered(1)),
                  pl.BlockSpec((None, lk, v_ext.shape[2]), lambda b, i: (b, 0, 0), pipeline_mode=pl.Buffered(1)),
                  _const_spec(diff_lam.shape), _const_spec((1, LANES))],
        out_specs=pl.BlockSpec((None, tq, w), lambda b, i: (b, i, 0)),
        out_shape=jax.ShapeDtypeStruct(q.shape, BF16),
        scratch_shapes=[pltpu.VMEM((n_heads, 2 * tq, LANES), BF16), pltpu.VMEM((n_heads, 2 * tq, LANES), F32),
                        pltpu.VMEM((n_heads, 2 * tq, 2 * LANES), F32), pltpu.VMEM((2, 2 * tq, tk), F32)],
        compiler_params=pltpu.CompilerParams(dimension_semantics=("parallel", "parallel"),
                                             vmem_limit_bytes=DIFF_VMEM_LIMIT),
        name="diff_attention",
    )(q, kt_all, v_ext, diff_lam.astype(F32), subln_g.astype(F32).reshape(1, LANES))


def _pad_cols(w, total):
    return jnp.pad(w, ((0, 0), (0, total - w.shape[1])))


def _layer_ab(x, xc, m, mc, g_mix, w_in, w_out, q_g, k_g, rpb, conv_w, conv_b, dt_bias, a_log, d_skip, norm_g,
              ffn2):
    gw = SSD_HEADS * HEAD_DIM
    segs = [_Seg(gw, 'norm', gain=q_g, scale=HEAD_DIM ** -0.5 * LOG2E), _Seg(gw, 'norm', gain=k_g),
            _Seg(gw, dtype=BF16), _Seg(gw), _Seg(2 * gw), _Seg(LANES)]
    w16 = _pad_cols(w_in, sum(s.width for s in segs)).astype(BF16)
    pmat = _group_mean_matrix(gw, HEAD_DIM)
    q, k, v, z, xbc, dt = _norm_proj(x, m, g_mix, w16, 1, segs, pmat)
    q_c, k_c, v_c, z_c, xbc_c, dt_c = _norm_proj(xc, mc, g_mix, w16, 1, segs, pmat)
    o_na = _neighborhood_attention(q, k, v, k_c, v_c, _na_bias_table(rpb))
    o_c = _ctx_attention(q_c, k_c, v_c)

    a_neg = -jnp.exp(a_log.astype(F32))
    pc = jnp.stack([dt_bias.astype(F32), a_neg], axis=1)
    pr = jnp.swapaxes(pc, 1, 2)
    dl = jnp.repeat(d_skip.astype(F32), HEAD_DIM).reshape(1, gw)
    cw = conv_w.astype(F32)
    xbc_c = _dwconv(xbc_c, cw, conv_b, act=True)
    xbc = _dwconv(xbc, cw, conv_b, act=True)
    zeros = jnp.zeros((x.shape[0], SSD_STATE, gw), F32)
    yf_c, yb_c, hf_c, hb_c = _ssd(xbc_c, dt_c, pc, pr, dl, zeros, zeros)
    yf, yb, _, _ = _ssd(xbc, dt, pc, pr, dl, hf_c, hb_c)

    ng = norm_g.astype(F32).reshape(1, gw)
    wo16 = w_out.astype(BF16)
    x = _mixer_out_ffn(_mix_ab, "mixer_out_ab_ffn", x, m, [o_na, yf, yb, z], [ng, wo16], *ffn2, 2)
    xc = _mixer_out_ffn(_mix_ab, "mixer_out_ab_ffn", xc, mc, [o_c, yf_c, yb_c, z_c], [ng, wo16], *ffn2, 2)
    return x, xc


def _layer_cd(x, xc, m, mc, g_mix, w_in, w_out, conv_w, conv_b, wa, ba, wx, bx, lam_p, q_g, k_g, diff_lam,
              subln_g, lam_init, ffn2):
    gw = w_out.shape[0] // 2
    w16 = w_in.astype(BF16)
    pmat = _group_mean_matrix(gw, HEAD_DIM)
    tabs = _rope_tables(x.shape[1])
    q_scale = HEAD_DIM ** -0.5 * LOG2E

    def segs(rope):
        return [_Seg(gw), _Seg(gw), _Seg(gw, 'norm', gain=q_g, scale=q_scale, rope=rope),
                _Seg(gw, 'norm_t', gain=k_g, rope=rope), _Seg(gw, 'vext')]

    gate, xr, q, kt, v_ext = _norm_proj(x, m, g_mix, w16, 1, segs(True), pmat, tabs)
    _, xr_c, _, kt_c, v_ext_c = _norm_proj(xc, mc, g_mix, w16, 1, segs(False), pmat)

    cw = conv_w.astype(F32)
    xr_c = _dwconv(xr_c, cw, conv_b, act=False)
    xr = _dwconv(xr, cw, conv_b, act=False)
    wa_d = jnp.stack([_block_diag(wa[0]), _block_diag(wa[1])]).astype(BF16)
    wx_d = jnp.stack([_block_diag(wx[0]), _block_diag(wx[1])]).astype(BF16)
    p = jnp.stack([ba.astype(F32), bx.astype(F32), lam_p.astype(F32)], axis=1)
    zeros = jnp.zeros((x.shape[0], 1, gw), F32)
    _, _, lf_c, lb_c = _lru(xr_c, wa_d, wx_d, p, zeros, zeros)
    hf, hb, _, _ = _lru(xr, wa_d, wx_d, p, lf_c, lb_c)

    kt_all = jnp.concatenate([kt_c, kt], axis=2)
    v_all = jnp.concatenate([v_ext_c, v_ext], axis=1)
    o = _diff_attention(q, kt_all, v_all, diff_lam, subln_g, lam_init)
    return _mixer_out_ffn(_mix_cd, "mixer_out_cd_ffn", x, m, [gate, hf, hb, o], [w_out.astype(BF16)], *ffn2, 2)


def kernel(x, c, ctx, c_ctx, w_mod, b_mod, norm_g, ffn_w1, ffn_w3, ffn_w2, ab_w_in, ab_w_out, na_q_g, na_k_g, na_rpb, ssd_conv_w, ssd_conv_b, ssd_dt_bias, ssd_a_log, ssd_d, ssd_norm_g, cd_w_in, cd_w_out, lru_conv_w, lru_conv_b, lru_wa, lru_ba, lru_wx, lru_bx, lru_lambda, diff_q_g, diff_k_g, diff_lambda, diff_subln_g):
    bsz, _, d = x.shape
    depth = w_mod.shape[0]
    cc = jnp.concatenate([c.astype(F32), c_ctx.astype(F32)[None], jnp.zeros((8 - bsz - 1, d), F32)], axis=0)
    mods = _modulation(cc, w_mod.astype(F32), b_mod.astype(F32))
    xc = ctx
    for i in range(depth):
        last = i == depth - 1
        j = i // 2
        m = mods[i, :bsz].reshape(bsz, N_MOD, d)
        mc = mods[i, bsz:bsz + 1].reshape(1, N_MOD, d)
        g = norm_g[i].astype(F32)
        ffn1 = (g[0], ffn_w1[i, 0].astype(BF16), ffn_w3[i, 0].astype(BF16), ffn_w2[i, 0].astype(BF16))
        ffn2 = (g[2], ffn_w1[i, 1].astype(BF16), ffn_w3[i, 1].astype(BF16), ffn_w2[i, 1].astype(BF16))
        x = _ffn(x, m, *ffn1, 0)
        xc = _ffn(xc, mc, *ffn1, 0)
        if i % 2 == 0:
            x, xc = _layer_ab(x, xc, m, mc, g[1], ab_w_in[j], ab_w_out[j], na_q_g[j], na_k_g[j], na_rpb[j],
                              ssd_conv_w[j], ssd_conv_b[j], ssd_dt_bias[j], ssd_a_log[j], ssd_d[j], ssd_norm_g[j],
                              ffn2)
        else:
            assert last, "a C|D layer that is not the last one would also need the context stream's mixer output"
            lam_init = 0.8 - 0.6 * math.exp(-0.3 * i)
            x = _layer_cd(x, xc, m, mc, g[1], cd_w_in[j], cd_w_out[j], lru_conv_w[j], lru_conv_b[j], lru_wa[j],
                          lru_ba[j], lru_wx[j], lru_bx[j], lru_lambda[j], diff_q_g[j], diff_k_g[j],
                          diff_lambda[j], diff_subln_g[j], lam_init, ffn2)
    return x
```

```python
import functools
import math

import jax
import jax.numpy as jnp
from jax import lax
from jax.experimental import pallas as pl
from jax.experimental.pallas import tpu as pltpu

F32 = jnp.float32
BF16 = jnp.bfloat16
HIGHEST = lax.Precision.HIGHEST

GRID_W = 64
EPS = 1e-6
HEAD_DIM = 64
N_MOD = 9
NA_WIN_R = 8
NA_WIN_C = 16
SSD_STATE = 128
SSD_CHUNK = 128
SSD_HEADS = 8
LRU_C = 8.0
ROPE_BASE = 10000.0
NEG_BIG = -1e30

LANES = 128
SUBLANES = 8
VMEM_LIMIT = 48 * 1024 * 1024
DIFF_VMEM_LIMIT = 56 * 1024 * 1024
DIFF_TQ = 512
FFN_ROW_BLOCKS = 2
NA_ROWS_PER_ITER = 4
SSD_CHUNKS_PER_STEP = 4
MXU_TILE = 256
FFN_CHUNK = 6 * MXU_TILE
LOG2E = math.log2(math.e)


def _cp(*sem):
    return pltpu.CompilerParams(dimension_semantics=sem, vmem_limit_bytes=VMEM_LIMIT)


def _const_spec(shape):
    nd = len(shape)
    return pl.BlockSpec(shape, lambda *_: (0,) * nd, pipeline_mode=pl.Buffered(1))


def _dot(a, b):
    return jnp.dot(a, b, preferred_element_type=F32)


def _dot_nt(a, b):
    return lax.dot_general(a, b, (((1,), (1,)), ((), ())), preferred_element_type=F32)


def _sigmoid(x):
    return 0.5 * jnp.tanh(0.5 * x) + 0.5


def _silu(x):
    return x * _sigmoid(x)


def _softplus(x):
    return jnp.maximum(x, 0.0) + jnp.log(1.0 + jnp.exp(-jnp.abs(x)))


def _rms_mod(x, g, shift, scale):
    ms = jnp.mean(x * x, axis=-1, keepdims=True)
    return (x * lax.rsqrt(ms + EPS) * g) * (1.0 + scale) + shift


def _mod_kernel(c_ref, w_ref, b_ref, o_ref):
    s = _silu(c_ref[...])
    o_ref[...] = jnp.dot(s, w_ref[...], preferred_element_type=F32, precision=HIGHEST) + b_ref[...]


def _modulation(cc, w_mod, b_mod):
    depth, d, n = w_mod.shape
    tn = 1024
    return pl.pallas_call(
        _mod_kernel,
        grid=(depth, n // tn),
        in_specs=[pl.BlockSpec((8, d), lambda l, j: (0, 0)),
                  pl.BlockSpec((None, d, tn), lambda l, j: (l, 0, j)),
                  pl.BlockSpec((None, 1, tn), lambda l, j: (l, 0, j))],
        out_specs=pl.BlockSpec((None, 8, tn), lambda l, j: (l, 0, j)),
        out_shape=jax.ShapeDtypeStruct((depth, 8, n), F32),
        compiler_params=_cp("parallel", "parallel"),
        name="modulation",
    )(cc, w_mod, b_mod.reshape(depth, 1, n))


def _mod_spec(mod):
    if mod.shape[0] == 1:
        return pl.BlockSpec((None, N_MOD, mod.shape[2]), lambda b, i: (0, 0, 0))
    return pl.BlockSpec((None, N_MOD, mod.shape[2]), lambda b, i: (b, 0, 0))


def _ffn_body(x, mod_ref, g_ref, w1_ref, w3_ref, w2_ref, k, chunk):
    tm = x.shape[0]
    rb = tm // FFN_ROW_BLOCKS if tm % (FFN_ROW_BLOCKS * SUBLANES) == 0 else tm
    ff = w1_ref.shape[1]
    outs = []
    for r0 in range(0, tm, rb):
        xr = x[r0:r0 + rb]
        h = _rms_mod(xr, g_ref[...], mod_ref[3 * k:3 * k + 1, :], mod_ref[3 * k + 1:3 * k + 2, :]).astype(BF16)
        acc = jnp.zeros(xr.shape, F32)
        for c0 in range(0, ff, chunk):
            c1 = min(c0 + chunk, ff)
            a = _dot(h, w1_ref[:, c0:c1])
            b = _dot(h, w3_ref[:, c0:c1])
            acc = acc + _dot((_silu(a) * b).astype(BF16), w2_ref[c0:c1, :])
        outs.append(xr + (0.5 * mod_ref[3 * k + 2:3 * k + 3, :]) * acc)
    return outs[0] if len(outs) == 1 else jnp.concatenate(outs, axis=0)


def _ffn_kernel(x_ref, mod_ref, g_ref, w1_ref, w3_ref, w2_ref, o_ref, *, k, chunk):
    o_ref[...] = _ffn_body(x_ref[...], mod_ref, g_ref, w1_ref, w3_ref, w2_ref, k, chunk)


def _ffn(x, mod, g, w1, w3, w2, k):
    bsz, length, d = x.shape
    if mod.shape[0] == 1 and bsz > 1:
        return _ffn(x.reshape(1, bsz * length, d), mod, g, w1, w3, w2, k).reshape(x.shape)
    ff = w1.shape[1]
    tm = min(512, length)
    return pl.pallas_call(
        functools.partial(_ffn_kernel, k=k, chunk=FFN_CHUNK),
        grid=(bsz, length // tm),
        in_specs=[pl.BlockSpec((None, tm, d), lambda b, i: (b, i, 0)),
                  _mod_spec(mod),
                  _const_spec((1, d)),
                  _const_spec((d, ff)), _const_spec((d, ff)), _const_spec((ff, d))],
        out_specs=pl.BlockSpec((None, tm, d), lambda b, i: (b, i, 0)),
        out_shape=jax.ShapeDtypeStruct(x.shape, F32),
        compiler_params=_cp("parallel", "parallel"),
        name="half_ffn",
    )(x, mod, g.reshape(1, d), w1, w3, w2)


class _Seg:
    def __init__(self, width, kind='plain', dtype=F32, gain=None, scale=1.0, rope=False, conv_w=None, conv_b=None,
                 act=False):
        self.width, self.kind, self.dtype, self.gain, self.scale, self.rope = width, kind, dtype, gain, scale, rope
        self.conv_w, self.conv_b, self.act = conv_w, conv_b, act


def _head_norm(y, gain, pmat, scale, rope_tabs):
    ms = _dot((y * y).astype(BF16), pmat)
    y = y * lax.rsqrt(ms + EPS) * gain
    if rope_tabs is not None:
        reps = y.shape[-1] // rope_tabs[0].shape[-1]
        cs = jnp.concatenate([rope_tabs[0]] * reps, axis=1)
        sn = jnp.concatenate([rope_tabs[1]] * reps, axis=1)
        y = y * cs + _swap_pairs(y) * sn
    return y * scale


def _conv_taps(xx, w_ref, b_ref, t, halo, act):
    taps = w_ref.shape[0]
    left = taps // 2
    tot = t + 2 * halo
    y = b_ref[...]
    for j in range(taps):
        sh = (left - j) % tot
        xs = xx if sh == 0 else pltpu.roll(xx, sh, axis=0)
        y = y + w_ref[j:j + 1, :] * xs[halo:halo + t]
    return _silu(y) if act else y


def _norm_proj_kernel(x_ref, mod_ref, g_ref, w_ref, *rest, k, segs, n_gain, has_rope, n_conv):
    pos = 0
    pmat = None
    if n_gain:
        pmat = rest[0][...]
        pos = 1
    gains = rest[pos:pos + n_gain]
    pos += n_gain
    rope_tabs = None
    if has_rope:
        rope_tabs = (rest[pos][...], rest[pos + 1][...])
        pos += 2
    shift, scale = mod_ref[3 * k:3 * k + 1, :], mod_ref[3 * k + 1:3 * k + 2, :]
    h_halo = None
    if n_conv:
        halo_refs = rest[pos:pos + 2]
        conv_refs = rest[pos + 2:pos + 2 + 2 * n_conv]
        pos += 2 + 2 * n_conv
        h_halo = [_rms_mod(r[...], g_ref[...], shift, scale).astype(BF16) for r in halo_refs]
        i, n = pl.program_id(1), pl.num_programs(1)
    o_refs = rest[pos:]
    h = _rms_mod(x_ref[...], g_ref[...], shift, scale).astype(BF16)
    off, gi, ci = 0, 0, 0
    for seg, o_ref in zip(segs, o_refs):
        w_seg = w_ref[:, off:off + seg.width]
        y = _dot(h, w_seg)
        off += seg.width
        if seg.kind == 'conv':
            y_prev = jnp.where(i > 0, _dot(h_halo[0], w_seg), 0.0)
            y_next = jnp.where(i < n - 1, _dot(h_halo[1], w_seg), 0.0)
            xx = jnp.concatenate([y_prev, y, y_next], axis=0)
            o_ref[...] = _conv_taps(xx, conv_refs[2 * ci], conv_refs[2 * ci + 1], y.shape[0], y_prev.shape[0],
                                    seg.act).astype(o_ref.dtype)
            ci += 1
        elif seg.kind in ('norm', 'norm_t'):
            y = _head_norm(y, gains[gi][...], pmat, seg.scale, rope_tabs if seg.rope else None)
            gi += 1
            o_ref[...] = (y.T if seg.kind == 'norm_t' else y).astype(o_ref.dtype)
        elif seg.kind == 'vext':
            y16 = y.astype(BF16)
            ones = jnp.ones((y.shape[0], LANES), BF16)
            for hp in range(seg.width // LANES):
                o_ref[:, 2 * hp * LANES:(2 * hp + 1) * LANES] = y16[:, hp * LANES:(hp + 1) * LANES]
                o_ref[:, (2 * hp + 1) * LANES:(2 * hp + 2) * LANES] = ones
        else:
            o_ref[...] = y.astype(o_ref.dtype)


def _norm_proj(x, mod, g, w, k, segs, pmat=None, rope_tabs=None):
    bsz, length, d = x.shape
    tm = min(512, length)
    tok = lambda wd: pl.BlockSpec((None, tm, wd), lambda b, i: (b, i, 0))
    in_specs = [tok(d), _mod_spec(mod), _const_spec((1, d)), _const_spec(w.shape)]
    args = [x, mod, g.reshape(1, d), w]
    gains = [s for s in segs if s.kind in ('norm', 'norm_t')]
    if gains:
        in_specs.append(_const_spec(pmat.shape))
        args.append(pmat)
        for s in gains:
            in_specs.append(_const_spec((1, s.width)))
            args.append(jnp.tile(s.gain.astype(F32), s.width // s.gain.shape[0]).reshape(1, s.width))
    has_rope = any(s.rope for s in segs)
    if has_rope:
        in_specs += [pl.BlockSpec((tm, rope_tabs[0].shape[-1]), lambda b, i: (i, 0))] * 2
        args += list(rope_tabs)
    convs = [s for s in segs if s.kind == 'conv']
    if convs:
        per = tm // SUBLANES
        nh = length // SUBLANES
        in_specs += [pl.BlockSpec((None, SUBLANES, d), lambda b, i: (b, jnp.maximum(i * per - 1, 0), 0)),
                     pl.BlockSpec((None, SUBLANES, d), lambda b, i: (b, jnp.minimum((i + 1) * per, nh - 1), 0))]
        args += [x, x]
        for s in convs:
            in_specs += [_const_spec(s.conv_w.shape), _const_spec((1, s.width))]
            args += [s.conv_w.astype(F32), s.conv_b.astype(F32).reshape(1, s.width)]
    out_specs, out_shape = [], []
    for s in segs:
        if s.kind == 'norm_t':
            out_specs.append(pl.BlockSpec((None, s.width, tm), lambda b, i: (b, 0, i)))
            out_shape.append(jax.ShapeDtypeStruct((bsz, s.width, length), BF16))
        elif s.kind == 'vext':
            out_specs.append(tok(2 * s.width))
            out_shape.append(jax.ShapeDtypeStruct((bsz, length, 2 * s.width), BF16))
        else:
            out_specs.append(tok(s.width))
            out_shape.append(jax.ShapeDtypeStruct((bsz, length, s.width), BF16 if s.kind == 'norm' else s.dtype))
    return pl.pallas_call(
        functools.partial(_norm_proj_kernel, k=k, segs=tuple(segs), n_gain=len(gains), has_rope=has_rope,
                          n_conv=len(convs)),
        grid=(bsz, length // tm),
        in_specs=in_specs,
        out_specs=out_specs,
        out_shape=out_shape,
        compiler_params=_cp("parallel", "parallel"),
        name="norm_proj",
    )(*args)


def _swap_pairs(x):
    n = x.shape[-1]
    lane = lax.broadcasted_iota(jnp.int32, x.shape, 1)
    return jnp.where(lane % 2 == 0, pltpu.roll(x, n - 1, axis=1), pltpu.roll(x, 1, axis=1))


def _group_mean_matrix(w, group):
    idx = jnp.arange(w) // group
    return jnp.where(idx[:, None] == idx[None, :], 1.0 / group, 0.0).astype(BF16)


def _rope_tables(length):
    t = jnp.arange(length)
    row = (t // GRID_W).astype(F32)
    col = (t % GRID_W).astype(F32)
    n = HEAD_DIM // 4
    inv = ROPE_BASE ** (-jnp.arange(n, dtype=F32) / n)
    ang = jnp.concatenate([row[:, None] * inv, col[:, None] * inv], axis=-1)
    cos = jnp.repeat(jnp.cos(ang), 2, axis=-1)
    sin = jnp.repeat(jnp.sin(ang), 2, axis=-1)
    sign = jnp.tile(jnp.array([-1.0, 1.0], F32), HEAD_DIM // 2)
    reps = LANES // HEAD_DIM
    return jnp.tile(cos, (1, reps)), jnp.tile(sin * sign, (1, reps))


def _half_masks(shape):
    lane = lax.broadcasted_iota(jnp.int32, shape, len(shape) - 1)
    return lane < HEAD_DIM, lane >= HEAD_DIM


def _softmax_pv(s_list, v_list):
    m = s_list[0].max(axis=-1, keepdims=True)
    for s in s_list[1:]:
        m = jnp.maximum(m, s.max(axis=-1, keepdims=True))
    acc, l = None, None
    for s, v in zip(s_list, v_list):
        p = jnp.exp2(s - m)
        ls = p.sum(axis=-1, keepdims=True)
        o = _dot(p.astype(BF16), v)
        acc = o if acc is None else acc + o
        l = ls if l is None else l + ls
    return acc / l


def _na_kernel(q_ref, kp_ref, kc_ref, kn_ref, vp_ref, vc_ref, vn_ref, kctx_ref, vctx_ref, bias_ref,
               o_ref, kbuf, vbuf, vcbuf, *, rows_per_blk, n_rows):
    i = pl.program_id(1)
    blk = rows_per_blk * GRID_W
    n_pairs = q_ref.shape[-1] // LANES
    vw = 2 * LANES
    kbuf[0:blk, :] = kp_ref[...]
    kbuf[blk:2 * blk, :] = kc_ref[...]
    kbuf[2 * blk:3 * blk, :] = kn_ref[...]
    for hp in range(n_pairs):
        sl = slice(hp * LANES, (hp + 1) * LANES)
        vbuf[0:blk, hp * vw:hp * vw + LANES] = vp_ref[:, sl]
        vbuf[blk:2 * blk, hp * vw:hp * vw + LANES] = vc_ref[:, sl]
        vbuf[2 * blk:3 * blk, hp * vw:hp * vw + LANES] = vn_ref[:, sl]
        vbuf[:, hp * vw + LANES:(hp + 1) * vw] = jnp.ones((3 * blk, LANES), BF16)
        vcbuf[:, hp * vw:hp * vw + LANES] = vctx_ref[:, sl]
        vcbuf[:, hp * vw + LANES:(hp + 1) * vw] = jnp.ones((vcbuf.shape[0], LANES), BF16)
    win = NA_WIN_R * GRID_W

    lo, hi = _half_masks((GRID_W, LANES))

    def rows_body(jj, carry):
        rows = [jj * NA_ROWS_PER_ITER + u for u in range(NA_ROWS_PER_ITER)]
        offs, units = [], []
        for j in rows:
            r = i * rows_per_blk + j
            r0 = jnp.clip(r - NA_WIN_R // 2, 0, n_rows - NA_WIN_R)
            off = pl.multiple_of((r0 - (i - 1) * rows_per_blk) * GRID_W, GRID_W)
            cfg = r0 - r + NA_WIN_R - 1
            offs.append(off)
            qrow = q_ref[pl.ds(pl.multiple_of(j * GRID_W, GRID_W), GRID_W), :]
            for hp in range(n_pairs):
                sl = slice(hp * LANES, (hp + 1) * LANES)
                qp = qrow[:, sl]
                zero = jnp.zeros_like(qp)
                qst = jnp.concatenate([jnp.where(lo, qp, zero), jnp.where(hi, qp, zero)], axis=0)
                units.append((_dot_nt(qst, kbuf[pl.ds(off, win), sl]) + bias_ref[cfg, hp],
                              _dot_nt(qst, kctx_ref[:, sl])))
        probs = []
        for s_lat, s_ctx in units:
            m = jnp.maximum(s_lat.max(axis=-1, keepdims=True), s_ctx.max(axis=-1, keepdims=True))
            probs.append((jnp.exp2(s_lat - m).astype(BF16), jnp.exp2(s_ctx - m).astype(BF16)))
        for u, j in enumerate(rows):
            outs = []
            for hp in range(n_pairs):
                p_lat, p_ctx = probs[u * n_pairs + hp]
                ov = (_dot(p_lat, vbuf[pl.ds(offs[u], win), hp * vw:(hp + 1) * vw])
                      + _dot(p_ctx, vcbuf[:, hp * vw:(hp + 1) * vw]))
                o = ov[:, 0:LANES] / ov[:, LANES:vw]
                outs.append(jnp.where(lo, o[0:GRID_W], o[GRID_W:2 * GRID_W]))
            o_ref[pl.ds(pl.multiple_of(j * GRID_W, GRID_W), GRID_W), :] = (
                jnp.concatenate(outs, axis=1).astype(o_ref.dtype))
        return carry

    lax.fori_loop(0, rows_per_blk // NA_ROWS_PER_ITER, rows_body, 0)


def _na_bias_table(rpb):
    heads = rpb.shape[0]
    qc = jnp.arange(GRID_W)
    kc = jnp.arange(GRID_W)
    qc0 = jnp.clip(qc - NA_WIN_C // 2, 0, GRID_W - NA_WIN_C)
    col_in = (kc[None, :] >= qc0[:, None]) & (kc[None, :] < qc0[:, None] + NA_WIN_C)
    pad = GRID_W - NA_WIN_C
    rp = jnp.pad(rpb.astype(F32), ((0, 0), (0, 0), (pad, pad)))
    toep = jnp.stack([rp[:, :, GRID_W - 1 - q:2 * GRID_W - 1 - q] for q in range(GRID_W)], axis=2)
    toep = jnp.where(col_in[None, None], toep, NEG_BIG)
    tab = jnp.stack([toep[:, c:c + NA_WIN_R] for c in range(NA_WIN_R)], axis=0)
    tab = tab.transpose(0, 1, 3, 2, 4) * LOG2E
    return tab.reshape(NA_WIN_R, heads // 2, 2 * GRID_W, NA_WIN_R * GRID_W)


def _neighborhood_attention(q, k, v, kc, vc, bias):
    bsz, s, w = q.shape
    n_rows = s // GRID_W
    rpb_rows = NA_WIN_R
    blk = rpb_rows * GRID_W
    nb = s // blk
    lc = kc.shape[1]
    cur = pl.BlockSpec((None, blk, w), lambda b, i: (b, i, 0))
    prev = pl.BlockSpec((None, blk, w), lambda b, i: (b, jnp.maximum(i - 1, 0), 0))
    nxt = pl.BlockSpec((None, blk, w), lambda b, i: (b, jnp.minimum(i + 1, nb - 1), 0))
    ctx = pl.BlockSpec((None, lc, w), lambda b, i: (b, 0, 0))
    return pl.pallas_call(
        functools.partial(_na_kernel, rows_per_blk=rpb_rows, n_rows=n_rows),
        grid=(bsz, nb),
        in_specs=[cur, prev, cur, nxt, prev, cur, nxt, ctx, ctx, _const_spec(bias.shape)],
        out_specs=cur,
        out_shape=jax.ShapeDtypeStruct(q.shape, BF16),
        scratch_shapes=[pltpu.VMEM((3 * blk, w), BF16), pltpu.VMEM((3 * blk, 2 * w), BF16),
                        pltpu.VMEM((lc, 2 * w), BF16)],
        compiler_params=_cp("parallel", "parallel"),
        name="neighborhood_attention",
    )(q, k, k, k, v, v, v, kc, vc, bias)


def _ctx_attn_kernel(q_ref, k_ref, v_ref, o_ref):
    n_pairs = q_ref.shape[-1] // LANES
    outs = []
    for hp in range(n_pairs):
        sl = slice(hp * LANES, (hp + 1) * LANES)
        qp, kp, vp = q_ref[:, sl], k_ref[:, sl], v_ref[:, sl]
        o_pair = None
        for t, mask in enumerate(_half_masks(qp.shape)):
            qm = jnp.where(mask, qp, jnp.zeros_like(qp))
            vm = jnp.where(mask, vp, jnp.zeros_like(vp))
            o_t = _softmax_pv([_dot_nt(qm, kp)], [vm])
            o_pair = o_t if o_pair is None else o_pair + o_t
        outs.append(o_pair)
    o_ref[...] = jnp.concatenate(outs, axis=1).astype(o_ref.dtype)


def _ctx_attention(q, k, v):
    bsz, lc, w = q.shape
    spec = pl.BlockSpec((None, lc, w), lambda b: (b, 0, 0))
    return pl.pallas_call(
        _ctx_attn_kernel, grid=(bsz,), in_specs=[spec, spec, spec], out_specs=spec,
        out_shape=jax.ShapeDtypeStruct(q.shape, BF16),
        compiler_params=_cp("parallel"), name="ctx_attention",
    )(q, k, v)


class _SsdUnit:
    def __init__(self, xbc_ref, dtc_ref, dtr_ref, pc_ref, pr_ref, y_ref, d, reverse):
        self.xbc_ref, self.dtc_ref, self.dtr_ref, self.pc_ref, self.pr_ref = xbc_ref, dtc_ref, dtr_ref, pc_ref, pr_ref
        self.y_ref, self.d, self.reverse = y_ref, d, reverse


def _ssd_stage_decay(u):
    q = u.xbc_ref.shape[0]
    n_heads = SSD_HEADS
    ii = lax.broadcasted_iota(jnp.int32, (q, q), 0)
    jj = lax.broadcasted_iota(jnp.int32, (q, q), 1)
    u.keep = (ii <= jj) if u.reverse else (ii >= jj)
    tri = u.keep.astype(F32)
    d = u.d
    dt_c = _softplus(u.dtc_ref[:, d * n_heads:(d + 1) * n_heads] + u.pc_ref[0:1, :])
    u.dt_r = _softplus(u.dtr_ref[d * n_heads:(d + 1) * n_heads, :] + u.pr_ref[:, 0:1])
    da_c = dt_c * u.pc_ref[1:2, :]
    da_r = u.dt_r * u.pr_ref[:, 1:2]
    u.acs_c = jnp.dot(tri, da_c, preferred_element_type=F32, precision=HIGHEST)
    u.acs_r = lax.dot_general(da_r, tri, (((1,), (1,)), ((), ())), preferred_element_type=F32,
                              precision=HIGHEST)
    edge = 0 if u.reverse else q - 1
    tot_r = u.acs_r[:, edge:edge + 1]
    u.w_end_r = jnp.exp(tot_r - u.acs_r) * u.dt_r
    u.e_tot_r = jnp.exp(tot_r)


def _ssd_stage_cb(u):
    gw = SSD_HEADS * HEAD_DIM
    u.bm_t, u.cm16, u.cb = [], [], []
    for g in range(2):
        bm = u.xbc_ref[:, gw + g * SSD_STATE:gw + (g + 1) * SSD_STATE]
        cm = u.xbc_ref[:, gw + 2 * SSD_STATE + g * SSD_STATE:gw + 2 * SSD_STATE + (g + 1) * SSD_STATE]
        u.bm_t.append(bm.T)
        u.cm16.append(cm.astype(BF16))
        u.cb.append(_dot(u.cm16[g], u.bm_t[g].astype(BF16)))


def _ssd_stage_local(u):
    q = u.xbc_ref.shape[0]
    lane_lo, lane_hi = _half_masks((q, LANES))
    u.y_diag, u.st_new, u.e_in, u.e_tot = [], [], [], []
    for pair in range(SSD_HEADS // 2):
        g = pair // 2
        h0 = 2 * pair
        xs = u.xbc_ref[:, pair * LANES:(pair + 1) * LANES]
        y_pair, st_new, e_in = None, None, []
        for t, lmask in enumerate((lane_lo, lane_hi)):
            h = h0 + t
            a_bc = jnp.broadcast_to(u.acs_c[:, h:h + 1], (q, q))
            seg = a_bc - u.acs_r[h:h + 1, :]
            dec = jnp.where(u.keep, jnp.exp(jnp.where(u.keep, seg, 0.0)), 0.0) * u.dt_r[h:h + 1, :]
            xm = jnp.where(lmask, xs, 0.0).astype(BF16)
            yd = _dot((u.cb[g] * dec).astype(BF16), xm)
            sn = _dot((u.bm_t[g] * u.w_end_r[h:h + 1, :]).astype(BF16), xm)
            y_pair = yd if y_pair is None else y_pair + yd
            st_new = sn if st_new is None else st_new + sn
            e_in.append(jnp.exp(a_bc))
        u.y_diag.append(y_pair)
        u.st_new.append(st_new)
        u.e_in.append(jnp.where(lane_lo, e_in[0], e_in[1]))
        u.e_tot.append(jnp.where(lane_lo[0:1], u.e_tot_r[h0:h0 + 1, :], u.e_tot_r[h0 + 1:h0 + 2, :]))


def _ssd_stage_state(u, state, dl_ref):
    new_state = []
    for pair in range(SSD_HEADS // 2):
        sl = slice(pair * LANES, (pair + 1) * LANES)
        y_pair = u.y_diag[pair] + _dot(u.cm16[pair // 2], state[pair].astype(BF16)) * u.e_in[pair]
        if dl_ref is not None:
            y_pair = y_pair + dl_ref[:, sl] * u.xbc_ref[:, sl]
        u.y_ref[:, sl] = y_pair
        new_state.append(state[pair] * u.e_tot[pair] + u.st_new[pair])
    return new_state


def _ssd_kernel(xf_ref, xb_ref, dtcf_ref, dtcb_ref, dtrf_ref, dtrb_ref, pc_ref, pr_ref, dl_ref, h0f_ref, h0b_ref,
                yf_ref, yb_ref, hf_ref, hb_ref, sf, sb):
    c = pl.program_id(1)

    @pl.when(c == 0)
    def _():
        sf[...] = h0f_ref[...]
        sb[...] = h0b_ref[...]

    q = SSD_CHUNK
    n_sub = xf_ref.shape[0] // q
    n_pairs = sf.shape[1] // LANES
    fwd, bwd = [], []
    for s in range(n_sub):
        f = slice(s * q, (s + 1) * q)
        r = slice((n_sub - 1 - s) * q, (n_sub - s) * q)
        fwd.append(_SsdUnit(xf_ref.at[f], dtcf_ref.at[f], dtrf_ref.at[:, f], pc_ref.at[0], pr_ref.at[0],
                            yf_ref.at[f], 0, False))
        bwd.append(_SsdUnit(xb_ref.at[r], dtcb_ref.at[r], dtrb_ref.at[:, r], pc_ref.at[1], pr_ref.at[1],
                            yb_ref.at[r], 1, True))
    units = [u for pair in zip(fwd, bwd) for u in pair]
    for stage in (_ssd_stage_decay, _ssd_stage_cb, _ssd_stage_local):
        for u in units:
            stage(u)
    st_f = [sf[:, p * LANES:(p + 1) * LANES] for p in range(n_pairs)]
    st_b = [sb[:, p * LANES:(p + 1) * LANES] for p in range(n_pairs)]
    for uf, ub in zip(fwd, bwd):
        st_f = _ssd_stage_state(uf, st_f, dl_ref)
        st_b = _ssd_stage_state(ub, st_b, None)
    for p in range(n_pairs):
        sf[:, p * LANES:(p + 1) * LANES] = st_f[p]
        sb[:, p * LANES:(p + 1) * LANES] = st_b[p]
    hf_ref[...] = sf[...]
    hb_ref[...] = sb[...]


def _ssd(xbc, dt, pc, pr, dl, h0f, h0b):
    bsz, length, cw = xbc.shape
    assert SSD_CHUNK == LANES, "the per-head decay matrix is built as one (chunk, 128-lane) tile"
    n_chunks = length // SSD_CHUNK
    per_step = next(c for c in range(SSD_CHUNKS_PER_STEP, 0, -1) if n_chunks % c == 0)
    q = SSD_CHUNK * per_step
    nc = length // q
    gw = SSD_HEADS * HEAD_DIM
    dtw = dt.shape[-1]
    dt_t = jnp.swapaxes(dt, 1, 2)
    fwd3 = lambda b, c: (b, c, 0)
    bwd3 = lambda b, c: (b, nc - 1 - c, 0)
    st_spec = pl.BlockSpec((None, SSD_STATE, gw), lambda b, c: (b, 0, 0))
    return pl.pallas_call(
        _ssd_kernel,
        grid=(bsz, nc),
        in_specs=[pl.BlockSpec((None, q, cw), fwd3), pl.BlockSpec((None, q, cw), bwd3),
                  pl.BlockSpec((None, q, dtw), fwd3), pl.BlockSpec((None, q, dtw), bwd3),
                  pl.BlockSpec((None, dtw, q), lambda b, c: (b, 0, c)),
                  pl.BlockSpec((None, dtw, q), lambda b, c: (b, 0, nc - 1 - c)),
                  _const_spec(pc.shape), _const_spec(pr.shape), _const_spec(dl.shape), st_spec, st_spec],
        out_specs=[pl.BlockSpec((None, q, gw), fwd3), pl.BlockSpec((None, q, gw), bwd3), st_spec, st_spec],
        out_shape=[jax.ShapeDtypeStruct((bsz, length, gw), F32)] * 2
                  + [jax.ShapeDtypeStruct((bsz, SSD_STATE, gw), F32)] * 2,
        scratch_shapes=[pltpu.VMEM((SSD_STATE, gw), F32)] * 2,
        compiler_params=_cp("parallel", "arbitrary"),
        name="ssd_scan",
    )(xbc, xbc, dt, dt, dt_t, dt_t, pc, pr, dl, h0f, h0b)


def _gelu_tanh(x):
    return 0.5 * x * (1.0 + jnp.tanh(math.sqrt(2.0 / math.pi) * (x + 0.044715 * (x * x * x))))


def _mix_ab(ona_ref, yf_ref, yb_ref, z_ref, ng_ref, w_ref):
    gw = ona_ref.shape[-1]
    y = (yf_ref[...] + yb_ref[...]) * _silu(z_ref[...])
    gated = y * lax.rsqrt(jnp.mean(y * y, axis=-1, keepdims=True) + EPS) * ng_ref[...]
    return _dot(ona_ref[...], w_ref[0:gw, :]) + _dot(gated.astype(BF16), w_ref[gw:2 * gw, :])


def _mix_cd(gate_ref, hf_ref, hb_ref, od_ref, w_ref):
    gw = od_ref.shape[-1]
    lru = _gelu_tanh(gate_ref[...]) * (hf_ref[...] + hb_ref[...])
    return _dot(lru.astype(BF16), w_ref[0:gw, :]) + _dot(od_ref[...], w_ref[gw:2 * gw, :])


def _mix_ffn_kernel(x_ref, mod_ref, *rest, mix, n_mix, k, chunk):
    g_ref, w1_ref, w3_ref, w2_ref, o_ref = rest[n_mix:]
    x = x_ref[...] + mod_ref[5:6, :] * mix(*rest[:n_mix])
    o_ref[...] = _ffn_body(x, mod_ref, g_ref, w1_ref, w3_ref, w2_ref, k, chunk)


def _mixer_out_ffn(mix, name, x, mod, parts, consts, g, w1, w3, w2, k):
    bsz, length, d = x.shape
    if mod.shape[0] == 1 and bsz > 1:
        flat = lambda a: a.reshape(1, bsz * length, a.shape[-1])
        return _mixer_out_ffn(mix, name, flat(x), mod, [flat(a) for a in parts], consts, g, w1, w3, w2,
                              k).reshape(x.shape)
    ff = w1.shape[1]
    tm = min(512, length)
    tok = lambda a: pl.BlockSpec((None, tm, a.shape[-1]), lambda b, i: (b, i, 0))
    return pl.pallas_call(
        functools.partial(_mix_ffn_kernel, mix=mix, n_mix=len(parts) + len(consts), k=k, chunk=FFN_CHUNK),
        grid=(bsz, length // tm),
        in_specs=[tok(x), _mod_spec(mod)] + [tok(a) for a in parts] + [_const_spec(a.shape) for a in consts]
                 + [_const_spec((1, d)), _const_spec((d, ff)), _const_spec((d, ff)), _const_spec((ff, d))],
        out_specs=tok(x),
        out_shape=jax.ShapeDtypeStruct(x.shape, F32),
        compiler_params=pltpu.CompilerParams(dimension_semantics=("parallel", "parallel"),
                                             vmem_limit_bytes=DIFF_VMEM_LIMIT),
        name=name,
    )(x, mod, *parts, *consts, g.reshape(1, d), w1, w3, w2)


def _lru_direction(x_ref, wa_ref, wx_ref, p_ref, carry, h_ref, *, reverse):
    t = x_ref.shape[0]
    x = x_ref[...]
    x16 = x.astype(BF16)
    r = _sigmoid(_dot(x16, wa_ref[...]) + p_ref[0:1, :])
    ig = _sigmoid(_dot(x16, wx_ref[...]) + p_ref[1:2, :])
    log_a = -LRU_C * r * _softplus(-p_ref[2:3, :])
    a = jnp.exp(log_a)
    b = jnp.sqrt(1.0 - a * a) * (ig * x)
    n_groups = t // SUBLANES
    a = a.reshape(n_groups, SUBLANES, a.shape[-1])
    b = b.reshape(n_groups, SUBLANES, b.shape[-1])
    row = lax.broadcasted_iota(jnp.int32, a.shape, 1)
    s = 1
    while s < SUBLANES:
        fill = (row >= SUBLANES - s) if reverse else (row < s)
        shift = SUBLANES - s if reverse else s
        a_sh = jnp.where(fill, 1.0, pltpu.roll(a, shift, axis=1))
        b_sh = jnp.where(fill, 0.0, pltpu.roll(b, shift, axis=1))
        b = a * b_sh + b
        a = a * a_sh
        s *= 2
    h_prev = carry[...]
    for gi in (range(n_groups - 1, -1, -1) if reverse else range(n_groups)):
        r0 = gi * SUBLANES
        hg = a[gi] * h_prev + b[gi]
        h_ref[r0:r0 + SUBLANES, :] = hg
        h_prev = hg[0:1] if reverse else hg[SUBLANES - 1:SUBLANES]
    carry[...] = h_prev


def _lru_kernel(xf_ref, xb_ref, wa_ref, wx_ref, p_ref, h0f_ref, h0b_ref, hf_ref, hb_ref, lf_ref, lb_ref, cf, cb):
    c = pl.program_id(1)

    @pl.when(c == 0)
    def _():
        cf[...] = h0f_ref[...]
        cb[...] = h0b_ref[...]

    _lru_direction(xf_ref, wa_ref.at[0], wx_ref.at[0], p_ref.at[0], cf, hf_ref, reverse=False)
    _lru_direction(xb_ref, wa_ref.at[1], wx_ref.at[1], p_ref.at[1], cb, hb_ref, reverse=True)
    lf_ref[...] = cf[...]
    lb_ref[...] = cb[...]


def _lru(x, wa, wx, p, h0f, h0b):
    bsz, length, w = x.shape
    t = min(512, length)
    nt = length // t
    fwd = lambda b, c: (b, c, 0)
    bwd = lambda b, c: (b, nt - 1 - c, 0)
    st = pl.BlockSpec((None, 1, w), lambda b, c: (b, 0, 0))
    return pl.pallas_call(
        _lru_kernel,
        grid=(bsz, nt),
        in_specs=[pl.BlockSpec((None, t, w), fwd), pl.BlockSpec((None, t, w), bwd),
                  _const_spec(wa.shape), _const_spec(wx.shape), _const_spec(p.shape), st, st],
        out_specs=[pl.BlockSpec((None, t, w), fwd), pl.BlockSpec((None, t, w), bwd), st, st],
        out_shape=[jax.ShapeDtypeStruct(x.shape, F32)] * 2 + [jax.ShapeDtypeStruct((bsz, 1, w), F32)] * 2,
        scratch_shapes=[pltpu.VMEM((1, w), F32)] * 2,
        compiler_params=_cp("parallel", "arbitrary"),
        name="rglru_scan",
    )(x, x, wa, wx, p, h0f, h0b)


def _block_diag(wb):
    nb, bs, _ = wb.shape
    eye = jnp.eye(nb, dtype=wb.dtype)
    return (wb[:, :, None, :] * eye[:, None, :, None]).reshape(nb * bs, nb * bs)


def _diff_attn_kernel(q_ref, kt_ref, v_ref, lam_ref, sg_ref, o_ref, qs, m_s, acc_s, s_buf, *, lam_init, n_heads, tk):
    tq = q_ref.shape[0]
    vw = 2 * LANES
    for h in range(n_heads):
        qp = q_ref[:, h * LANES:(h + 1) * LANES]
        lo, hi = _half_masks(qp.shape)
        qs[h, 0:tq, :] = jnp.where(lo, qp, jnp.zeros_like(qp))
        qs[h, tq:2 * tq, :] = jnp.where(hi, qp, jnp.zeros_like(qp))
    m_s[...] = jnp.full(m_s.shape, NEG_BIG, F32)
    acc_s[...] = jnp.zeros(acc_s.shape, F32)

    n_chunks = kt_ref.shape[1] // tk

    def scores(h, k0):
        return _dot(qs[h], kt_ref[h * LANES:(h + 1) * LANES, pl.ds(k0, tk)])

    s_buf[0] = scores(0, 0)

    def chunk(c, carry):
        k0 = pl.multiple_of(c * tk, tk)
        k_next = pl.multiple_of(jnp.minimum(c + 1, n_chunks - 1) * tk, tk)
        for h in range(n_heads):
            if h + 1 < n_heads:
                s_buf[(h + 1) % 2] = scores(h + 1, k0)
            else:
                s_buf[0] = scores(0, k_next)
            s = s_buf[h % 2]
            m_old = m_s[h]
            m_new = jnp.maximum(m_old, s.max(axis=-1, keepdims=True))
            alpha = jnp.exp2(m_old - m_new)
            p = jnp.concatenate([jnp.exp2(s[:, j * LANES:(j + 1) * LANES] - m_new).astype(BF16)
                                 for j in range(tk // LANES)], axis=1)
            pv = _dot(p, v_ref[pl.ds(k0, tk), h * vw:(h + 1) * vw])
            acc_s[h] = jnp.concatenate([alpha, alpha], axis=1) * acc_s[h] + pv
            m_s[h] = m_new
        return carry

    lax.fori_loop(0, n_chunks, chunk, 0)

    dl = lam_ref[...]
    lam = (jnp.exp(jnp.sum(dl[0:1] * dl[1:2], axis=-1, keepdims=True))
           - jnp.exp(jnp.sum(dl[2:3] * dl[3:4], axis=-1, keepdims=True)) + lam_init)
    for h in range(n_heads):
        a = acc_s[h]
        o = a[0:tq, 0:LANES] / a[0:tq, LANES:vw] - lam * (a[tq:2 * tq, 0:LANES] / a[tq:2 * tq, LANES:vw])
        o = o * lax.rsqrt(jnp.mean(o * o, axis=-1, keepdims=True) + EPS) * sg_ref[...] * (1.0 - lam_init)
        o_ref[:, h * LANES:(h + 1) * LANES] = o.astype(o_ref.dtype)


def _diff_attention(q, kt_all, v_ext, diff_lam, subln_g, lam_init):
    bsz, s, w = q.shape
    lk = kt_all.shape[2]
    n_heads = w // LANES
    tq = min(DIFF_TQ, s)
    tk = next(c for c in (768, 512, 256, 128) if lk % c == 0)
    return pl.pallas_call(
        functools.partial(_diff_attn_kernel, lam_init=lam_init, n_heads=n_heads, tk=tk),
        grid=(bsz, s // tq),
        in_specs=[pl.BlockSpec((None, tq, w), lambda b, i: (b, i, 0)),
                  pl.BlockSpec((None, w, lk), lambda b, i: (b, 0, 0), pipeline_mode=pl.Buffered(1)),
                  pl.BlockSpec((None, lk, v_ext.shape[2]), lambda b, i: (b, 0, 0), pipeline_mode=pl.Buffered(1)),
                  _const_spec(diff_lam.shape), _const_spec((1, LANES))],
        out_specs=pl.BlockSpec((None, tq, w), lambda b, i: (b, i, 0)),
        out_shape=jax.ShapeDtypeStruct(q.shape, BF16),
        scratch_shapes=[pltpu.VMEM((n_heads, 2 * tq, LANES), BF16), pltpu.VMEM((n_heads, 2 * tq, LANES), F32),
                        pltpu.VMEM((n_heads, 2 * tq, 2 * LANES), F32), pltpu.VMEM((2, 2 * tq, tk), F32)],
        compiler_params=pltpu.CompilerParams(dimension_semantics=("parallel", "parallel"),
                                             vmem_limit_bytes=DIFF_VMEM_LIMIT),
        name="diff_attention",
    )(q, kt_all, v_ext, diff_lam.astype(F32), subln_g.astype(F32).reshape(1, LANES))


def _pad_cols(w, total):
    return jnp.pad(w, ((0, 0), (0, total - w.shape[1])))


def _layer_ab(x, xc, m, mc, g_mix, w_in, w_out, q_g, k_g, rpb, conv_w, conv_b, dt_bias, a_log, d_skip, norm_g,
              ffn2):
    gw = SSD_HEADS * HEAD_DIM
    segs = [_Seg(gw, 'norm', gain=q_g, scale=HEAD_DIM ** -0.5 * LOG2E), _Seg(gw, 'norm', gain=k_g),
            _Seg(gw, dtype=BF16), _Seg(gw), _Seg(2 * gw, 'conv', conv_w=conv_w, conv_b=conv_b, act=True),
            _Seg(LANES)]
    w16 = _pad_cols(w_in, sum(s.width for s in segs)).astype(BF16)
    pmat = _group_mean_matrix(gw, HEAD_DIM)
    q, k, v, z, xbc, dt = _norm_proj(x, m, g_mix, w16, 1, segs, pmat)
    q_c, k_c, v_c, z_c, xbc_c, dt_c = _norm_proj(xc, mc, g_mix, w16, 1, segs, pmat)
    o_na = _neighborhood_attention(q, k, v, k_c, v_c, _na_bias_table(rpb))
    o_c = _ctx_attention(q_c, k_c, v_c)

    a_neg = -jnp.exp(a_log.astype(F32))
    pc = jnp.stack([dt_bias.astype(F32), a_neg], axis=1)
    pr = jnp.swapaxes(pc, 1, 2)
    dl = jnp.repeat(d_skip.astype(F32), HEAD_DIM).reshape(1, gw)
    zeros = jnp.zeros((x.shape[0], SSD_STATE, gw), F32)
    yf_c, yb_c, hf_c, hb_c = _ssd(xbc_c, dt_c, pc, pr, dl, zeros, zeros)
    yf, yb, _, _ = _ssd(xbc, dt, pc, pr, dl, hf_c, hb_c)

    ng = norm_g.astype(F32).reshape(1, gw)
    wo16 = w_out.astype(BF16)
    x = _mixer_out_ffn(_mix_ab, "mixer_out_ab_ffn", x, m, [o_na, yf, yb, z], [ng, wo16], *ffn2, 2)
    xc = _mixer_out_ffn(_mix_ab, "mixer_out_ab_ffn", xc, mc, [o_c, yf_c, yb_c, z_c], [ng, wo16], *ffn2, 2)
    return x, xc


def _layer_cd(x, xc, m, mc, g_mix, w_in, w_out, conv_w, conv_b, wa, ba, wx, bx, lam_p, q_g, k_g, diff_lam,
              subln_g, lam_init, ffn2):
    gw = w_out.shape[0] // 2
    w16 = w_in.astype(BF16)
    pmat = _group_mean_matrix(gw, HEAD_DIM)
    tabs = _rope_tables(x.shape[1])
    q_scale = HEAD_DIM ** -0.5 * LOG2E

    def segs(rope):
        return [_Seg(gw), _Seg(gw, 'conv', conv_w=conv_w, conv_b=conv_b),
                _Seg(gw, 'norm', gain=q_g, scale=q_scale, rope=rope),
                _Seg(gw, 'norm_t', gain=k_g, rope=rope), _Seg(gw, 'vext')]

    gate, xr, q, kt, v_ext = _norm_proj(x, m, g_mix, w16, 1, segs(True), pmat, tabs)
    _, xr_c, _, kt_c, v_ext_c = _norm_proj(xc, mc, g_mix, w16, 1, segs(False), pmat)

    wa_d = jnp.stack([_block_diag(wa[0]), _block_diag(wa[1])]).astype(BF16)
    wx_d = jnp.stack([_block_diag(wx[0]), _block_diag(wx[1])]).astype(BF16)
    p = jnp.stack([ba.astype(F32), bx.astype(F32), lam_p.astype(F32)], axis=1)
    zeros = jnp.zeros((x.shape[0], 1, gw), F32)
    _, _, lf_c, lb_c = _lru(xr_c, wa_d, wx_d, p, zeros, zeros)
    hf, hb, _, _ = _lru(xr, wa_d, wx_d, p, lf_c, lb_c)

    kt_all = jnp.concatenate([kt_c, kt], axis=2)
    v_all = jnp.concatenate([v_ext_c, v_ext], axis=1)
    o = _diff_attention(q, kt_all, v_all, diff_lam, subln_g, lam_init)
    return _mixer_out_ffn(_mix_cd, "mixer_out_cd_ffn", x, m, [gate, hf, hb, o], [w_out.astype(BF16)], *ffn2, 2)


def kernel(x, c, ctx, c_ctx, w_mod, b_mod, norm_g, ffn_w1, ffn_w3, ffn_w2, ab_w_in, ab_w_out, na_q_g, na_k_g, na_rpb, ssd_conv_w, ssd_conv_b, ssd_dt_bias, ssd_a_log, ssd_d, ssd_norm_g, cd_w_in, cd_w_out, lru_conv_w, lru_conv_b, lru_wa, lru_ba, lru_wx, lru_bx, lru_lambda, diff_q_g, diff_k_g, diff_lambda, diff_subln_g):
    bsz, _, d = x.shape
    depth = w_mod.shape[0]
    cc = jnp.concatenate([c.astype(F32), c_ctx.astype(F32)[None], jnp.zeros((8 - bsz - 1, d), F32)], axis=0)
    mods = _modulation(cc, w_mod.astype(F32), b_mod.astype(F32))
    xc = ctx
    for i in range(depth):
        last = i == depth - 1
        j = i // 2
        m = mods[i, :bsz].reshape(bsz, N_MOD, d)
        mc = mods[i, bsz:bsz + 1].reshape(1, N_MOD, d)
        g = norm_g[i].astype(F32)
        ffn1 = (g[0], ffn_w1[i, 0].astype(BF16), ffn_w3[i, 0].astype(BF16), ffn_w2[i, 0].astype(BF16))
        ffn2 = (g[2], ffn_w1[i, 1].astype(BF16), ffn_w3[i, 1].astype(BF16), ffn_w2[i, 1].astype(BF16))
        x = _ffn(x, m, *ffn1, 0)
        xc = _ffn(xc, mc, *ffn1, 0)
        if i % 2 == 0:
            x, xc = _layer_ab(x, xc, m, mc, g[1], ab_w_in[j], ab_w_out[j], na_q_g[j], na_k_g[j], na_rpb[j],
                              ssd_conv_w[j], ssd_conv_b[j], ssd_dt_bias[j], ssd_a_log[j], ssd_d[j], ssd_norm_g[j],
                              ffn2)
        else:
            assert last, "a C|D layer that is not the last one would also need the context stream's mixer output"
            lam_init = 0.8 - 0.6 * math.exp(-0.3 * i)
            x = _layer_cd(x, xc, m, mc, g[1], cd_w_in[j], cd_w_out[j], lru_conv_w[j], lru_conv_b[j], lru_wa[j],
                          lru_ba[j], lru_wx[j], lru_bx[j], lru_lambda[j], diff_q_g[j], diff_k_g[j],
                          diff_lambda[j], diff_subln_g[j], lam_init, ffn2)
    return x
```

```python
import functools
import math

import jax
import jax.numpy as jnp
from jax import lax
from jax.experimental import pallas as pl
from jax.experimental.pallas import tpu as pltpu

F32 = jnp.float32
BF16 = jnp.bfloat16
HIGHEST = lax.Precision.HIGHEST

GRID_W = 64
EPS = 1e-6
HEAD_DIM = 64
N_MOD = 9
NA_WIN_R = 8
NA_WIN_C = 16
SSD_STATE = 128
SSD_CHUNK = 128
SSD_HEADS = 8
LRU_C = 8.0
ROPE_BASE = 10000.0
NEG_BIG = -1e30

LANES = 128
SUBLANES = 8
VMEM_LIMIT = 48 * 1024 * 1024
DIFF_VMEM_LIMIT = 56 * 1024 * 1024
DIFF_TQ = 512
DIFF_LOOKAHEAD = 1
DIFF_SCORE_BUFS = 2
FFN_ROW_BLOCKS = 2
NA_ROWS_PER_ITER = 4
SSD_CHUNKS_PER_STEP = 4
MXU_TILE = 256
FFN_CHUNK = 6 * MXU_TILE
LOG2E = math.log2(math.e)


def _cp(*sem):
    return pltpu.CompilerParams(dimension_semantics=sem, vmem_limit_bytes=VMEM_LIMIT)


def _const_spec(shape):
    nd = len(shape)
    return pl.BlockSpec(shape, lambda *_: (0,) * nd, pipeline_mode=pl.Buffered(1))


def _dot(a, b):
    return jnp.dot(a, b, preferred_element_type=F32)


def _dot_nt(a, b):
    return lax.dot_general(a, b, (((1,), (1,)), ((), ())), preferred_element_type=F32)


def _sigmoid(x):
    return 0.5 * jnp.tanh(0.5 * x) + 0.5


def _silu(x):
    return x * _sigmoid(x)


def _softplus(x):
    return jnp.maximum(x, 0.0) + jnp.log(1.0 + jnp.exp(-jnp.abs(x)))


def _rms_mod(x, g, shift, scale):
    ms = jnp.mean(x * x, axis=-1, keepdims=True)
    return (x * lax.rsqrt(ms + EPS) * g) * (1.0 + scale) + shift


def _mod_kernel(c_ref, w_ref, b_ref, o_ref):
    s = _silu(c_ref[...])
    o_ref[...] = jnp.dot(s, w_ref[...], preferred_element_type=F32, precision=HIGHEST) + b_ref[...]


def _modulation(cc, w_mod, b_mod):
    depth, d, n = w_mod.shape
    tn = 1024
    return pl.pallas_call(
        _mod_kernel,
        grid=(depth, n // tn),
        in_specs=[pl.BlockSpec((8, d), lambda l, j: (0, 0)),
                  pl.BlockSpec((None, d, tn), lambda l, j: (l, 0, j)),
                  pl.BlockSpec((None, 1, tn), lambda l, j: (l, 0, j))],
        out_specs=pl.BlockSpec((None, 8, tn), lambda l, j: (l, 0, j)),
        out_shape=jax.ShapeDtypeStruct((depth, 8, n), F32),
        compiler_params=_cp("parallel", "parallel"),
        name="modulation",
    )(cc, w_mod, b_mod.reshape(depth, 1, n))


def _mod_spec(mod):
    if mod.shape[0] == 1:
        return pl.BlockSpec((None, N_MOD, mod.shape[2]), lambda b, i: (0, 0, 0))
    return pl.BlockSpec((None, N_MOD, mod.shape[2]), lambda b, i: (b, 0, 0))


def _ffn_body(x, mod_ref, g_ref, w1_ref, w3_ref, w2_ref, k, chunk):
    tm = x.shape[0]
    rb = tm // FFN_ROW_BLOCKS if tm % (FFN_ROW_BLOCKS * SUBLANES) == 0 else tm
    ff = w1_ref.shape[1]
    outs = []
    for r0 in range(0, tm, rb):
        xr = x[r0:r0 + rb]
        h = _rms_mod(xr, g_ref[...], mod_ref[3 * k:3 * k + 1, :], mod_ref[3 * k + 1:3 * k + 2, :]).astype(BF16)
        acc = jnp.zeros(xr.shape, F32)
        for c0 in range(0, ff, chunk):
            c1 = min(c0 + chunk, ff)
            a = _dot(h, w1_ref[:, c0:c1])
            b = _dot(h, w3_ref[:, c0:c1])
            acc = acc + _dot((_silu(a) * b).astype(BF16), w2_ref[c0:c1, :])
        outs.append(xr + (0.5 * mod_ref[3 * k + 2:3 * k + 3, :]) * acc)
    return outs[0] if len(outs) == 1 else jnp.concatenate(outs, axis=0)


def _ffn_kernel(x_ref, mod_ref, g_ref, w1_ref, w3_ref, w2_ref, o_ref, *, k, chunk):
    o_ref[...] = _ffn_body(x_ref[...], mod_ref, g_ref, w1_ref, w3_ref, w2_ref, k, chunk)


def _ffn(x, mod, g, w1, w3, w2, k):
    bsz, length, d = x.shape
    if mod.shape[0] == 1 and bsz > 1:
        return _ffn(x.reshape(1, bsz * length, d), mod, g, w1, w3, w2, k).reshape(x.shape)
    ff = w1.shape[1]
    tm = min(512, length)
    return pl.pallas_call(
        functools.partial(_ffn_kernel, k=k, chunk=FFN_CHUNK),
        grid=(bsz, length // tm),
        in_specs=[pl.BlockSpec((None, tm, d), lambda b, i: (b, i, 0)),
                  _mod_spec(mod),
                  _const_spec((1, d)),
                  _const_spec((d, ff)), _const_spec((d, ff)), _const_spec((ff, d))],
        out_specs=pl.BlockSpec((None, tm, d), lambda b, i: (b, i, 0)),
        out_shape=jax.ShapeDtypeStruct(x.shape, F32),
        compiler_params=_cp("parallel", "parallel"),
        name="half_ffn",
    )(x, mod, g.reshape(1, d), w1, w3, w2)


class _Seg:
    def __init__(self, width, kind='plain', dtype=F32, gain=None, scale=1.0, rope=False, conv_w=None, conv_b=None,
                 act=False):
        self.width, self.kind, self.dtype, self.gain, self.scale, self.rope = width, kind, dtype, gain, scale, rope
        self.conv_w, self.conv_b, self.act = conv_w, conv_b, act


def _head_norm(y, gain, pmat, scale, rope_tabs):
    ms = _dot((y * y).astype(BF16), pmat)
    y = y * lax.rsqrt(ms + EPS) * gain
    if rope_tabs is not None:
        reps = y.shape[-1] // rope_tabs[0].shape[-1]
        cs = jnp.concatenate([rope_tabs[0]] * reps, axis=1)
        sn = jnp.concatenate([rope_tabs[1]] * reps, axis=1)
        y = y * cs + _swap_pairs(y) * sn
    return y * scale


def _conv_taps(xx, w_ref, b_ref, t, halo, act):
    taps = w_ref.shape[0]
    left = taps // 2
    tot = t + 2 * halo
    y = b_ref[...]
    for j in range(taps):
        sh = (left - j) % tot
        xs = xx if sh == 0 else pltpu.roll(xx, sh, axis=0)
        y = y + w_ref[j:j + 1, :] * xs[halo:halo + t]
    return _silu(y) if act else y


def _norm_proj_kernel(x_ref, mod_ref, g_ref, w_ref, *rest, k, segs, n_gain, has_rope, n_conv):
    pos = 0
    pmat = None
    if n_gain:
        pmat = rest[0][...]
        pos = 1
    gains = rest[pos:pos + n_gain]
    pos += n_gain
    rope_tabs = None
    if has_rope:
        rope_tabs = (rest[pos][...], rest[pos + 1][...])
        pos += 2
    shift, scale = mod_ref[3 * k:3 * k + 1, :], mod_ref[3 * k + 1:3 * k + 2, :]
    h_halo = None
    if n_conv:
        halo_refs = rest[pos:pos + 2]
        conv_refs = rest[pos + 2:pos + 2 + 2 * n_conv]
        pos += 2 + 2 * n_conv
        h_halo = [_rms_mod(r[...], g_ref[...], shift, scale).astype(BF16) for r in halo_refs]
        i, n = pl.program_id(1), pl.num_programs(1)
    o_refs = rest[pos:]
    h = _rms_mod(x_ref[...], g_ref[...], shift, scale).astype(BF16)
    offs = [sum(s.width for s in segs[:j]) for j in range(len(segs))]
    gain_of = {j: gains[n] for n, j in enumerate(j for j, s in enumerate(segs) if s.kind in ('norm', 'norm_t'))}
    conv_of = {j: n for n, j in enumerate(j for j, s in enumerate(segs) if s.kind == 'conv')}
    rank = {'conv': 0, 'norm': 1, 'norm_t': 1}
    for j in sorted(range(len(segs)), key=lambda j: rank.get(segs[j].kind, 2)):
        seg, o_ref = segs[j], o_refs[j]
        w_seg = w_ref[:, offs[j]:offs[j] + seg.width]
        y = _dot(h, w_seg)
        if seg.kind == 'conv':
            ci = conv_of[j]
            y_prev = jnp.where(i > 0, _dot(h_halo[0], w_seg), 0.0)
            y_next = jnp.where(i < n - 1, _dot(h_halo[1], w_seg), 0.0)
            xx = jnp.concatenate([y_prev, y, y_next], axis=0)
            o_ref[...] = _conv_taps(xx, conv_refs[2 * ci], conv_refs[2 * ci + 1], y.shape[0], y_prev.shape[0],
                                    seg.act).astype(o_ref.dtype)
        elif seg.kind in ('norm', 'norm_t'):
            y = _head_norm(y, gain_of[j][...], pmat, seg.scale, rope_tabs if seg.rope else None)
            o_ref[...] = (y.T if seg.kind == 'norm_t' else y).astype(o_ref.dtype)
        elif seg.kind == 'vext':
            y16 = y.astype(BF16)
            ones = jnp.ones((y.shape[0], LANES), BF16)
            for hp in range(seg.width // LANES):
                o_ref[:, 2 * hp * LANES:(2 * hp + 1) * LANES] = y16[:, hp * LANES:(hp + 1) * LANES]
                o_ref[:, (2 * hp + 1) * LANES:(2 * hp + 2) * LANES] = ones
        else:
            o_ref[...] = y.astype(o_ref.dtype)


def _norm_proj(x, mod, g, w, k, segs, pmat=None, rope_tabs=None):
    bsz, length, d = x.shape
    tm = min(512, length)
    tok = lambda wd: pl.BlockSpec((None, tm, wd), lambda b, i: (b, i, 0))
    in_specs = [tok(d), _mod_spec(mod), _const_spec((1, d)), _const_spec(w.shape)]
    args = [x, mod, g.reshape(1, d), w]
    gains = [s for s in segs if s.kind in ('norm', 'norm_t')]
    if gains:
        in_specs.append(_const_spec(pmat.shape))
        args.append(pmat)
        for s in gains:
            in_specs.append(_const_spec((1, s.width)))
            args.append(jnp.tile(s.gain.astype(F32), s.width // s.gain.shape[0]).reshape(1, s.width))
    has_rope = any(s.rope for s in segs)
    if has_rope:
        in_specs += [pl.BlockSpec((tm, rope_tabs[0].shape[-1]), lambda b, i: (i, 0))] * 2
        args += list(rope_tabs)
    convs = [s for s in segs if s.kind == 'conv']
    if convs:
        per = tm // SUBLANES
        nh = length // SUBLANES
        in_specs += [pl.BlockSpec((None, SUBLANES, d), lambda b, i: (b, jnp.maximum(i * per - 1, 0), 0)),
                     pl.BlockSpec((None, SUBLANES, d), lambda b, i: (b, jnp.minimum((i + 1) * per, nh - 1), 0))]
        args += [x, x]
        for s in convs:
            in_specs += [_const_spec(s.conv_w.shape), _const_spec((1, s.width))]
            args += [s.conv_w.astype(F32), s.conv_b.astype(F32).reshape(1, s.width)]
    out_specs, out_shape = [], []
    for s in segs:
        if s.kind == 'norm_t':
            out_specs.append(pl.BlockSpec((None, s.width, tm), lambda b, i: (b, 0, i)))
            out_shape.append(jax.ShapeDtypeStruct((bsz, s.width, length), BF16))
        elif s.kind == 'vext':
            out_specs.append(tok(2 * s.width))
            out_shape.append(jax.ShapeDtypeStruct((bsz, length, 2 * s.width), BF16))
        else:
            out_specs.append(tok(s.width))
            out_shape.append(jax.ShapeDtypeStruct((bsz, length, s.width), BF16 if s.kind == 'norm' else s.dtype))
    return pl.pallas_call(
        functools.partial(_norm_proj_kernel, k=k, segs=tuple(segs), n_gain=len(gains), has_rope=has_rope,
                          n_conv=len(convs)),
        grid=(bsz, length // tm),
        in_specs=in_specs,
        out_specs=out_specs,
        out_shape=out_shape,
        compiler_params=_cp("parallel", "parallel"),
        name="norm_proj",
    )(*args)


def _swap_pairs(x):
    n = x.shape[-1]
    lane = lax.broadcasted_iota(jnp.int32, x.shape, 1)
    return jnp.where(lane % 2 == 0, pltpu.roll(x, n - 1, axis=1), pltpu.roll(x, 1, axis=1))


def _group_mean_matrix(w, group):
    idx = jnp.arange(w) // group
    return jnp.where(idx[:, None] == idx[None, :], 1.0 / group, 0.0).astype(BF16)


def _rope_tables(length):
    t = jnp.arange(length)
    row = (t // GRID_W).astype(F32)
    col = (t % GRID_W).astype(F32)
    n = HEAD_DIM // 4
    inv = ROPE_BASE ** (-jnp.arange(n, dtype=F32) / n)
    ang = jnp.concatenate([row[:, None] * inv, col[:, None] * inv], axis=-1)
    cos = jnp.repeat(jnp.cos(ang), 2, axis=-1)
    sin = jnp.repeat(jnp.sin(ang), 2, axis=-1)
    sign = jnp.tile(jnp.array([-1.0, 1.0], F32), HEAD_DIM // 2)
    reps = LANES // HEAD_DIM
    return jnp.tile(cos, (1, reps)), jnp.tile(sin * sign, (1, reps))


def _half_masks(shape):
    lane = lax.broadcasted_iota(jnp.int32, shape, len(shape) - 1)
    return lane < HEAD_DIM, lane >= HEAD_DIM


def _softmax_pv(s_list, v_list):
    m = s_list[0].max(axis=-1, keepdims=True)
    for s in s_list[1:]:
        m = jnp.maximum(m, s.max(axis=-1, keepdims=True))
    acc, l = None, None
    for s, v in zip(s_list, v_list):
        p = jnp.exp2(s - m)
        ls = p.sum(axis=-1, keepdims=True)
        o = _dot(p.astype(BF16), v)
        acc = o if acc is None else acc + o
        l = ls if l is None else l + ls
    return acc / l


def _na_kernel(q_ref, kp_ref, kc_ref, kn_ref, vp_ref, vc_ref, vn_ref, kctx_ref, vctx_ref, bias_ref,
               o_ref, kbuf, vbuf, vcbuf, *, rows_per_blk, n_rows):
    i = pl.program_id(1)
    blk = rows_per_blk * GRID_W
    n_pairs = q_ref.shape[-1] // LANES
    vw = 2 * LANES
    kbuf[0:blk, :] = kp_ref[...]
    kbuf[blk:2 * blk, :] = kc_ref[...]
    kbuf[2 * blk:3 * blk, :] = kn_ref[...]
    for hp in range(n_pairs):
        sl = slice(hp * LANES, (hp + 1) * LANES)
        vbuf[0:blk, hp * vw:hp * vw + LANES] = vp_ref[:, sl]
        vbuf[blk:2 * blk, hp * vw:hp * vw + LANES] = vc_ref[:, sl]
        vbuf[2 * blk:3 * blk, hp * vw:hp * vw + LANES] = vn_ref[:, sl]
        vcbuf[:, hp * vw:hp * vw + LANES] = vctx_ref[:, sl]

    @pl.when(i == 0)
    def _():
        for hp in range(n_pairs):
            vbuf[:, hp * vw + LANES:(hp + 1) * vw] = jnp.ones((3 * blk, LANES), BF16)
            vcbuf[:, hp * vw + LANES:(hp + 1) * vw] = jnp.ones((vcbuf.shape[0], LANES), BF16)
    win = NA_WIN_R * GRID_W

    lo, hi = _half_masks((GRID_W, LANES))

    def rows_body(jj, carry):
        rows = [jj * NA_ROWS_PER_ITER + u for u in range(NA_ROWS_PER_ITER)]
        offs, units = [], []
        for j in rows:
            r = i * rows_per_blk + j
            r0 = jnp.clip(r - NA_WIN_R // 2, 0, n_rows - NA_WIN_R)
            off = pl.multiple_of((r0 - (i - 1) * rows_per_blk) * GRID_W, GRID_W)
            cfg = r0 - r + NA_WIN_R - 1
            offs.append(off)
            qrow = q_ref[pl.ds(pl.multiple_of(j * GRID_W, GRID_W), GRID_W), :]
            for hp in range(n_pairs):
                sl = slice(hp * LANES, (hp + 1) * LANES)
                qp = qrow[:, sl]
                zero = jnp.zeros_like(qp)
                qst = jnp.concatenate([jnp.where(lo, qp, zero), jnp.where(hi, qp, zero)], axis=0)
                units.append((_dot_nt(qst, kbuf[pl.ds(off, win), sl]) + bias_ref[cfg, hp],
                              _dot_nt(qst, kctx_ref[:, sl])))
        probs = []
        for s_lat, s_ctx in units:
            m = jnp.maximum(s_lat.max(axis=-1, keepdims=True), s_ctx.max(axis=-1, keepdims=True))
            probs.append((jnp.exp2(s_lat - m).astype(BF16), jnp.exp2(s_ctx - m).astype(BF16)))
        for u, j in enumerate(rows):
            outs = []
            for hp in range(n_pairs):
                p_lat, p_ctx = probs[u * n_pairs + hp]
                ov = (_dot(p_lat, vbuf[pl.ds(offs[u], win), hp * vw:(hp + 1) * vw])
                      + _dot(p_ctx, vcbuf[:, hp * vw:(hp + 1) * vw]))
                o = ov[:, 0:LANES] / ov[:, LANES:vw]
                outs.append(jnp.where(lo, o[0:GRID_W], o[GRID_W:2 * GRID_W]))
            o_ref[pl.ds(pl.multiple_of(j * GRID_W, GRID_W), GRID_W), :] = (
                jnp.concatenate(outs, axis=1).astype(o_ref.dtype))
        return carry

    lax.fori_loop(0, rows_per_blk // NA_ROWS_PER_ITER, rows_body, 0)


def _na_bias_table(rpb):
    heads = rpb.shape[0]
    qc = jnp.arange(GRID_W)
    kc = jnp.arange(GRID_W)
    qc0 = jnp.clip(qc - NA_WIN_C // 2, 0, GRID_W - NA_WIN_C)
    col_in = (kc[None, :] >= qc0[:, None]) & (kc[None, :] < qc0[:, None] + NA_WIN_C)
    pad = GRID_W - NA_WIN_C
    rp = jnp.pad(rpb.astype(F32), ((0, 0), (0, 0), (pad, pad)))
    toep = jnp.stack([rp[:, :, GRID_W - 1 - q:2 * GRID_W - 1 - q] for q in range(GRID_W)], axis=2)
    toep = jnp.where(col_in[None, None], toep, NEG_BIG)
    tab = jnp.stack([toep[:, c:c + NA_WIN_R] for c in range(NA_WIN_R)], axis=0)
    tab = tab.transpose(0, 1, 3, 2, 4) * LOG2E
    return tab.reshape(NA_WIN_R, heads // 2, 2 * GRID_W, NA_WIN_R * GRID_W)


def _neighborhood_attention(q, k, v, kc, vc, bias):
    bsz, s, w = q.shape
    n_rows = s // GRID_W
    rpb_rows = NA_WIN_R
    blk = rpb_rows * GRID_W
    nb = s // blk
    lc = kc.shape[1]
    cur = pl.BlockSpec((None, blk, w), lambda b, i: (b, i, 0))
    prev = pl.BlockSpec((None, blk, w), lambda b, i: (b, jnp.maximum(i - 1, 0), 0))
    nxt = pl.BlockSpec((None, blk, w), lambda b, i: (b, jnp.minimum(i + 1, nb - 1), 0))
    ctx = pl.BlockSpec((None, lc, w), lambda b, i: (b, 0, 0))
    return pl.pallas_call(
        functools.partial(_na_kernel, rows_per_blk=rpb_rows, n_rows=n_rows),
        grid=(bsz, nb),
        in_specs=[cur, prev, cur, nxt, prev, cur, nxt, ctx, ctx, _const_spec(bias.shape)],
        out_specs=cur,
        out_shape=jax.ShapeDtypeStruct(q.shape, BF16),
        scratch_shapes=[pltpu.VMEM((3 * blk, w), BF16), pltpu.VMEM((3 * blk, 2 * w), BF16),
                        pltpu.VMEM((lc, 2 * w), BF16)],
        compiler_params=_cp("parallel", "arbitrary"),
        name="neighborhood_attention",
    )(q, k, k, k, v, v, v, kc, vc, bias)


def _ctx_attn_kernel(q_ref, k_ref, v_ref, o_ref):
    n_pairs = q_ref.shape[-1] // LANES
    outs = []
    for hp in range(n_pairs):
        sl = slice(hp * LANES, (hp + 1) * LANES)
        qp, kp, vp = q_ref[:, sl], k_ref[:, sl], v_ref[:, sl]
        o_pair = None
        for t, mask in enumerate(_half_masks(qp.shape)):
            qm = jnp.where(mask, qp, jnp.zeros_like(qp))
            vm = jnp.where(mask, vp, jnp.zeros_like(vp))
            o_t = _softmax_pv([_dot_nt(qm, kp)], [vm])
            o_pair = o_t if o_pair is None else o_pair + o_t
        outs.append(o_pair)
    o_ref[...] = jnp.concatenate(outs, axis=1).astype(o_ref.dtype)


def _ctx_attention(q, k, v):
    bsz, lc, w = q.shape
    spec = pl.BlockSpec((None, lc, w), lambda b: (b, 0, 0))
    return pl.pallas_call(
        _ctx_attn_kernel, grid=(bsz,), in_specs=[spec, spec, spec], out_specs=spec,
        out_shape=jax.ShapeDtypeStruct(q.shape, BF16),
        compiler_params=_cp("parallel"), name="ctx_attention",
    )(q, k, v)


class _SsdUnit:
    def __init__(self, xbc_ref, dtc_ref, dtr_ref, pc_ref, pr_ref, y_ref, d, reverse):
        self.xbc_ref, self.dtc_ref, self.dtr_ref, self.pc_ref, self.pr_ref = xbc_ref, dtc_ref, dtr_ref, pc_ref, pr_ref
        self.y_ref, self.d, self.reverse = y_ref, d, reverse


def _ssd_stage_decay(u):
    q = u.xbc_ref.shape[0]
    n_heads = SSD_HEADS
    ii = lax.broadcasted_iota(jnp.int32, (q, q), 0)
    jj = lax.broadcasted_iota(jnp.int32, (q, q), 1)
    u.keep = (ii <= jj) if u.reverse else (ii >= jj)
    tri = u.keep.astype(F32)
    d = u.d
    dt_c = _softplus(u.dtc_ref[:, d * n_heads:(d + 1) * n_heads] + u.pc_ref[0:1, :])
    u.dt_r = _softplus(u.dtr_ref[d * n_heads:(d + 1) * n_heads, :] + u.pr_ref[:, 0:1])
    da_c = dt_c * u.pc_ref[1:2, :]
    da_r = u.dt_r * u.pr_ref[:, 1:2]
    u.acs_c = jnp.dot(tri, da_c, preferred_element_type=F32, precision=HIGHEST)
    u.acs_r = lax.dot_general(da_r, tri, (((1,), (1,)), ((), ())), preferred_element_type=F32,
                              precision=HIGHEST)
    edge = 0 if u.reverse else q - 1
    tot_r = u.acs_r[:, edge:edge + 1]
    u.w_end_r = jnp.exp(tot_r - u.acs_r) * u.dt_r
    u.e_tot_r = jnp.exp(tot_r)


def _ssd_stage_cb(u):
    gw = SSD_HEADS * HEAD_DIM
    u.bm_t, u.cm16, u.cb = [], [], []
    for g in range(2):
        bm = u.xbc_ref[:, gw + g * SSD_STATE:gw + (g + 1) * SSD_STATE]
        cm = u.xbc_ref[:, gw + 2 * SSD_STATE + g * SSD_STATE:gw + 2 * SSD_STATE + (g + 1) * SSD_STATE]
        u.bm_t.append(bm.T)
        u.cm16.append(cm.astype(BF16))
        u.cb.append(_dot(u.cm16[g], u.bm_t[g].astype(BF16)))


def _ssd_stage_local(u):
    q = u.xbc_ref.shape[0]
    lane_lo, lane_hi = _half_masks((q, LANES))
    u.y_diag, u.st_new, u.e_in, u.e_tot = [], [], [], []
    for pair in range(SSD_HEADS // 2):
        g = pair // 2
        h0 = 2 * pair
        xs = u.xbc_ref[:, pair * LANES:(pair + 1) * LANES]
        y_pair, st_new, e_in = None, None, []
        for t, lmask in enumerate((lane_lo, lane_hi)):
            h = h0 + t
            a_bc = jnp.broadcast_to(u.acs_c[:, h:h + 1], (q, q))
            seg = a_bc - u.acs_r[h:h + 1, :]
            dec = jnp.where(u.keep, jnp.exp(jnp.where(u.keep, seg, 0.0)), 0.0) * u.dt_r[h:h + 1, :]
            xm = jnp.where(lmask, xs, 0.0).astype(BF16)
            yd = _dot((u.cb[g] * dec).astype(BF16), xm)
            sn = _dot((u.bm_t[g] * u.w_end_r[h:h + 1, :]).astype(BF16), xm)
            y_pair = yd if y_pair is None else y_pair + yd
            st_new = sn if st_new is None else st_new + sn
            e_in.append(jnp.exp(a_bc))
        u.y_diag.append(y_pair)
        u.st_new.append(st_new)
        u.e_in.append(jnp.where(lane_lo, e_in[0], e_in[1]))
        u.e_tot.append(jnp.where(lane_lo[0:1], u.e_tot_r[h0:h0 + 1, :], u.e_tot_r[h0 + 1:h0 + 2, :]))


def _ssd_stage_state(u, state, dl_ref):
    new_state = []
    for pair in range(SSD_HEADS // 2):
        sl = slice(pair * LANES, (pair + 1) * LANES)
        y_pair = u.y_diag[pair] + _dot(u.cm16[pair // 2], state[pair].astype(BF16)) * u.e_in[pair]
        if dl_ref is not None:
            y_pair = y_pair + dl_ref[:, sl] * u.xbc_ref[:, sl]
        u.y_ref[:, sl] = y_pair
        new_state.append(state[pair] * u.e_tot[pair] + u.st_new[pair])
    return new_state


def _ssd_kernel(xf_ref, xb_ref, dtcf_ref, dtcb_ref, dtrf_ref, dtrb_ref, pc_ref, pr_ref, dl_ref, h0f_ref, h0b_ref,
                yf_ref, yb_ref, hf_ref, hb_ref, sf, sb):
    c = pl.program_id(1)

    @pl.when(c == 0)
    def _():
        sf[...] = h0f_ref[...]
        sb[...] = h0b_ref[...]

    q = SSD_CHUNK
    n_sub = xf_ref.shape[0] // q
    n_pairs = sf.shape[1] // LANES
    fwd, bwd = [], []
    for s in range(n_sub):
        f = slice(s * q, (s + 1) * q)
        r = slice((n_sub - 1 - s) * q, (n_sub - s) * q)
        fwd.append(_SsdUnit(xf_ref.at[f], dtcf_ref.at[f], dtrf_ref.at[:, f], pc_ref.at[0], pr_ref.at[0],
                            yf_ref.at[f], 0, False))
        bwd.append(_SsdUnit(xb_ref.at[r], dtcb_ref.at[r], dtrb_ref.at[:, r], pc_ref.at[1], pr_ref.at[1],
                            yb_ref.at[r], 1, True))
    units = [u for pair in zip(fwd, bwd) for u in pair]
    for stage in (_ssd_stage_decay, _ssd_stage_cb, _ssd_stage_local):
        for u in units:
            stage(u)
    st_f = [sf[:, p * LANES:(p + 1) * LANES] for p in range(n_pairs)]
    st_b = [sb[:, p * LANES:(p + 1) * LANES] for p in range(n_pairs)]
    for uf, ub in zip(fwd, bwd):
        st_f = _ssd_stage_state(uf, st_f, dl_ref)
        st_b = _ssd_stage_state(ub, st_b, None)
    for p in range(n_pairs):
        sf[:, p * LANES:(p + 1) * LANES] = st_f[p]
        sb[:, p * LANES:(p + 1) * LANES] = st_b[p]
    hf_ref[...] = sf[...]
    hb_ref[...] = sb[...]


def _ssd(xbc, dt, pc, pr, dl, h0f, h0b):
    bsz, length, cw = xbc.shape
    assert SSD_CHUNK == LANES, "the per-head decay matrix is built as one (chunk, 128-lane) tile"
    n_chunks = length // SSD_CHUNK
    per_step = next(c for c in range(SSD_CHUNKS_PER_STEP, 0, -1) if n_chunks % c == 0)
    q = SSD_CHUNK * per_step
    nc = length // q
    gw = SSD_HEADS * HEAD_DIM
    dtw = dt.shape[-1]
    dt_t = jnp.swapaxes(dt, 1, 2)
    fwd3 = lambda b, c: (b, c, 0)
    bwd3 = lambda b, c: (b, nc - 1 - c, 0)
    st_spec = pl.BlockSpec((None, SSD_STATE, gw), lambda b, c: (b, 0, 0))
    return pl.pallas_call(
        _ssd_kernel,
        grid=(bsz, nc),
        in_specs=[pl.BlockSpec((None, q, cw), fwd3), pl.BlockSpec((None, q, cw), bwd3),
                  pl.BlockSpec((None, q, dtw), fwd3), pl.BlockSpec((None, q, dtw), bwd3),
                  pl.BlockSpec((None, dtw, q), lambda b, c: (b, 0, c)),
                  pl.BlockSpec((None, dtw, q), lambda b, c: (b, 0, nc - 1 - c)),
                  _const_spec(pc.shape), _const_spec(pr.shape), _const_spec(dl.shape), st_spec, st_spec],
        out_specs=[pl.BlockSpec((None, q, gw), fwd3), pl.BlockSpec((None, q, gw), bwd3), st_spec, st_spec],
        out_shape=[jax.ShapeDtypeStruct((bsz, length, gw), F32)] * 2
                  + [jax.ShapeDtypeStruct((bsz, SSD_STATE, gw), F32)] * 2,
        scratch_shapes=[pltpu.VMEM((SSD_STATE, gw), F32)] * 2,
        compiler_params=_cp("parallel", "arbitrary"),
        name="ssd_scan",
    )(xbc, xbc, dt, dt, dt_t, dt_t, pc, pr, dl, h0f, h0b)


def _gelu_tanh(x):
    return 0.5 * x * (1.0 + jnp.tanh(math.sqrt(2.0 / math.pi) * (x + 0.044715 * (x * x * x))))


def _mix_ab(ona_ref, yf_ref, yb_ref, z_ref, ng_ref, w_ref):
    gw = ona_ref.shape[-1]
    y = (yf_ref[...] + yb_ref[...]) * _silu(z_ref[...])
    gated = y * lax.rsqrt(jnp.mean(y * y, axis=-1, keepdims=True) + EPS) * ng_ref[...]
    return _dot(ona_ref[...], w_ref[0:gw, :]) + _dot(gated.astype(BF16), w_ref[gw:2 * gw, :])


def _mix_cd(gate_ref, hf_ref, hb_ref, od_ref, w_ref):
    gw = od_ref.shape[-1]
    lru = _gelu_tanh(gate_ref[...]) * (hf_ref[...] + hb_ref[...])
    return _dot(lru.astype(BF16), w_ref[0:gw, :]) + _dot(od_ref[...], w_ref[gw:2 * gw, :])


def _mix_ffn_kernel(x_ref, mod_ref, *rest, mix, n_mix, k, chunk):
    g_ref, w1_ref, w3_ref, w2_ref, o_ref = rest[n_mix:]
    x = x_ref[...] + mod_ref[5:6, :] * mix(*rest[:n_mix])
    o_ref[...] = _ffn_body(x, mod_ref, g_ref, w1_ref, w3_ref, w2_ref, k, chunk)


def _mixer_out_ffn(mix, name, x, mod, parts, consts, g, w1, w3, w2, k):
    bsz, length, d = x.shape
    if mod.shape[0] == 1 and bsz > 1:
        flat = lambda a: a.reshape(1, bsz * length, a.shape[-1])
        return _mixer_out_ffn(mix, name, flat(x), mod, [flat(a) for a in parts], consts, g, w1, w3, w2,
                              k).reshape(x.shape)
    ff = w1.shape[1]
    tm = min(512, length)
    tok = lambda a: pl.BlockSpec((None, tm, a.shape[-1]), lambda b, i: (b, i, 0))
    return pl.pallas_call(
        functools.partial(_mix_ffn_kernel, mix=mix, n_mix=len(parts) + len(consts), k=k, chunk=FFN_CHUNK),
        grid=(bsz, length // tm),
        in_specs=[tok(x), _mod_spec(mod)] + [tok(a) for a in parts] + [_const_spec(a.shape) for a in consts]
                 + [_const_spec((1, d)), _const_spec((d, ff)), _const_spec((d, ff)), _const_spec((ff, d))],
        out_specs=tok(x),
        out_shape=jax.ShapeDtypeStruct(x.shape, F32),
        compiler_params=pltpu.CompilerParams(dimension_semantics=("parallel", "parallel"),
                                             vmem_limit_bytes=DIFF_VMEM_LIMIT),
        name=name,
    )(x, mod, *parts, *consts, g.reshape(1, d), w1, w3, w2)


def _lru_direction(x_ref, wa_ref, wx_ref, p_ref, carry, h_ref, *, reverse):
    t = x_ref.shape[0]
    x = x_ref[...]
    x16 = x.astype(BF16)
    ta = jnp.tanh(_dot(x16, wa_ref[...]) + p_ref[0:1, :])
    ti = jnp.tanh(_dot(x16, wx_ref[...]) + p_ref[1:2, :])
    c = (-0.5 * LRU_C * LOG2E) * _softplus(-p_ref[2:3, :])
    a = jnp.exp2(c * ta + c)
    xh = 0.5 * x
    b = jnp.sqrt(1.0 - a * a) * (xh * ti + xh)
    n_groups = t // SUBLANES
    a = a.reshape(n_groups, SUBLANES, a.shape[-1])
    b = b.reshape(n_groups, SUBLANES, b.shape[-1])
    row = lax.broadcasted_iota(jnp.int32, a.shape, 1)
    s = 1
    while s < SUBLANES:
        fill = (row >= SUBLANES - s) if reverse else (row < s)
        shift = SUBLANES - s if reverse else s
        a_sh = jnp.where(fill, 1.0, pltpu.roll(a, shift, axis=1))
        b_sh = jnp.where(fill, 0.0, pltpu.roll(b, shift, axis=1))
        b = a * b_sh + b
        a = a * a_sh
        s *= 2
    h_prev = carry[...]
    for gi in (range(n_groups - 1, -1, -1) if reverse else range(n_groups)):
        r0 = gi * SUBLANES
        hg = a[gi] * h_prev + b[gi]
        h_ref[r0:r0 + SUBLANES, :] = hg
        h_prev = hg[0:1] if reverse else hg[SUBLANES - 1:SUBLANES]
    carry[...] = h_prev


def _lru_kernel(xf_ref, xb_ref, wa_ref, wx_ref, p_ref, h0f_ref, h0b_ref, hf_ref, hb_ref, lf_ref, lb_ref, cf, cb):
    c = pl.program_id(1)

    @pl.when(c == 0)
    def _():
        cf[...] = h0f_ref[...]
        cb[...] = h0b_ref[...]

    _lru_direction(xf_ref, wa_ref.at[0], wx_ref.at[0], p_ref.at[0], cf, hf_ref, reverse=False)
    _lru_direction(xb_ref, wa_ref.at[1], wx_ref.at[1], p_ref.at[1], cb, hb_ref, reverse=True)
    lf_ref[...] = cf[...]
    lb_ref[...] = cb[...]


def _lru(x, wa, wx, p, h0f, h0b):
    bsz, length, w = x.shape
    t = min(512, length)
    nt = length // t
    fwd = lambda b, c: (b, c, 0)
    bwd = lambda b, c: (b, nt - 1 - c, 0)
    st = pl.BlockSpec((None, 1, w), lambda b, c: (b, 0, 0))
    return pl.pallas_call(
        _lru_kernel,
        grid=(bsz, nt),
        in_specs=[pl.BlockSpec((None, t, w), fwd), pl.BlockSpec((None, t, w), bwd),
                  _const_spec(wa.shape), _const_spec(wx.shape), _const_spec(p.shape), st, st],
        out_specs=[pl.BlockSpec((None, t, w), fwd), pl.BlockSpec((None, t, w), bwd), st, st],
        out_shape=[jax.ShapeDtypeStruct(x.shape, F32)] * 2 + [jax.ShapeDtypeStruct((bsz, 1, w), F32)] * 2,
        scratch_shapes=[pltpu.VMEM((1, w), F32)] * 2,
        compiler_params=_cp("parallel", "arbitrary"),
        name="rglru_scan",
    )(x, x, wa, wx, p, h0f, h0b)


def _block_diag(wb):
    nb, bs, _ = wb.shape
    eye = jnp.eye(nb, dtype=wb.dtype)
    return (wb[:, :, None, :] * eye[:, None, :, None]).reshape(nb * bs, nb * bs)


def _diff_attn_kernel(q_ref, kt_ref, v_ref, lam_ref, sg_ref, o_ref, qs, m_s, acc_s, s_buf, *, lam_init, n_heads, tk):
    tq = q_ref.shape[0]
    vw = 2 * LANES
    for h in range(n_heads):
        qp = q_ref[:, h * LANES:(h + 1) * LANES]
        lo, hi = _half_masks(qp.shape)
        qs[h, 0:tq, :] = jnp.where(lo, qp, jnp.zeros_like(qp))
        qs[h, tq:2 * tq, :] = jnp.where(hi, qp, jnp.zeros_like(qp))
    m_s[...] = jnp.full(m_s.shape, NEG_BIG, F32)
    acc_s[...] = jnp.zeros(acc_s.shape, F32)

    n_chunks = kt_ref.shape[1] // tk

    def scores(h, k0):
        return _dot(qs[h], kt_ref[h * LANES:(h + 1) * LANES, pl.ds(k0, tk)])

    n_buf = s_buf.shape[0]
    assert n_heads % n_buf == 0 and DIFF_LOOKAHEAD < n_buf <= n_heads
    for h in range(DIFF_LOOKAHEAD):
        s_buf[h % n_buf] = scores(h, 0)

    def chunk(c, carry, last=False):
        k0 = c * tk if last else pl.multiple_of(c * tk, tk)
        for h in range(n_heads):
            ha = h + DIFF_LOOKAHEAD
            if ha < n_heads:
                s_buf[ha % n_buf] = scores(ha, k0)
            elif not last:
                s_buf[ha % n_buf] = scores(ha - n_heads, pl.multiple_of(k0 + tk, tk))
            v_blk = v_ref[pl.ds(k0, tk), h * vw:(h + 1) * vw]
            s = s_buf[h % n_buf]
            m_old = m_s[h]
            m_new = jnp.maximum(m_old, s.max(axis=-1, keepdims=True))
            alpha = jnp.exp2(m_old - m_new)
            p = jnp.concatenate([jnp.exp2(s[:, j * LANES:(j + 1) * LANES] - m_new).astype(BF16)
                                 for j in range(tk // LANES)], axis=1)
            pv = _dot(p, v_blk)
            acc_s[h] = jnp.concatenate([alpha, alpha], axis=1) * acc_s[h] + pv
            m_s[h] = m_new
        return carry

    lax.fori_loop(0, n_chunks - 1, chunk, 0)
    chunk(n_chunks - 1, 0, last=True)

    dl = lam_ref[...]
    lam = (jnp.exp(jnp.sum(dl[0:1] * dl[1:2], axis=-1, keepdims=True))
           - jnp.exp(jnp.sum(dl[2:3] * dl[3:4], axis=-1, keepdims=True)) + lam_init)
    for h in range(n_heads):
        a = acc_s[h]
        o = a[0:tq, 0:LANES] / a[0:tq, LANES:vw] - lam * (a[tq:2 * tq, 0:LANES] / a[tq:2 * tq, LANES:vw])
        o = o * lax.rsqrt(jnp.mean(o * o, axis=-1, keepdims=True) + EPS) * sg_ref[...] * (1.0 - lam_init)
        o_ref[:, h * LANES:(h + 1) * LANES] = o.astype(o_ref.dtype)


def _diff_attention(q, kt_all, v_ext, diff_lam, subln_g, lam_init):
    bsz, s, w = q.shape
    lk = kt_all.shape[2]
    n_heads = w // LANES
    tq = min(DIFF_TQ, s)
    tk = next(c for c in (768, 512, 256, 128) if lk % c == 0)
    return pl.pallas_call(
        functools.partial(_diff_attn_kernel, lam_init=lam_init, n_heads=n_heads, tk=tk),
        grid=(bsz, s // tq),
        in_specs=[pl.BlockSpec((None, tq, w), lambda b, i: (b, i, 0)),
                  pl.BlockSpec((None, w, lk), lambda b, i: (b, 0, 0), pipeline_mode=pl.Buffered(1)),
                  pl.BlockSpec((None, lk, v_ext.shape[2]), lambda b, i: (b, 0, 0), pipeline_mode=pl.Buffered(1)),
                  _const_spec(diff_lam.shape), _const_spec((1, LANES))],
        out_specs=pl.BlockSpec((None, tq, w), lambda b, i: (b, i, 0)),
        out_shape=jax.ShapeDtypeStruct(q.shape, BF16),
        scratch_shapes=[pltpu.VMEM((n_heads, 2 * tq, LANES), BF16), pltpu.VMEM((n_heads, 2 * tq, LANES), F32),
                        pltpu.VMEM((n_heads, 2 * tq, 2 * LANES), F32), pltpu.VMEM((DIFF_SCORE_BUFS, 2 * tq, tk), F32)],
        compiler_params=pltpu.CompilerParams(dimension_semantics=("parallel", "parallel"),
                                             vmem_limit_bytes=DIFF_VMEM_LIMIT),
        name="diff_attention",
    )(q, kt_all, v_ext, diff_lam.astype(F32), subln_g.astype(F32).reshape(1, LANES))


def _pad_cols(w, total):
    return jnp.pad(w, ((0, 0), (0, total - w.shape[1])))


def _layer_ab(x, xc, m, mc, g_mix, w_in, w_out, q_g, k_g, rpb, conv_w, conv_b, dt_bias, a_log, d_skip, norm_g,
              ffn2):
    gw = SSD_HEADS * HEAD_DIM
    segs = [_Seg(gw, 'norm', gain=q_g, scale=HEAD_DIM ** -0.5 * LOG2E), _Seg(gw, 'norm', gain=k_g),
            _Seg(gw, dtype=BF16), _Seg(gw), _Seg(2 * gw, 'conv', conv_w=conv_w, conv_b=conv_b, act=True),
            _Seg(LANES)]
    w16 = _pad_cols(w_in, sum(s.width for s in segs)).astype(BF16)
    pmat = _group_mean_matrix(gw, HEAD_DIM)
    q, k, v, z, xbc, dt = _norm_proj(x, m, g_mix, w16, 1, segs, pmat)
    q_c, k_c, v_c, z_c, xbc_c, dt_c = _norm_proj(xc, mc, g_mix, w16, 1, segs, pmat)
    o_na = _neighborhood_attention(q, k, v, k_c, v_c, _na_bias_table(rpb))
    o_c = _ctx_attention(q_c, k_c, v_c)

    a_neg = -jnp.exp(a_log.astype(F32))
    pc = jnp.stack([dt_bias.astype(F32), a_neg], axis=1)
    pr = jnp.swapaxes(pc, 1, 2)
    dl = jnp.repeat(d_skip.astype(F32), HEAD_DIM).reshape(1, gw)
    zeros = jnp.zeros((x.shape[0], SSD_STATE, gw), F32)
    yf_c, yb_c, hf_c, hb_c = _ssd(xbc_c, dt_c, pc, pr, dl, zeros, zeros)
    yf, yb, _, _ = _ssd(xbc, dt, pc, pr, dl, hf_c, hb_c)

    ng = norm_g.astype(F32).reshape(1, gw)
    wo16 = w_out.astype(BF16)
    x = _mixer_out_ffn(_mix_ab, "mixer_out_ab_ffn", x, m, [o_na, yf, yb, z], [ng, wo16], *ffn2, 2)
    xc = _mixer_out_ffn(_mix_ab, "mixer_out_ab_ffn", xc, mc, [o_c, yf_c, yb_c, z_c], [ng, wo16], *ffn2, 2)
    return x, xc


def _layer_cd(x, xc, m, mc, g_mix, w_in, w_out, conv_w, conv_b, wa, ba, wx, bx, lam_p, q_g, k_g, diff_lam,
              subln_g, lam_init, ffn2):
    gw = w_out.shape[0] // 2
    w16 = w_in.astype(BF16)
    pmat = _group_mean_matrix(gw, HEAD_DIM)
    tabs = _rope_tables(x.shape[1])
    q_scale = HEAD_DIM ** -0.5 * LOG2E

    def segs(rope):
        return [_Seg(gw), _Seg(gw, 'conv', conv_w=conv_w, conv_b=conv_b),
                _Seg(gw, 'norm', gain=q_g, scale=q_scale, rope=rope),
                _Seg(gw, 'norm_t', gain=k_g, rope=rope), _Seg(gw, 'vext')]

    gate, xr, q, kt, v_ext = _norm_proj(x, m, g_mix, w16, 1, segs(True), pmat, tabs)
    _, xr_c, _, kt_c, v_ext_c = _norm_proj(xc, mc, g_mix, w16, 1, segs(False), pmat)

    wa_d = (0.5 * jnp.stack([_block_diag(wa[0]), _block_diag(wa[1])])).astype(BF16)
    wx_d = (0.5 * jnp.stack([_block_diag(wx[0]), _block_diag(wx[1])])).astype(BF16)
    p = jnp.stack([0.5 * ba.astype(F32), 0.5 * bx.astype(F32), lam_p.astype(F32)], axis=1)
    zeros = jnp.zeros((x.shape[0], 1, gw), F32)
    _, _, lf_c, lb_c = _lru(xr_c, wa_d, wx_d, p, zeros, zeros)
    hf, hb, _, _ = _lru(xr, wa_d, wx_d, p, lf_c, lb_c)

    kt_all = jnp.concatenate([kt_c, kt], axis=2)
    v_all = jnp.concatenate([v_ext_c, v_ext], axis=1)
    o = _diff_attention(q, kt_all, v_all, diff_lam, subln_g, lam_init)
    return _mixer_out_ffn(_mix_cd, "mixer_out_cd_ffn", x, m, [gate, hf, hb, o], [w_out.astype(BF16)], *ffn2, 2)


def kernel(x, c, ctx, c_ctx, w_mod, b_mod, norm_g, ffn_w1, ffn_w3, ffn_w2, ab_w_in, ab_w_out, na_q_g, na_k_g, na_rpb, ssd_conv_w, ssd_conv_b, ssd_dt_bias, ssd_a_log, ssd_d, ssd_norm_g, cd_w_in, cd_w_out, lru_conv_w, lru_conv_b, lru_wa, lru_ba, lru_wx, lru_bx, lru_lambda, diff_q_g, diff_k_g, diff_lambda, diff_subln_g):
    bsz, _, d = x.shape
    depth = w_mod.shape[0]
    cc = jnp.concatenate([c.astype(F32), c_ctx.astype(F32)[None], jnp.zeros((8 - bsz - 1, d), F32)], axis=0)
    mods = _modulation(cc, w_mod.astype(F32), b_mod.astype(F32))
    xc = ctx
    for i in range(depth):
        last = i == depth - 1
        j = i // 2
        m = mods[i, :bsz].reshape(bsz, N_MOD, d)
        mc = mods[i, bsz:bsz + 1].reshape(1, N_MOD, d)
        g = norm_g[i].astype(F32)
        ffn1 = (g[0], ffn_w1[i, 0].astype(BF16), ffn_w3[i, 0].astype(BF16), ffn_w2[i, 0].astype(BF16))
        ffn2 = (g[2], ffn_w1[i, 1].astype(BF16), ffn_w3[i, 1].astype(BF16), ffn_w2[i, 1].astype(BF16))
        x = _ffn(x, m, *ffn1, 0)
        xc = _ffn(xc, mc, *ffn1, 0)
        if i % 2 == 0:
            x, xc = _layer_ab(x, xc, m, mc, g[1], ab_w_in[j], ab_w_out[j], na_q_g[j], na_k_g[j], na_rpb[j],
                              ssd_conv_w[j], ssd_conv_b[j], ssd_dt_bias[j], ssd_a_log[j], ssd_d[j], ssd_norm_g[j],
                              ffn2)
        else:
            assert last, "a C|D layer that is not the last one would also need the context stream's mixer output"
            lam_init = 0.8 - 0.6 * math.exp(-0.3 * i)
            x = _layer_cd(x, xc, m, mc, g[1], cd_w_in[j], cd_w_out[j], lru_conv_w[j], lru_conv_b[j], lru_wa[j],
                          lru_ba[j], lru_wx[j], lru_bx[j], lru_lambda[j], diff_q_g[j], diff_k_g[j],
                          diff_lambda[j], diff_subln_g[j], lam_init, ffn2)
    return x
```

```python
import functools
import math

import jax
import jax.numpy as jnp
from jax import lax
from jax.experimental import pallas as pl
from jax.experimental.pallas import tpu as pltpu

F32 = jnp.float32
BF16 = jnp.bfloat16
HIGHEST = lax.Precision.HIGHEST

GRID_W = 64
EPS = 1e-6
HEAD_DIM = 64
N_MOD = 9
NA_WIN_R = 8
NA_WIN_C = 16
SSD_STATE = 128
SSD_CHUNK = 128
SSD_HEADS = 8
LRU_C = 8.0
ROPE_BASE = 10000.0
NEG_BIG = -1e30

LANES = 128
SUBLANES = 8
VMEM_LIMIT = 48 * 1024 * 1024
DIFF_VMEM_LIMIT = 56 * 1024 * 1024
DIFF_TQ = 512
DIFF_LOOKAHEAD = 1
DIFF_SCORE_BUFS = 2
FFN_ROW_BLOCKS = 2
NA_ROWS_PER_ITER = 8
SSD_CHUNKS_PER_STEP = 8
MXU_TILE = 256
FFN_CHUNK = 6 * MXU_TILE
LOG2E = math.log2(math.e)


def _cp(*sem):
    return pltpu.CompilerParams(dimension_semantics=sem, vmem_limit_bytes=VMEM_LIMIT)


def _const_spec(shape):
    nd = len(shape)
    return pl.BlockSpec(shape, lambda *_: (0,) * nd, pipeline_mode=pl.Buffered(1))


def _dot(a, b):
    return jnp.dot(a, b, preferred_element_type=F32)


def _dot_nt(a, b):
    return lax.dot_general(a, b, (((1,), (1,)), ((), ())), preferred_element_type=F32)


def _sigmoid(x):
    return 0.5 * jnp.tanh(0.5 * x) + 0.5


def _silu(x):
    return x * _sigmoid(x)


def _softplus(x):
    return jnp.maximum(x, 0.0) + jnp.log(1.0 + jnp.exp(-jnp.abs(x)))


def _rms_mod(x, g, shift, scale):
    ms = jnp.mean(x * x, axis=-1, keepdims=True)
    return (x * lax.rsqrt(ms + EPS) * g) * (1.0 + scale) + shift


def _mod_kernel(c_ref, w_ref, b_ref, o_ref):
    s = _silu(c_ref[...])
    o_ref[...] = jnp.dot(s, w_ref[...], preferred_element_type=F32, precision=HIGHEST) + b_ref[...]


def _modulation(cc, w_mod, b_mod):
    depth, d, n = w_mod.shape
    tn = 1024
    return pl.pallas_call(
        _mod_kernel,
        grid=(depth, n // tn),
        in_specs=[pl.BlockSpec((8, d), lambda l, j: (0, 0)),
                  pl.BlockSpec((None, d, tn), lambda l, j: (l, 0, j)),
                  pl.BlockSpec((None, 1, tn), lambda l, j: (l, 0, j))],
        out_specs=pl.BlockSpec((None, 8, tn), lambda l, j: (l, 0, j)),
        out_shape=jax.ShapeDtypeStruct((depth, 8, n), F32),
        compiler_params=_cp("parallel", "parallel"),
        name="modulation",
    )(cc, w_mod, b_mod.reshape(depth, 1, n))


def _mod_spec(mod):
    if mod.shape[0] == 1:
        return pl.BlockSpec((None, N_MOD, mod.shape[2]), lambda b, i: (0, 0, 0))
    return pl.BlockSpec((None, N_MOD, mod.shape[2]), lambda b, i: (b, 0, 0))


def _ffn_body(x, mod_ref, g_ref, w1_ref, w3_ref, w2_ref, k, chunk):
    tm = x.shape[0]
    rb = tm // FFN_ROW_BLOCKS if tm % (FFN_ROW_BLOCKS * SUBLANES) == 0 else tm
    ff = w1_ref.shape[1]
    outs = []
    for r0 in range(0, tm, rb):
        xr = x[r0:r0 + rb]
        h = _rms_mod(xr, g_ref[...], mod_ref[3 * k:3 * k + 1, :], mod_ref[3 * k + 1:3 * k + 2, :]).astype(BF16)
        acc = jnp.zeros(xr.shape, F32)
        for c0 in range(0, ff, chunk):
            c1 = min(c0 + chunk, ff)
            a = _dot(h, w1_ref[:, c0:c1])
            b = _dot(h, w3_ref[:, c0:c1])
            acc = acc + _dot((_silu(a) * b).astype(BF16), w2_ref[c0:c1, :])
        outs.append(xr + (0.5 * mod_ref[3 * k + 2:3 * k + 3, :]) * acc)
    return outs[0] if len(outs) == 1 else jnp.concatenate(outs, axis=0)


def _ffn_kernel(x_ref, mod_ref, g_ref, w1_ref, w3_ref, w2_ref, o_ref, *, k, chunk):
    o_ref[...] = _ffn_body(x_ref[...], mod_ref, g_ref, w1_ref, w3_ref, w2_ref, k, chunk)


def _ffn(x, mod, g, w1, w3, w2, k):
    bsz, length, d = x.shape
    if mod.shape[0] == 1 and bsz > 1:
        return _ffn(x.reshape(1, bsz * length, d), mod, g, w1, w3, w2, k).reshape(x.shape)
    ff = w1.shape[1]
    tm = min(512, length)
    return pl.pallas_call(
        functools.partial(_ffn_kernel, k=k, chunk=FFN_CHUNK),
        grid=(bsz, length // tm),
        in_specs=[pl.BlockSpec((None, tm, d), lambda b, i: (b, i, 0)),
                  _mod_spec(mod),
                  _const_spec((1, d)),
                  _const_spec((d, ff)), _const_spec((d, ff)), _const_spec((ff, d))],
        out_specs=pl.BlockSpec((None, tm, d), lambda b, i: (b, i, 0)),
        out_shape=jax.ShapeDtypeStruct(x.shape, F32),
        compiler_params=_cp("parallel", "parallel"),
        name="half_ffn",
    )(x, mod, g.reshape(1, d), w1, w3, w2)


class _Seg:
    def __init__(self, width, kind='plain', dtype=F32, gain=None, scale=1.0, rope=False, conv_w=None, conv_b=None,
                 act=False):
        self.width, self.kind, self.dtype, self.gain, self.scale, self.rope = width, kind, dtype, gain, scale, rope
        self.conv_w, self.conv_b, self.act = conv_w, conv_b, act


def _head_norm(y, gain, pmat, scale, rope_tabs):
    ms = _dot((y * y).astype(BF16), pmat)
    y = y * lax.rsqrt(ms + EPS) * gain
    if rope_tabs is not None:
        reps = y.shape[-1] // rope_tabs[0].shape[-1]
        cs = jnp.concatenate([rope_tabs[0]] * reps, axis=1)
        sn = jnp.concatenate([rope_tabs[1]] * reps, axis=1)
        y = y * cs + _swap_pairs(y) * sn
    return y * scale


def _conv_taps(xx, w_ref, b_ref, t, halo, act):
    taps = w_ref.shape[0]
    left = taps // 2
    tot = t + 2 * halo
    y = b_ref[...]
    for j in range(taps):
        sh = (left - j) % tot
        xs = xx if sh == 0 else pltpu.roll(xx, sh, axis=0)
        y = y + w_ref[j:j + 1, :] * xs[halo:halo + t]
    return _silu(y) if act else y


def _norm_proj_kernel(x_ref, mod_ref, g_ref, w_ref, *rest, k, segs, n_gain, has_rope, n_conv):
    pos = 0
    pmat = None
    if n_gain:
        pmat = rest[0][...]
        pos = 1
    gains = rest[pos:pos + n_gain]
    pos += n_gain
    rope_tabs = None
    if has_rope:
        rope_tabs = (rest[pos][...], rest[pos + 1][...])
        pos += 2
    shift, scale = mod_ref[3 * k:3 * k + 1, :], mod_ref[3 * k + 1:3 * k + 2, :]
    h_halo = None
    if n_conv:
        halo_refs = rest[pos:pos + 2]
        conv_refs = rest[pos + 2:pos + 2 + 2 * n_conv]
        pos += 2 + 2 * n_conv
        h_halo = [_rms_mod(r[...], g_ref[...], shift, scale).astype(BF16) for r in halo_refs]
        i, n = pl.program_id(1), pl.num_programs(1)
    o_refs = rest[pos:]
    h = _rms_mod(x_ref[...], g_ref[...], shift, scale).astype(BF16)
    offs = [sum(s.width for s in segs[:j]) for j in range(len(segs))]
    gain_of = {j: gains[n] for n, j in enumerate(j for j, s in enumerate(segs) if s.kind in ('norm', 'norm_t'))}
    conv_of = {j: n for n, j in enumerate(j for j, s in enumerate(segs) if s.kind == 'conv')}
    rank = {'conv': 0, 'norm': 1, 'norm_t': 1}
    for j in sorted(range(len(segs)), key=lambda j: rank.get(segs[j].kind, 2)):
        seg, o_ref = segs[j], o_refs[j]
        w_seg = w_ref[:, offs[j]:offs[j] + seg.width]
        y = _dot(h, w_seg)
        if seg.kind == 'conv':
            ci = conv_of[j]
            y_prev = jnp.where(i > 0, _dot(h_halo[0], w_seg), 0.0)
            y_next = jnp.where(i < n - 1, _dot(h_halo[1], w_seg), 0.0)
            xx = jnp.concatenate([y_prev, y, y_next], axis=0)
            o_ref[...] = _conv_taps(xx, conv_refs[2 * ci], conv_refs[2 * ci + 1], y.shape[0], y_prev.shape[0],
                                    seg.act).astype(o_ref.dtype)
        elif seg.kind in ('norm', 'norm_t'):
            y = _head_norm(y, gain_of[j][...], pmat, seg.scale, rope_tabs if seg.rope else None)
            o_ref[...] = (y.T if seg.kind == 'norm_t' else y).astype(o_ref.dtype)
        elif seg.kind == 'vext':
            y16 = y.astype(BF16)
            ones = jnp.ones((y.shape[0], LANES), BF16)
            for hp in range(seg.width // LANES):
                o_ref[:, 2 * hp * LANES:(2 * hp + 1) * LANES] = y16[:, hp * LANES:(hp + 1) * LANES]
                o_ref[:, (2 * hp + 1) * LANES:(2 * hp + 2) * LANES] = ones
        else:
            o_ref[...] = y.astype(o_ref.dtype)


def _norm_proj(x, mod, g, w, k, segs, pmat=None, rope_tabs=None):
    bsz, length, d = x.shape
    tm = min(512, length)
    tok = lambda wd: pl.BlockSpec((None, tm, wd), lambda b, i: (b, i, 0))
    in_specs = [tok(d), _mod_spec(mod), _const_spec((1, d)), _const_spec(w.shape)]
    args = [x, mod, g.reshape(1, d), w]
    gains = [s for s in segs if s.kind in ('norm', 'norm_t')]
    if gains:
        in_specs.append(_const_spec(pmat.shape))
        args.append(pmat)
        for s in gains:
            in_specs.append(_const_spec((1, s.width)))
            args.append(jnp.tile(s.gain.astype(F32), s.width // s.gain.shape[0]).reshape(1, s.width))
    has_rope = any(s.rope for s in segs)
    if has_rope:
        in_specs += [pl.BlockSpec((tm, rope_tabs[0].shape[-1]), lambda b, i: (i, 0))] * 2
        args += list(rope_tabs)
    convs = [s for s in segs if s.kind == 'conv']
    if convs:
        per = tm // SUBLANES
        nh = length // SUBLANES
        in_specs += [pl.BlockSpec((None, SUBLANES, d), lambda b, i: (b, jnp.maximum(i * per - 1, 0), 0)),
                     pl.BlockSpec((None, SUBLANES, d), lambda b, i: (b, jnp.minimum((i + 1) * per, nh - 1), 0))]
        args += [x, x]
        for s in convs:
            in_specs += [_const_spec(s.conv_w.shape), _const_spec((1, s.width))]
            args += [s.conv_w.astype(F32), s.conv_b.astype(F32).reshape(1, s.width)]
    out_specs, out_shape = [], []
    for s in segs:
        if s.kind == 'norm_t':
            out_specs.append(pl.BlockSpec((None, s.width, tm), lambda b, i: (b, 0, i)))
            out_shape.append(jax.ShapeDtypeStruct((bsz, s.width, length), BF16))
        elif s.kind == 'vext':
            out_specs.append(tok(2 * s.width))
            out_shape.append(jax.ShapeDtypeStruct((bsz, length, 2 * s.width), BF16))
        else:
            out_specs.append(tok(s.width))
            out_shape.append(jax.ShapeDtypeStruct((bsz, length, s.width), BF16 if s.kind == 'norm' else s.dtype))
    return pl.pallas_call(
        functools.partial(_norm_proj_kernel, k=k, segs=tuple(segs), n_gain=len(gains), has_rope=has_rope,
                          n_conv=len(convs)),
        grid=(bsz, length // tm),
        in_specs=in_specs,
        out_specs=out_specs,
        out_shape=out_shape,
        compiler_params=_cp("parallel", "parallel"),
        name="norm_proj",
    )(*args)


def _swap_pairs(x):
    n = x.shape[-1]
    lane = lax.broadcasted_iota(jnp.int32, x.shape, 1)
    return jnp.where(lane % 2 == 0, pltpu.roll(x, n - 1, axis=1), pltpu.roll(x, 1, axis=1))


def _group_mean_matrix(w, group):
    idx = jnp.arange(w) // group
    return jnp.where(idx[:, None] == idx[None, :], 1.0 / group, 0.0).astype(BF16)


def _rope_tables(length):
    t = jnp.arange(length)
    row = (t // GRID_W).astype(F32)
    col = (t % GRID_W).astype(F32)
    n = HEAD_DIM // 4
    inv = ROPE_BASE ** (-jnp.arange(n, dtype=F32) / n)
    ang = jnp.concatenate([row[:, None] * inv, col[:, None] * inv], axis=-1)
    cos = jnp.repeat(jnp.cos(ang), 2, axis=-1)
    sin = jnp.repeat(jnp.sin(ang), 2, axis=-1)
    sign = jnp.tile(jnp.array([-1.0, 1.0], F32), HEAD_DIM // 2)
    reps = LANES // HEAD_DIM
    return jnp.tile(cos, (1, reps)), jnp.tile(sin * sign, (1, reps))


def _half_masks(shape):
    lane = lax.broadcasted_iota(jnp.int32, shape, len(shape) - 1)
    return lane < HEAD_DIM, lane >= HEAD_DIM


def _softmax_pv(s_list, v_list):
    m = s_list[0].max(axis=-1, keepdims=True)
    for s in s_list[1:]:
        m = jnp.maximum(m, s.max(axis=-1, keepdims=True))
    acc, l = None, None
    for s, v in zip(s_list, v_list):
        p = jnp.exp2(s - m)
        ls = p.sum(axis=-1, keepdims=True)
        o = _dot(p.astype(BF16), v)
        acc = o if acc is None else acc + o
        l = ls if l is None else l + ls
    return acc / l


def _na_kernel(q_ref, kp_ref, kc_ref, kn_ref, vp_ref, vc_ref, vn_ref, kctx_ref, vctx_ref, bias_ref,
               o_ref, kbuf, vbuf, vcbuf, *, rows_per_blk, n_rows):
    i = pl.program_id(1)
    blk = rows_per_blk * GRID_W
    n_pairs = q_ref.shape[-1] // LANES
    vw = 2 * LANES
    kbuf[0:blk, :] = kp_ref[...]
    kbuf[blk:2 * blk, :] = kc_ref[...]
    kbuf[2 * blk:3 * blk, :] = kn_ref[...]
    for hp in range(n_pairs):
        sl = slice(hp * LANES, (hp + 1) * LANES)
        vbuf[0:blk, hp * vw:hp * vw + LANES] = vp_ref[:, sl]
        vbuf[blk:2 * blk, hp * vw:hp * vw + LANES] = vc_ref[:, sl]
        vbuf[2 * blk:3 * blk, hp * vw:hp * vw + LANES] = vn_ref[:, sl]
        vcbuf[:, hp * vw:hp * vw + LANES] = vctx_ref[:, sl]

    @pl.when(i == 0)
    def _():
        for hp in range(n_pairs):
            vbuf[:, hp * vw + LANES:(hp + 1) * vw] = jnp.ones((3 * blk, LANES), BF16)
            vcbuf[:, hp * vw + LANES:(hp + 1) * vw] = jnp.ones((vcbuf.shape[0], LANES), BF16)
    win = NA_WIN_R * GRID_W

    lo, hi = _half_masks((GRID_W, LANES))

    def rows_body(jj, carry):
        rows = [jj * NA_ROWS_PER_ITER + u for u in range(NA_ROWS_PER_ITER)]
        offs, units = [], []
        for j in rows:
            r = i * rows_per_blk + j
            r0 = jnp.clip(r - NA_WIN_R // 2, 0, n_rows - NA_WIN_R)
            off = pl.multiple_of((r0 - (i - 1) * rows_per_blk) * GRID_W, GRID_W)
            cfg = r0 - r + NA_WIN_R - 1
            offs.append(off)
            qrow = q_ref[pl.ds(pl.multiple_of(j * GRID_W, GRID_W), GRID_W), :]
            for hp in range(n_pairs):
                sl = slice(hp * LANES, (hp + 1) * LANES)
                qp = qrow[:, sl]
                zero = jnp.zeros_like(qp)
                qst = jnp.concatenate([jnp.where(lo, qp, zero), jnp.where(hi, qp, zero)], axis=0)
                units.append((_dot_nt(qst, kbuf[pl.ds(off, win), sl]) + bias_ref[cfg, hp],
                              _dot_nt(qst, kctx_ref[:, sl])))
        probs = []
        for s_lat, s_ctx in units:
            m = jnp.maximum(s_lat.max(axis=-1, keepdims=True), s_ctx.max(axis=-1, keepdims=True))
            probs.append((jnp.exp2(s_lat - m).astype(BF16), jnp.exp2(s_ctx - m).astype(BF16)))
        for u, j in enumerate(rows):
            outs = []
            for hp in range(n_pairs):
                p_lat, p_ctx = probs[u * n_pairs + hp]
                ov = (_dot(p_lat, vbuf[pl.ds(offs[u], win), hp * vw:(hp + 1) * vw])
                      + _dot(p_ctx, vcbuf[:, hp * vw:(hp + 1) * vw]))
                o = ov[:, 0:LANES] / ov[:, LANES:vw]
                outs.append(jnp.where(lo, o[0:GRID_W], o[GRID_W:2 * GRID_W]))
            o_ref[pl.ds(pl.multiple_of(j * GRID_W, GRID_W), GRID_W), :] = (
                jnp.concatenate(outs, axis=1).astype(o_ref.dtype))
        return carry

    lax.fori_loop(0, rows_per_blk // NA_ROWS_PER_ITER, rows_body, 0)


def _na_bias_table(rpb):
    heads = rpb.shape[0]
    qc = jnp.arange(GRID_W)
    kc = jnp.arange(GRID_W)
    qc0 = jnp.clip(qc - NA_WIN_C // 2, 0, GRID_W - NA_WIN_C)
    col_in = (kc[None, :] >= qc0[:, None]) & (kc[None, :] < qc0[:, None] + NA_WIN_C)
    pad = GRID_W - NA_WIN_C
    rp = jnp.pad(rpb.astype(F32), ((0, 0), (0, 0), (pad, pad)))
    toep = jnp.stack([rp[:, :, GRID_W - 1 - q:2 * GRID_W - 1 - q] for q in range(GRID_W)], axis=2)
    toep = jnp.where(col_in[None, None], toep, NEG_BIG)
    tab = jnp.stack([toep[:, c:c + NA_WIN_R] for c in range(NA_WIN_R)], axis=0)
    tab = tab.transpose(0, 1, 3, 2, 4) * LOG2E
    return tab.reshape(NA_WIN_R, heads // 2, 2 * GRID_W, NA_WIN_R * GRID_W)


def _neighborhood_attention(q, k, v, kc, vc, bias):
    bsz, s, w = q.shape
    n_rows = s // GRID_W
    rpb_rows = NA_WIN_R
    blk = rpb_rows * GRID_W
    nb = s // blk
    lc = kc.shape[1]
    cur = pl.BlockSpec((None, blk, w), lambda b, i: (b, i, 0))
    prev = pl.BlockSpec((None, blk, w), lambda b, i: (b, jnp.maximum(i - 1, 0), 0))
    nxt = pl.BlockSpec((None, blk, w), lambda b, i: (b, jnp.minimum(i + 1, nb - 1), 0))
    ctx = pl.BlockSpec((None, lc, w), lambda b, i: (b, 0, 0))
    return pl.pallas_call(
        functools.partial(_na_kernel, rows_per_blk=rpb_rows, n_rows=n_rows),
        grid=(bsz, nb),
        in_specs=[cur, prev, cur, nxt, prev, cur, nxt, ctx, ctx, _const_spec(bias.shape)],
        out_specs=cur,
        out_shape=jax.ShapeDtypeStruct(q.shape, BF16),
        scratch_shapes=[pltpu.VMEM((3 * blk, w), BF16), pltpu.VMEM((3 * blk, 2 * w), BF16),
                        pltpu.VMEM((lc, 2 * w), BF16)],
        compiler_params=_cp("parallel", "arbitrary"),
        name="neighborhood_attention",
    )(q, k, k, k, v, v, v, kc, vc, bias)


def _ctx_attn_kernel(q_ref, k_ref, v_ref, o_ref):
    n_pairs = q_ref.shape[-1] // LANES
    outs = []
    for hp in range(n_pairs):
        sl = slice(hp * LANES, (hp + 1) * LANES)
        qp, kp, vp = q_ref[:, sl], k_ref[:, sl], v_ref[:, sl]
        o_pair = None
        for t, mask in enumerate(_half_masks(qp.shape)):
            qm = jnp.where(mask, qp, jnp.zeros_like(qp))
            vm = jnp.where(mask, vp, jnp.zeros_like(vp))
            o_t = _softmax_pv([_dot_nt(qm, kp)], [vm])
            o_pair = o_t if o_pair is None else o_pair + o_t
        outs.append(o_pair)
    o_ref[...] = jnp.concatenate(outs, axis=1).astype(o_ref.dtype)


def _ctx_attention(q, k, v):
    bsz, lc, w = q.shape
    spec = pl.BlockSpec((None, lc, w), lambda b: (b, 0, 0))
    return pl.pallas_call(
        _ctx_attn_kernel, grid=(bsz,), in_specs=[spec, spec, spec], out_specs=spec,
        out_shape=jax.ShapeDtypeStruct(q.shape, BF16),
        compiler_params=_cp("parallel"), name="ctx_attention",
    )(q, k, v)


class _SsdUnit:
    def __init__(self, xbc_ref, dtc_ref, dtr_ref, pc_ref, pr_ref, y_ref, d, reverse):
        self.xbc_ref, self.dtc_ref, self.dtr_ref, self.pc_ref, self.pr_ref = xbc_ref, dtc_ref, dtr_ref, pc_ref, pr_ref
        self.y_ref, self.d, self.reverse = y_ref, d, reverse


def _ssd_stage_decay(u):
    q = u.xbc_ref.shape[0]
    n_heads = SSD_HEADS
    ii = lax.broadcasted_iota(jnp.int32, (q, q), 0)
    jj = lax.broadcasted_iota(jnp.int32, (q, q), 1)
    u.keep = (ii <= jj) if u.reverse else (ii >= jj)
    tri = u.keep.astype(F32)
    d = u.d
    dt_c = _softplus(u.dtc_ref[:, d * n_heads:(d + 1) * n_heads] + u.pc_ref[0:1, :])
    u.dt_r = _softplus(u.dtr_ref[d * n_heads:(d + 1) * n_heads, :] + u.pr_ref[:, 0:1])
    da_c = dt_c * u.pc_ref[1:2, :]
    da_r = u.dt_r * u.pr_ref[:, 1:2]
    u.acs_c = jnp.dot(tri, da_c, preferred_element_type=F32, precision=HIGHEST)
    u.acs_r = lax.dot_general(da_r, tri, (((1,), (1,)), ((), ())), preferred_element_type=F32,
                              precision=HIGHEST)
    edge = 0 if u.reverse else q - 1
    tot_r = u.acs_r[:, edge:edge + 1]
    u.w_end_r = jnp.exp(tot_r - u.acs_r) * u.dt_r
    u.e_tot_r = jnp.exp(tot_r)


def _ssd_stage_cb(u):
    gw = SSD_HEADS * HEAD_DIM
    u.bm_t, u.cm16, u.cb = [], [], []
    for g in range(2):
        bm = u.xbc_ref[:, gw + g * SSD_STATE:gw + (g + 1) * SSD_STATE]
        cm = u.xbc_ref[:, gw + 2 * SSD_STATE + g * SSD_STATE:gw + 2 * SSD_STATE + (g + 1) * SSD_STATE]
        u.bm_t.append(bm.T)
        u.cm16.append(cm.astype(BF16))
        u.cb.append(_dot(u.cm16[g], u.bm_t[g].astype(BF16)))


def _ssd_stage_local(u):
    q = u.xbc_ref.shape[0]
    lane_lo, lane_hi = _half_masks((q, LANES))
    u.y_diag, u.st_new, u.e_in, u.e_tot = [], [], [], []
    for pair in range(SSD_HEADS // 2):
        g = pair // 2
        h0 = 2 * pair
        xs = u.xbc_ref[:, pair * LANES:(pair + 1) * LANES]
        y_pair, st_new, e_in = None, None, []
        for t, lmask in enumerate((lane_lo, lane_hi)):
            h = h0 + t
            a_bc = jnp.broadcast_to(u.acs_c[:, h:h + 1], (q, q))
            seg = a_bc - u.acs_r[h:h + 1, :]
            dec = jnp.where(u.keep, jnp.exp(jnp.where(u.keep, seg, 0.0)), 0.0) * u.dt_r[h:h + 1, :]
            xm = jnp.where(lmask, xs, 0.0).astype(BF16)
            yd = _dot((u.cb[g] * dec).astype(BF16), xm)
            sn = _dot((u.bm_t[g] * u.w_end_r[h:h + 1, :]).astype(BF16), xm)
            y_pair = yd if y_pair is None else y_pair + yd
            st_new = sn if st_new is None else st_new + sn
            e_in.append(jnp.exp(a_bc))
        u.y_diag.append(y_pair)
        u.st_new.append(st_new)
        u.e_in.append(jnp.where(lane_lo, e_in[0], e_in[1]))
        u.e_tot.append(jnp.where(lane_lo[0:1], u.e_tot_r[h0:h0 + 1, :], u.e_tot_r[h0 + 1:h0 + 2, :]))


def _ssd_stage_state(u, state, dl_ref):
    new_state = []
    for pair in range(SSD_HEADS // 2):
        sl = slice(pair * LANES, (pair + 1) * LANES)
        y_pair = u.y_diag[pair] + _dot(u.cm16[pair // 2], state[pair].astype(BF16)) * u.e_in[pair]
        if dl_ref is not None:
            y_pair = y_pair + dl_ref[:, sl] * u.xbc_ref[:, sl]
        u.y_ref[:, sl] = y_pair
        new_state.append(state[pair] * u.e_tot[pair] + u.st_new[pair])
    return new_state


def _ssd_kernel(xf_ref, xb_ref, dtcf_ref, dtcb_ref, dtrf_ref, dtrb_ref, pc_ref, pr_ref, dl_ref, h0f_ref, h0b_ref,
                yf_ref, yb_ref, hf_ref, hb_ref, sf, sb):
    c = pl.program_id(1)

    @pl.when(c == 0)
    def _():
        sf[...] = h0f_ref[...]
        sb[...] = h0b_ref[...]

    q = SSD_CHUNK
    n_sub = xf_ref.shape[0] // q
    n_pairs = sf.shape[1] // LANES
    fwd, bwd = [], []
    for s in range(n_sub):
        f = slice(s * q, (s + 1) * q)
        r = slice((n_sub - 1 - s) * q, (n_sub - s) * q)
        fwd.append(_SsdUnit(xf_ref.at[f], dtcf_ref.at[f], dtrf_ref.at[:, f], pc_ref.at[0], pr_ref.at[0],
                            yf_ref.at[f], 0, False))
        bwd.append(_SsdUnit(xb_ref.at[r], dtcb_ref.at[r], dtrb_ref.at[:, r], pc_ref.at[1], pr_ref.at[1],
                            yb_ref.at[r], 1, True))
    units = [u for pair in zip(fwd, bwd) for u in pair]
    for stage in (_ssd_stage_decay, _ssd_stage_cb, _ssd_stage_local):
        for u in units:
            stage(u)
    st_f = [sf[:, p * LANES:(p + 1) * LANES] for p in range(n_pairs)]
    st_b = [sb[:, p * LANES:(p + 1) * LANES] for p in range(n_pairs)]
    for uf, ub in zip(fwd, bwd):
        st_f = _ssd_stage_state(uf, st_f, dl_ref)
        st_b = _ssd_stage_state(ub, st_b, None)
    for p in range(n_pairs):
        sf[:, p * LANES:(p + 1) * LANES] = st_f[p]
        sb[:, p * LANES:(p + 1) * LANES] = st_b[p]
    hf_ref[...] = sf[...]
    hb_ref[...] = sb[...]


def _ssd(xbc, dt, pc, pr, dl, h0f, h0b):
    bsz, length, cw = xbc.shape
    assert SSD_CHUNK == LANES, "the per-head decay matrix is built as one (chunk, 128-lane) tile"
    n_chunks = length // SSD_CHUNK
    per_step = next(c for c in range(SSD_CHUNKS_PER_STEP, 0, -1) if n_chunks % c == 0)
    q = SSD_CHUNK * per_step
    nc = length // q
    gw = SSD_HEADS * HEAD_DIM
    dtw = dt.shape[-1]
    dt_t = jnp.swapaxes(dt, 1, 2)
    fwd3 = lambda b, c: (b, c, 0)
    bwd3 = lambda b, c: (b, nc - 1 - c, 0)
    st_spec = pl.BlockSpec((None, SSD_STATE, gw), lambda b, c: (b, 0, 0))
    return pl.pallas_call(
        _ssd_kernel,
        grid=(bsz, nc),
        in_specs=[pl.BlockSpec((None, q, cw), fwd3), pl.BlockSpec((None, q, cw), bwd3),
                  pl.BlockSpec((None, q, dtw), fwd3), pl.BlockSpec((None, q, dtw), bwd3),
                  pl.BlockSpec((None, dtw, q), lambda b, c: (b, 0, c)),
                  pl.BlockSpec((None, dtw, q), lambda b, c: (b, 0, nc - 1 - c)),
                  _const_spec(pc.shape), _const_spec(pr.shape), _const_spec(dl.shape), st_spec, st_spec],
        out_specs=[pl.BlockSpec((None, q, gw), fwd3), pl.BlockSpec((None, q, gw), bwd3), st_spec, st_spec],
        out_shape=[jax.ShapeDtypeStruct((bsz, length, gw), F32)] * 2
                  + [jax.ShapeDtypeStruct((bsz, SSD_STATE, gw), F32)] * 2,
        scratch_shapes=[pltpu.VMEM((SSD_STATE, gw), F32)] * 2,
        compiler_params=_cp("parallel", "arbitrary"),
        name="ssd_scan",
    )(xbc, xbc, dt, dt, dt_t, dt_t, pc, pr, dl, h0f, h0b)


def _gelu_tanh(x):
    return 0.5 * x * (1.0 + jnp.tanh(math.sqrt(2.0 / math.pi) * (x + 0.044715 * (x * x * x))))


def _mix_ab(ona_ref, yf_ref, yb_ref, z_ref, ng_ref, w_ref):
    gw = ona_ref.shape[-1]
    y = (yf_ref[...] + yb_ref[...]) * _silu(z_ref[...])
    gated = y * lax.rsqrt(jnp.mean(y * y, axis=-1, keepdims=True) + EPS) * ng_ref[...]
    return _dot(ona_ref[...], w_ref[0:gw, :]) + _dot(gated.astype(BF16), w_ref[gw:2 * gw, :])


def _mix_cd(gate_ref, hf_ref, hb_ref, od_ref, w_ref):
    gw = od_ref.shape[-1]
    lru = _gelu_tanh(gate_ref[...]) * (hf_ref[...] + hb_ref[...])
    return _dot(lru.astype(BF16), w_ref[0:gw, :]) + _dot(od_ref[...], w_ref[gw:2 * gw, :])


def _mix_ffn_kernel(x_ref, mod_ref, *rest, mix, n_mix, k, chunk):
    g_ref, w1_ref, w3_ref, w2_ref, o_ref = rest[n_mix:]
    x = x_ref[...] + mod_ref[5:6, :] * mix(*rest[:n_mix])
    o_ref[...] = _ffn_body(x, mod_ref, g_ref, w1_ref, w3_ref, w2_ref, k, chunk)


def _mixer_out_ffn(mix, name, x, mod, parts, consts, g, w1, w3, w2, k):
    bsz, length, d = x.shape
    if mod.shape[0] == 1 and bsz > 1:
        flat = lambda a: a.reshape(1, bsz * length, a.shape[-1])
        return _mixer_out_ffn(mix, name, flat(x), mod, [flat(a) for a in parts], consts, g, w1, w3, w2,
                              k).reshape(x.shape)
    ff = w1.shape[1]
    tm = min(512, length)
    tok = lambda a: pl.BlockSpec((None, tm, a.shape[-1]), lambda b, i: (b, i, 0))
    return pl.pallas_call(
        functools.partial(_mix_ffn_kernel, mix=mix, n_mix=len(parts) + len(consts), k=k, chunk=FFN_CHUNK),
        grid=(bsz, length // tm),
        in_specs=[tok(x), _mod_spec(mod)] + [tok(a) for a in parts] + [_const_spec(a.shape) for a in consts]
                 + [_const_spec((1, d)), _const_spec((d, ff)), _const_spec((d, ff)), _const_spec((ff, d))],
        out_specs=tok(x),
        out_shape=jax.ShapeDtypeStruct(x.shape, F32),
        compiler_params=pltpu.CompilerParams(dimension_semantics=("parallel", "parallel"),
                                             vmem_limit_bytes=DIFF_VMEM_LIMIT),
        name=name,
    )(x, mod, *parts, *consts, g.reshape(1, d), w1, w3, w2)


def _lru_direction(x_ref, wa_ref, wx_ref, p_ref, carry, h_ref, *, reverse):
    t = x_ref.shape[0]
    x = x_ref[...]
    x16 = x.astype(BF16)
    ta = jnp.tanh(_dot(x16, wa_ref[...]) + p_ref[0:1, :])
    ti = jnp.tanh(_dot(x16, wx_ref[...]) + p_ref[1:2, :])
    c = (-0.5 * LRU_C * LOG2E) * _softplus(-p_ref[2:3, :])
    a = jnp.exp2(c * ta + c)
    xh = 0.5 * x
    b = jnp.sqrt(1.0 - a * a) * (xh * ti + xh)
    n_groups = t // SUBLANES
    a = a.reshape(n_groups, SUBLANES, a.shape[-1])
    b = b.reshape(n_groups, SUBLANES, b.shape[-1])
    row = lax.broadcasted_iota(jnp.int32, a.shape, 1)
    s = 1
    while s < SUBLANES:
        fill = (row >= SUBLANES - s) if reverse else (row < s)
        shift = SUBLANES - s if reverse else s
        a_sh = jnp.where(fill, 1.0, pltpu.roll(a, shift, axis=1))
        b_sh = jnp.where(fill, 0.0, pltpu.roll(b, shift, axis=1))
        b = a * b_sh + b
        a = a * a_sh
        s *= 2
    h_prev = carry[...]
    for gi in (range(n_groups - 1, -1, -1) if reverse else range(n_groups)):
        r0 = gi * SUBLANES
        hg = a[gi] * h_prev + b[gi]
        h_ref[r0:r0 + SUBLANES, :] = hg
        h_prev = hg[0:1] if reverse else hg[SUBLANES - 1:SUBLANES]
    carry[...] = h_prev


def _lru_kernel(xf_ref, xb_ref, wa_ref, wx_ref, p_ref, h0f_ref, h0b_ref, hf_ref, hb_ref, lf_ref, lb_ref, cf, cb):
    c = pl.program_id(1)

    @pl.when(c == 0)
    def _():
        cf[...] = h0f_ref[...]
        cb[...] = h0b_ref[...]

    _lru_direction(xf_ref, wa_ref.at[0], wx_ref.at[0], p_ref.at[0], cf, hf_ref, reverse=False)
    _lru_direction(xb_ref, wa_ref.at[1], wx_ref.at[1], p_ref.at[1], cb, hb_ref, reverse=True)
    lf_ref[...] = cf[...]
    lb_ref[...] = cb[...]


def _lru(x, wa, wx, p, h0f, h0b):
    bsz, length, w = x.shape
    t = min(512, length)
    nt = length // t
    fwd = lambda b, c: (b, c, 0)
    bwd = lambda b, c: (b, nt - 1 - c, 0)
    st = pl.BlockSpec((None, 1, w), lambda b, c: (b, 0, 0))
    return pl.pallas_call(
        _lru_kernel,
        grid=(bsz, nt),
        in_specs=[pl.BlockSpec((None, t, w), fwd), pl.BlockSpec((None, t, w), bwd),
                  _const_spec(wa.shape), _const_spec(wx.shape), _const_spec(p.shape), st, st],
        out_specs=[pl.BlockSpec((None, t, w), fwd), pl.BlockSpec((None, t, w), bwd), st, st],
        out_shape=[jax.ShapeDtypeStruct(x.shape, F32)] * 2 + [jax.ShapeDtypeStruct((bsz, 1, w), F32)] * 2,
        scratch_shapes=[pltpu.VMEM((1, w), F32)] * 2,
        compiler_params=_cp("parallel", "arbitrary"),
        name="rglru_scan",
    )(x, x, wa, wx, p, h0f, h0b)


def _block_diag(wb):
    nb, bs, _ = wb.shape
    eye = jnp.eye(nb, dtype=wb.dtype)
    return (wb[:, :, None, :] * eye[:, None, :, None]).reshape(nb * bs, nb * bs)


def _diff_attn_kernel(q_ref, kt_ref, v_ref, lam_ref, sg_ref, o_ref, qs, m_s, acc_s, s_buf, *, lam_init, n_heads, tk):
    tq = q_ref.shape[0]
    vw = 2 * LANES
    for h in range(n_heads):
        qp = q_ref[:, h * LANES:(h + 1) * LANES]
        lo, hi = _half_masks(qp.shape)
        qs[h, 0:tq, :] = jnp.where(lo, qp, jnp.zeros_like(qp))
        qs[h, tq:2 * tq, :] = jnp.where(hi, qp, jnp.zeros_like(qp))
    m_s[...] = jnp.full(m_s.shape, NEG_BIG, F32)
    acc_s[...] = jnp.zeros(acc_s.shape, F32)

    n_chunks = kt_ref.shape[1] // tk

    def scores(h, k0):
        return _dot(qs[h], kt_ref[h * LANES:(h + 1) * LANES, pl.ds(k0, tk)])

    n_buf = s_buf.shape[0]
    assert n_heads % n_buf == 0 and DIFF_LOOKAHEAD < n_buf <= n_heads
    for h in range(DIFF_LOOKAHEAD):
        s_buf[h % n_buf] = scores(h, 0)

    def chunk(c, carry, last=False):
        k0 = c * tk if last else pl.multiple_of(c * tk, tk)
        for h in range(n_heads):
            ha = h + DIFF_LOOKAHEAD
            if ha < n_heads:
                s_buf[ha % n_buf] = scores(ha, k0)
            elif not last:
                s_buf[ha % n_buf] = scores(ha - n_heads, pl.multiple_of(k0 + tk, tk))
            v_blk = v_ref[pl.ds(k0, tk), h * vw:(h + 1) * vw]
            s = s_buf[h % n_buf]
            m_old = m_s[h]
            m_new = jnp.maximum(m_old, s.max(axis=-1, keepdims=True))
            alpha = jnp.exp2(m_old - m_new)
            p = jnp.concatenate([jnp.exp2(s[:, j * LANES:(j + 1) * LANES] - m_new).astype(BF16)
                                 for j in range(tk // LANES)], axis=1)
            pv = _dot(p, v_blk)
            acc_s[h] = jnp.concatenate([alpha, alpha], axis=1) * acc_s[h] + pv
            m_s[h] = m_new
        return carry

    lax.fori_loop(0, n_chunks - 1, chunk, 0)
    chunk(n_chunks - 1, 0, last=True)

    dl = lam_ref[...]
    lam = (jnp.exp(jnp.sum(dl[0:1] * dl[1:2], axis=-1, keepdims=True))
           - jnp.exp(jnp.sum(dl[2:3] * dl[3:4], axis=-1, keepdims=True)) + lam_init)
    for h in range(n_heads):
        a = acc_s[h]
        o = a[0:tq, 0:LANES] / a[0:tq, LANES:vw] - lam * (a[tq:2 * tq, 0:LANES] / a[tq:2 * tq, LANES:vw])
        o = o * lax.rsqrt(jnp.mean(o * o, axis=-1, keepdims=True) + EPS) * sg_ref[...] * (1.0 - lam_init)
        o_ref[:, h * LANES:(h + 1) * LANES] = o.astype(o_ref.dtype)


def _diff_attention(q, kt_all, v_ext, diff_lam, subln_g, lam_init):
    bsz, s, w = q.shape
    lk = kt_all.shape[2]
    n_heads = w // LANES
    tq = min(DIFF_TQ, s)
    tk = next(c for c in (768, 512, 256, 128) if lk % c == 0)
    return pl.pallas_call(
        functools.partial(_diff_attn_kernel, lam_init=lam_init, n_heads=n_heads, tk=tk),
        grid=(bsz, s // tq),
        in_specs=[pl.BlockSpec((None, tq, w), lambda b, i: (b, i, 0)),
                  pl.BlockSpec((None, w, lk), lambda b, i: (b, 0, 0), pipeline_mode=pl.Buffered(1)),
                  pl.BlockSpec((None, lk, v_ext.shape[2]), lambda b, i: (b, 0, 0), pipeline_mode=pl.Buffered(1)),
                  _const_spec(diff_lam.shape), _const_spec((1, LANES))],
        out_specs=pl.BlockSpec((None, tq, w), lambda b, i: (b, i, 0)),
        out_shape=jax.ShapeDtypeStruct(q.shape, BF16),
        scratch_shapes=[pltpu.VMEM((n_heads, 2 * tq, LANES), BF16), pltpu.VMEM((n_heads, 2 * tq, LANES), F32),
                        pltpu.VMEM((n_heads, 2 * tq, 2 * LANES), F32), pltpu.VMEM((DIFF_SCORE_BUFS, 2 * tq, tk), F32)],
        compiler_params=pltpu.CompilerParams(dimension_semantics=("parallel", "parallel"),
                                             vmem_limit_bytes=DIFF_VMEM_LIMIT),
        name="diff_attention",
    )(q, kt_all, v_ext, diff_lam.astype(F32), subln_g.astype(F32).reshape(1, LANES))


def _pad_cols(w, total):
    return jnp.pad(w, ((0, 0), (0, total - w.shape[1])))


def _layer_ab(x, xc, m, mc, g_mix, w_in, w_out, q_g, k_g, rpb, conv_w, conv_b, dt_bias, a_log, d_skip, norm_g,
              ffn2):
    gw = SSD_HEADS * HEAD_DIM
    segs = [_Seg(gw, 'norm', gain=q_g, scale=HEAD_DIM ** -0.5 * LOG2E), _Seg(gw, 'norm', gain=k_g),
            _Seg(gw, dtype=BF16), _Seg(gw), _Seg(2 * gw, 'conv', conv_w=conv_w, conv_b=conv_b, act=True),
            _Seg(LANES)]
    w16 = _pad_cols(w_in, sum(s.width for s in segs)).astype(BF16)
    pmat = _group_mean_matrix(gw, HEAD_DIM)
    q, k, v, z, xbc, dt = _norm_proj(x, m, g_mix, w16, 1, segs, pmat)
    q_c, k_c, v_c, z_c, xbc_c, dt_c = _norm_proj(xc, mc, g_mix, w16, 1, segs, pmat)
    o_na = _neighborhood_attention(q, k, v, k_c, v_c, _na_bias_table(rpb))
    o_c = _ctx_attention(q_c, k_c, v_c)

    a_neg = -jnp.exp(a_log.astype(F32))
    pc = jnp.stack([dt_bias.astype(F32), a_neg], axis=1)
    pr = jnp.swapaxes(pc, 1, 2)
    dl = jnp.repeat(d_skip.astype(F32), HEAD_DIM).reshape(1, gw)
    zeros = jnp.zeros((x.shape[0], SSD_STATE, gw), F32)
    yf_c, yb_c, hf_c, hb_c = _ssd(xbc_c, dt_c, pc, pr, dl, zeros, zeros)
    yf, yb, _, _ = _ssd(xbc, dt, pc, pr, dl, hf_c, hb_c)

    ng = norm_g.astype(F32).reshape(1, gw)
    wo16 = w_out.astype(BF16)
    x = _mixer_out_ffn(_mix_ab, "mixer_out_ab_ffn", x, m, [o_na, yf, yb, z], [ng, wo16], *ffn2, 2)
    xc = _mixer_out_ffn(_mix_ab, "mixer_out_ab_ffn", xc, mc, [o_c, yf_c, yb_c, z_c], [ng, wo16], *ffn2, 2)
    return x, xc


def _layer_cd(x, xc, m, mc, g_mix, w_in, w_out, conv_w, conv_b, wa, ba, wx, bx, lam_p, q_g, k_g, diff_lam,
              subln_g, lam_init, ffn2):
    gw = w_out.shape[0] // 2
    w16 = w_in.astype(BF16)
    pmat = _group_mean_matrix(gw, HEAD_DIM)
    tabs = _rope_tables(x.shape[1])
    q_scale = HEAD_DIM ** -0.5 * LOG2E

    def segs(rope):
        return [_Seg(gw), _Seg(gw, 'conv', conv_w=conv_w, conv_b=conv_b),
                _Seg(gw, 'norm', gain=q_g, scale=q_scale, rope=rope),
                _Seg(gw, 'norm_t', gain=k_g, rope=rope), _Seg(gw, 'vext')]

    gate, xr, q, kt, v_ext = _norm_proj(x, m, g_mix, w16, 1, segs(True), pmat, tabs)
    _, xr_c, _, kt_c, v_ext_c = _norm_proj(xc, mc, g_mix, w16, 1, segs(False), pmat)

    wa_d = (0.5 * jnp.stack([_block_diag(wa[0]), _block_diag(wa[1])])).astype(BF16)
    wx_d = (0.5 * jnp.stack([_block_diag(wx[0]), _block_diag(wx[1])])).astype(BF16)
    p = jnp.stack([0.5 * ba.astype(F32), 0.5 * bx.astype(F32), lam_p.astype(F32)], axis=1)
    zeros = jnp.zeros((x.shape[0], 1, gw), F32)
    _, _, lf_c, lb_c = _lru(xr_c, wa_d, wx_d, p, zeros, zeros)
    hf, hb, _, _ = _lru(xr, wa_d, wx_d, p, lf_c, lb_c)

    kt_all = jnp.concatenate([kt_c, kt], axis=2)
    v_all = jnp.concatenate([v_ext_c, v_ext], axis=1)
    o = _diff_attention(q, kt_all, v_all, diff_lam, subln_g, lam_init)
    return _mixer_out_ffn(_mix_cd, "mixer_out_cd_ffn", x, m, [gate, hf, hb, o], [w_out.astype(BF16)], *ffn2, 2)


def kernel(x, c, ctx, c_ctx, w_mod, b_mod, norm_g, ffn_w1, ffn_w3, ffn_w2, ab_w_in, ab_w_out, na_q_g, na_k_g, na_rpb, ssd_conv_w, ssd_conv_b, ssd_dt_bias, ssd_a_log, ssd_d, ssd_norm_g, cd_w_in, cd_w_out, lru_conv_w, lru_conv_b, lru_wa, lru_ba, lru_wx, lru_bx, lru_lambda, diff_q_g, diff_k_g, diff_lambda, diff_subln_g):
    bsz, _, d = x.shape
    depth = w_mod.shape[0]
    cc = jnp.concatenate([c.astype(F32), c_ctx.astype(F32)[None], jnp.zeros((8 - bsz - 1, d), F32)], axis=0)
    mods = _modulation(cc, w_mod.astype(F32), b_mod.astype(F32))
    xc = ctx
    for i in range(depth):
        last = i == depth - 1
        j = i // 2
        m = mods[i, :bsz].reshape(bsz, N_MOD, d)
        mc = mods[i, bsz:bsz + 1].reshape(1, N_MOD, d)
        g = norm_g[i].astype(F32)
        ffn1 = (g[0], ffn_w1[i, 0].astype(BF16), ffn_w3[i, 0].astype(BF16), ffn_w2[i, 0].astype(BF16))
        ffn2 = (g[2], ffn_w1[i, 1].astype(BF16), ffn_w3[i, 1].astype(BF16), ffn_w2[i, 1].astype(BF16))
        x = _ffn(x, m, *ffn1, 0)
        xc = _ffn(xc, mc, *ffn1, 0)
        if i % 2 == 0:
            x, xc = _layer_ab(x, xc, m, mc, g[1], ab_w_in[j], ab_w_out[j], na_q_g[j], na_k_g[j], na_rpb[j],
                              ssd_conv_w[j], ssd_conv_b[j], ssd_dt_bias[j], ssd_a_log[j], ssd_d[j], ssd_norm_g[j],
                              ffn2)
        else:
            assert last, "a C|D layer that is not the last one would also need the context stream's mixer output"
            lam_init = 0.8 - 0.6 * math.exp(-0.3 * i)
            x = _layer_cd(x, xc, m, mc, g[1], cd_w_in[j], cd_w_out[j], lru_conv_w[j], lru_conv_b[j], lru_wa[j],
                          lru_ba[j], lru_wx[j], lru_bx[j], lru_lambda[j], diff_q_g[j], diff_k_g[j],
                          diff_lambda[j], diff_subln_g[j], lam_init, ffn2)
    return x
```
